```python
import math
import jax, jax.numpy as jnp
from jax import lax
import numpy as np

D_MODEL = 2048
BATCH = 4
SEQ = 4096
DEPTH = 2

GRID_W = 64
CTX_LEN = 256

HG_HEADS = 4
HG_DK = 128
HG_DV = 128
HG_W = HG_HEADS * HG_DK
HG_CHUNK = 64

MLA_HEADS = 8
MLA_Q_LORA = 512
MLA_KV_LORA = 256
MLA_NOPE = 128
MLA_ROPE = 64
MLA_V = 128
MLA_W = MLA_HEADS * MLA_V
Q_BLOCK = 128

SWA_HEADS = 4
SWA_KV_HEADS = 2
SWA_HD = 128
SWA_W = SWA_HEADS * SWA_HD
SWA_KV_W = SWA_KV_HEADS * SWA_HD
SWA_WINDOW = 128
SWA_BLOCK = 128

D_MIX = HG_W + MLA_W + SWA_W

HG_COLS = 5 * HG_W
MLA_COLS = MLA_Q_LORA + MLA_KV_LORA + MLA_ROPE
SWA_COLS = SWA_W + 2 * SWA_KV_W
D_IN = HG_COLS + MLA_COLS + SWA_COLS

N_EXPERTS = 16
N_GROUPS = 4
E_PER_GROUP = N_EXPERTS // N_GROUPS
TOP_K = 2
D_FF = 512
ROUTE_SCALE = 2.5

ROPE_BASE = 10000.0
LN_EPS = 1e-5
RMS_EPS = 1e-6
DEEPNORM_ALPHA = (2.0 * DEPTH) ** 0.25
DEEPNORM_BETA = (8.0 * DEPTH) ** -0.25

kernel_name = "hybrid_hgrn2_mla_swa_moe_diffusion_block"

F32 = jnp.float32


def _layer_norm(x, g=None, b=None):
    xf = x.astype(F32)
    mu = xf.mean(-1, keepdims=True)
    var = jnp.square(xf - mu).mean(-1, keepdims=True)
    y = (xf - mu) * lax.rsqrt(var + LN_EPS)
    if g is not None:
        y = y * g.astype(F32) + b.astype(F32)
    return y.astype(x.dtype)


def _rms_norm(x, g):
    xf = x.astype(F32)
    y = xf * lax.rsqrt(jnp.mean(xf * xf, -1, keepdims=True) + RMS_EPS) * g.astype(F32)
    return y.astype(x.dtype)


def _modulate(x, shift, scale):
    return x * (1.0 + scale) + shift


def _axial_rope(rows, dim):
    row = jnp.repeat(jnp.arange(rows, dtype=jnp.int32), GRID_W)
    col = jnp.tile(jnp.arange(GRID_W, dtype=jnp.int32), rows)
    nf = dim // 4
    inv_freq = ROPE_BASE ** (-jnp.arange(nf, dtype=F32) / nf)
    pos = jnp.stack([row, col], -1).astype(F32)
    ang = pos[:, :, None] * inv_freq
    return jnp.cos(ang), jnp.sin(ang)


def _apply_rope(x, cos, sin):
    b, l, h, d = x.shape
    xr = x.reshape(b, l, h, 2, 2, d // 4).astype(F32)
    x1, x2 = xr[..., 0, :], xr[..., 1, :]
    c = cos[None, :, None]
    s = sin[None, :, None]
    out = jnp.stack([x1 * c - x2 * s, x2 * c + x1 * s], axis=-2)
    return out.reshape(b, l, h, d).astype(x.dtype)


def _hgrn2_scan(q, k, v, log_f, s0):
    b, l, h, _ = q.shape
    n = l // HG_CHUNK

    def to_chunks(t):
        return t.reshape(b, n, HG_CHUNK, h, t.shape[-1]).transpose(1, 0, 3, 2, 4)

    qc, kc, vc, gc = (to_chunks(t) for t in (q, k, v, log_f))
    mask = jnp.tril(jnp.ones((HG_CHUNK, HG_CHUNK), dtype=bool))[:, :, None]

    def step(S, inp):
        qi, ki, vi, gi = inp
        bcum = jnp.cumsum(gi, axis=-2)
        o_inter = jnp.einsum('bhtd,bhdv->bhtv', qi * jnp.exp(bcum), S)
        rel = bcum[:, :, :, None, :] - bcum[:, :, None, :, :]
        decay = jnp.exp(jnp.where(mask, rel, -jnp.inf))
        att = jnp.einsum('bhtd,bhsd,bhtsd->bhts', qi, ki, decay)
        o_intra = jnp.einsum('bhts,bhsv->bhtv', att, vi)
        btot = bcum[:, :, -1]
        k_dec = ki * jnp.exp(btot[:, :, None, :] - bcum)
        S_new = jnp.exp(btot)[..., None] * S + jnp.einsum('bhsd,bhsv->bhdv', k_dec, vi)
        return S_new, o_inter + o_intra

    S_fin, o = lax.scan(step, s0, (qc, kc, vc, gc))
    o = o.transpose(1, 0, 3, 2, 4).reshape(b, l, h, -1)
    return o, S_fin


def _hgrn2_mixer(p_lat, p_ctx, lb, norm_g, need_ctx):
    dtype = p_lat.dtype

    def parts(p):
        bb, ll, _ = p.shape
        heads = lambda t: t.reshape(bb, ll, HG_HEADS, HG_DK).astype(F32)
        q = heads(p[..., 0:HG_W])
        zf = heads(p[..., HG_W:2 * HG_W])
        zb = heads(p[..., 2 * HG_W:3 * HG_W])
        v = heads(p[..., 3 * HG_W:4 * HG_W])
        g = p[..., 4 * HG_W:5 * HG_W]
        return q, zf, zb, v, g

    def forget(z, lb_d):
        lb_d = lb_d.reshape(HG_HEADS, HG_DK)
        log_f = jnp.logaddexp(jnp.log(lb_d), jnp.log1p(-lb_d) + jax.nn.log_sigmoid(z))
        k = (1.0 - lb_d) * jax.nn.sigmoid(-z)
        return log_f, k

    def readout(o, g):
        bb, ll = o.shape[:2]
        o = _rms_norm(o, norm_g).reshape(bb, ll, HG_W)
        return (o * jax.nn.silu(g.astype(F32))).astype(dtype)

    flip = lambda t: jnp.flip(t, axis=1)
    q, zf, zb, v, g = parts(p_lat)
    qc, zfc, zbc, vc, gc = parts(p_ctx)
    s0 = jnp.zeros((q.shape[0], HG_HEADS, HG_DK, HG_DV), F32)

    lf, kf = forget(zf, lb[0])
    lfc, kfc = forget(zfc, lb[0])
    oc_f, s_f = _hgrn2_scan(qc, kfc, vc, lfc, s0)
    o_f, _ = _hgrn2_scan(q, kf, v, lf, s_f)
    lbk, kbk = forget(zb, lb[1])
    lbc, kbc = forget(zbc, lb[1])
    oc_b, s_b = _hgrn2_scan(flip(qc), flip(kbc), flip(vc), flip(lbc), s0)
    o_b, _ = _hgrn2_scan(flip(q), flip(kbk), flip(v), flip(lbk), s_b)

    out = readout(o_f + flip(o_b), g)
    out_c = readout(oc_f + flip(oc_b), gc) if need_ctx else None
    return out, out_c


def _dense_attention_blocks(q, k, v, scale):
    b, l, h, d = q.shape
    nb = l // Q_BLOCK
    qb = q.reshape(b, nb, Q_BLOCK, h, d).transpose(1, 0, 2, 3, 4)

    def one(qi):
        s = jnp.einsum('bqhd,bkhd->bhqk', qi, k).astype(F32) * scale
        p = jax.nn.softmax(s, axis=-1).astype(v.dtype)
        return jnp.einsum('bhqk,bkhd->bqhd', p, v)

    o = lax.map(one, qb)
    return o.transpose(1, 0, 2, 3, 4).reshape(b, l, h, -1)


def _mla_mixer(p_lat, p_ctx, q_norm, kv_norm, w_uq, w_ukv, rope, need_ctx):
    def qkv(p, rp):
        b, l, _ = p.shape
        cq = _rms_norm(p[..., :MLA_Q_LORA], q_norm)
        ckv = _rms_norm(p[..., MLA_Q_LORA:MLA_Q_LORA + MLA_KV_LORA], kv_norm)
        k_rope = p[..., MLA_Q_LORA + MLA_KV_LORA:][:, :, None, :]
        qf = (cq @ w_uq).reshape(b, l, MLA_HEADS, MLA_NOPE + MLA_ROPE)
        kv = (ckv @ w_ukv).reshape(b, l, MLA_HEADS, MLA_NOPE + MLA_V)
        q_nope, q_rope = qf[..., :MLA_NOPE], qf[..., MLA_NOPE:]
        k_nope, v = kv[..., :MLA_NOPE], kv[..., MLA_NOPE:]
        if rp is not None:
            q_rope = _apply_rope(q_rope, *rp)
            k_rope = _apply_rope(k_rope, *rp)
        q = jnp.concatenate([q_nope, q_rope], -1)
        k = jnp.concatenate([k_nope, jnp.broadcast_to(k_rope, (b, l, MLA_HEADS, MLA_ROPE))], -1)
        return q, k, v

    scale = (MLA_NOPE + MLA_ROPE) ** -0.5
    q, k, v = qkv(p_lat, rope)
    qc, kc, vc = qkv(p_ctx, None)
    k_all = jnp.concatenate([k, kc], axis=1)
    v_all = jnp.concatenate([v, vc], axis=1)
    b, l = q.shape[:2]
    out = _dense_attention_blocks(q, k_all, v_all, scale).reshape(b, l, MLA_W)
    out_c = None
    if need_ctx:
        out_c = _dense_attention_blocks(qc, kc, vc, scale).reshape(b, qc.shape[1], MLA_W)
    return out, out_c


def _swa_mixer(p_lat, p_ctx, sink, rope, need_ctx):
    G = SWA_HEADS // SWA_KV_HEADS

    def split(p):
        bb, ll, _ = p.shape
        q = p[..., :SWA_W].reshape(bb, ll, SWA_HEADS, SWA_HD)
        k = p[..., SWA_W:SWA_W + SWA_KV_W].reshape(bb, ll, SWA_KV_HEADS, SWA_HD)
        v = p[..., SWA_W + SWA_KV_W:].reshape(bb, ll, SWA_KV_HEADS, SWA_HD)
        return q, k, v

    q, k, v = split(p_lat)
    qc, kc, vc = split(p_ctx)
    q = _apply_rope(q, *rope)
    k = _apply_rope(k, *rope)
    b, l = q.shape[:2]
    lc = qc.shape[1]
    scale = SWA_HD ** -0.5
    sink_l = sink.reshape(SWA_KV_HEADS, G).astype(F32)

    nb = l // SWA_BLOCK
    nband = 3 * SWA_BLOCK
    qb = q.reshape(b, nb, SWA_BLOCK, SWA_KV_HEADS, G, SWA_HD)
    pad = ((0, 0), (SWA_BLOCK, SWA_BLOCK), (0, 0), (0, 0))
    kp = jnp.pad(k, pad).reshape(b, nb + 2, SWA_BLOCK, SWA_KV_HEADS, SWA_HD)
    vp = jnp.pad(v, pad).reshape(b, nb + 2, SWA_BLOCK, SWA_KV_HEADS, SWA_HD)
    kband = jnp.concatenate([kp[:, :-2], kp[:, 1:-1], kp[:, 2:]], axis=2)
    vband = jnp.concatenate([vp[:, :-2], vp[:, 1:-1], vp[:, 2:]], axis=2)

    a = jnp.arange(SWA_BLOCK)[:, None]
    j = jnp.arange(nband)[None, :]
    in_window = jnp.abs(j - SWA_BLOCK - a) <= SWA_WINDOW
    key_pos = (jnp.arange(nb)[:, None] - 1) * SWA_BLOCK + jnp.arange(nband)[None, :]
    valid = (key_pos >= 0) & (key_pos < l)
    mask = in_window[None] & valid[:, None, :]

    s_band = jnp.einsum('bnqkgd,bnjkd->bnkgqj', qb, kband).astype(F32) * scale
    s_band = jnp.where(mask[None, :, None, None], s_band, -jnp.inf)
    s_ctx = jnp.einsum('bnqkgd,bmkd->bnkgqm', qb, kc).astype(F32) * scale
    s_sink = jnp.broadcast_to(sink_l[None, None, :, :, None, None], s_band.shape[:-1] + (1,))
    prob = jax.nn.softmax(jnp.concatenate([s_band, s_ctx, s_sink], -1), axis=-1)
    p_band = prob[..., :nband].astype(v.dtype)
    p_ctx = prob[..., nband:nband + lc].astype(v.dtype)
    o = (jnp.einsum('bnkgqj,bnjkd->bnqkgd', p_band, vband)
         + jnp.einsum('bnkgqm,bmkd->bnqkgd', p_ctx, vc))
    out = o.reshape(b, l, SWA_W)

    out_c = None
    if need_ctx:
        qcg = qc.reshape(b, lc, SWA_KV_HEADS, G, SWA_HD)
        s = jnp.einsum('bqkgd,bmkd->bkgqm', qcg, kc).astype(F32) * scale
        ss = jnp.broadcast_to(sink_l[None, :, :, None, None], s.shape[:-1] + (1,))
        pc = jax.nn.softmax(jnp.concatenate([s, ss], -1), axis=-1)[..., :lc].astype(vc.dtype)
        out_c = jnp.einsum('bkgqm,bmkd->bqkgd', pc, vc).reshape(b, lc, SWA_W)
    return out, out_c


def _moe(h, router_w, router_b, w_gate, w_up, w_down):
    aff = jax.nn.sigmoid(jnp.einsum('bld,de->ble', h, router_w).astype(F32))
    sel = aff + router_b.astype(F32)
    grp = sel.reshape(sel.shape[:-1] + (N_GROUPS, E_PER_GROUP))
    grp_score = lax.top_k(grp, TOP_K)[0].sum(-1)
    g_idx = jnp.argmax(grp_score, axis=-1)
    grp_mask = jnp.repeat(g_idx[..., None] == jnp.arange(N_GROUPS), E_PER_GROUP, axis=-1)
    _, e_idx = lax.top_k(jnp.where(grp_mask, sel, -jnp.inf), TOP_K)
    w = jnp.take_along_axis(aff, e_idx, axis=-1)
    w = w / w.sum(-1, keepdims=True) * ROUTE_SCALE
    gates = jnp.sum(jax.nn.one_hot(e_idx, N_EXPERTS, dtype=F32) * w[..., None], axis=-2)
    y = jnp.zeros_like(h)
    for e in range(N_EXPERTS):
        act = jax.nn.silu(h @ w_gate[e]) * (h @ w_up[e])
        y = y + gates[..., e:e + 1].astype(h.dtype) * (act @ w_down[e])
    return y


def setup_inputs(seed: int = 0) -> dict:
    key = jax.random.key(seed)
    ks = jax.random.split(key, 24)
    D = D_MODEL

    def nrm(k, shape, s):
        return jax.random.normal(k, shape, F32) * s

    return {
        "x": nrm(ks[0], (BATCH, SEQ, D), 1.0),
        "c": nrm(ks[1], (BATCH, D), 1.0),
        "ctx": nrm(ks[2], (BATCH, CTX_LEN, D), 1.0),
        "c_ctx": nrm(ks[3], (D,), 1.0),
        "w_ada": nrm(ks[4], (DEPTH, D, 6 * D), 0.5 * D ** -0.5),
        "b_ada": nrm(ks[5], (DEPTH, 6 * D), 0.02),
        "w_in": nrm(ks[6], (DEPTH, D, D_IN), D ** -0.5),
        "w_out": nrm(ks[7], (DEPTH, D_MIX, D), DEEPNORM_BETA * D_MIX ** -0.5),
        "hg_lb_logits": nrm(ks[8], (DEPTH, 2, HG_W), 1.0),
        "hg_norm_g": 1.0 + nrm(ks[9], (DEPTH, HG_DV), 0.02),
        "mla_q_norm": 1.0 + nrm(ks[10], (DEPTH, MLA_Q_LORA), 0.02),
        "mla_kv_norm": 1.0 + nrm(ks[11], (DEPTH, MLA_KV_LORA), 0.02),
        "mla_w_uq": nrm(ks[12], (DEPTH, MLA_Q_LORA, MLA_HEADS * (MLA_NOPE + MLA_ROPE)), MLA_Q_LORA ** -0.5),
        "mla_w_ukv": nrm(ks[13], (DEPTH, MLA_KV_LORA, MLA_HEADS * (MLA_NOPE + MLA_V)), MLA_KV_LORA ** -0.5),
        "swa_sink": nrm(ks[14], (DEPTH, SWA_HEADS), 1.0),
        "ln1_g": 1.0 + nrm(ks[15], (DEPTH, D), 0.02),
        "ln1_b": nrm(ks[16], (DEPTH, D), 0.02),
        "ln2_g": 1.0 + nrm(ks[17], (DEPTH, D), 0.02),
        "ln2_b": nrm(ks[18], (DEPTH, D), 0.02),
        "router_w": nrm(ks[19], (D, N_EXPERTS), D ** -0.5),
        "router_b": nrm(ks[20], (N_EXPERTS,), 0.01),
        "moe_w_gate": nrm(ks[21], (DEPTH, N_EXPERTS, D, D_FF), D ** -0.5),
        "moe_w_up": nrm(ks[22], (DEPTH, N_EXPERTS, D, D_FF), D ** -0.5),
        "moe_w_down": nrm(ks[23], (DEPTH, N_EXPERTS, D_FF, D), DEEPNORM_BETA * D_FF ** -0.5),
    }


def reference(x, c, ctx, c_ctx, w_ada, b_ada, w_in, w_out, hg_lb_logits, hg_norm_g,
              mla_q_norm, mla_kv_norm, mla_w_uq, mla_w_ukv, swa_sink,
              ln1_g, ln1_b, ln2_g, ln2_b, router_w, router_b,
              moe_w_gate, moe_w_up, moe_w_down):
    L = x.shape[1]
    rows = L // GRID_W
    rope_swa = _axial_rope(rows, SWA_HD)
    rope_mla = _axial_rope(rows, MLA_ROPE)

    lb_all = jnp.cumsum(jax.nn.softmax(hg_lb_logits.astype(F32), axis=0), axis=0)
    lb_all = lb_all - lb_all[0:1]

    silu_c = jax.nn.silu(c)
    silu_cc = jax.nn.silu(c_ctx)
    xc = ctx
    o1, o2 = HG_COLS, HG_COLS + MLA_COLS

    for layer in range(DEPTH):
        need_ctx = layer < DEPTH - 1
        mod = silu_c @ w_ada[layer] + b_ada[layer]
        mod_c = silu_cc @ w_ada[layer] + b_ada[layer]
        sh1, sc1, g1, sh2, sc2, g2 = jnp.split(mod[:, None, :], 6, axis=-1)
        sh1c, sc1c, g1c, sh2c, sc2c, g2c = jnp.split(mod_c, 6, axis=-1)

        h = _modulate(_layer_norm(x), sh1, sc1)
        hc = _modulate(_layer_norm(xc), sh1c, sc1c)
        p = h @ w_in[layer]
        pc = hc @ w_in[layer]
        o_hg, oc_hg = _hgrn2_mixer(p[..., :o1], pc[..., :o1], lb_all[layer], hg_norm_g[layer], need_ctx)
        o_mla, oc_mla = _mla_mixer(p[..., o1:o2], pc[..., o1:o2], mla_q_norm[layer], mla_kv_norm[layer],
                                   mla_w_uq[layer], mla_w_ukv[layer], rope_mla, need_ctx)
        o_swa, oc_swa = _swa_mixer(p[..., o2:], pc[..., o2:], swa_sink[layer], rope_swa, need_ctx)
        mix = jnp.concatenate([o_hg, o_mla, o_swa], axis=-1) @ w_out[layer]
        x = _layer_norm(DEEPNORM_ALPHA * x + g1 * mix, ln1_g[layer], ln1_b[layer])
        if need_ctx:
            mixc = jnp.concatenate([oc_hg, oc_mla, oc_swa], axis=-1) @ w_out[layer]
            xc = _layer_norm(DEEPNORM_ALPHA * xc + g1c * mixc, ln1_g[layer], ln1_b[layer])

        h = _modulate(_layer_norm(x), sh2, sc2)
        y = _moe(h, router_w, router_b, moe_w_gate[layer], moe_w_up[layer], moe_w_down[layer])
        x = _layer_norm(DEEPNORM_ALPHA * x + g2 * y, ln2_g[layer], ln2_b[layer])
        if need_ctx:
            hc = _modulate(_layer_norm(xc), sh2c, sc2c)
            yc = _moe(hc, router_w, router_b, moe_w_gate[layer], moe_w_up[layer], moe_w_down[layer])
            xc = _layer_norm(DEEPNORM_ALPHA * xc + g2c * yc, ln2_g[layer], ln2_b[layer])
    return x
```

```python
import functools
import math

import jax
import jax.numpy as jnp
from jax import lax
from jax.experimental import pallas as pl
from jax.experimental.pallas import tpu as pltpu

F32 = jnp.float32
BF16 = jnp.bfloat16

GRID_W = 64
HG_HEADS = 4
HG_DK = 128
HG_W = HG_HEADS * HG_DK
HG_CHUNK = 64
HG_SUB = 16
MLA_HEADS = 8
MLA_Q_LORA = 512
MLA_KV_LORA = 256
MLA_NOPE = 128
MLA_ROPE = 64
MLA_V = 128
MLA_QK_PAD = 256
SWA_HEADS = 4
SWA_KV_HEADS = 2
SWA_HD = 128
SWA_W = SWA_HEADS * SWA_HD
SWA_KV_W = SWA_KV_HEADS * SWA_HD
SWA_WINDOW = 128
SWA_BLOCK = 128
N_EXPERTS = 16
N_GROUPS = 4
E_PER_GROUP = 4
N_PAIRS = 6
N_BUCKETS = N_GROUPS * N_PAIRS
ROUTE_SCALE = 2.5
ROPE_BASE = 10000.0
LN_EPS = 1e-5
RMS_EPS = 1e-6
LANES = 128
VMEM_LIMIT = 56 * 1024 * 1024

ATT_SQ, ATT_SK, ATT_SV = 0, SWA_W, SWA_W + SWA_KV_W
ATT_CQ = SWA_W + 2 * SWA_KV_W
ATT_CKV = ATT_CQ + MLA_Q_LORA
ATT_KR = ATT_CKV + MLA_KV_LORA
ATT_COLS = ATT_KR + LANES


def _cparams(sem):
    return pltpu.CompilerParams(dimension_semantics=sem, vmem_limit_bytes=VMEM_LIMIT)


def _dot(a, b):
    return jnp.dot(a, b, preferred_element_type=F32)


def _dot_nt(a, b):
    return lax.dot_general(a, b, (((1,), (1,)), ((), ())), preferred_element_type=F32)


def _dot_tn(a, b):
    return lax.dot_general(a, b, (((0,), (0,)), ((), ())), preferred_element_type=F32)


def _sigmoid(x):
    return 1.0 / (1.0 + jnp.exp(-x))


def _silu(x):
    return x * _sigmoid(x)


def _ln_rows(x):
    mu = jnp.mean(x, axis=-1, keepdims=True)
    xc = x - mu
    var = jnp.mean(xc * xc, axis=-1, keepdims=True)
    return xc * lax.rsqrt(var + LN_EPS)


def _rope_partner(x, half):
    lane = lax.broadcasted_iota(jnp.int32, x.shape, x.ndim - 1)
    first = (lane & half) == 0
    n = x.shape[-1]
    return jnp.where(first, pltpu.roll(x, n - half, x.ndim - 1), pltpu.roll(x, half, x.ndim - 1))


def _ada_kernel(c_ref, w_ref, b_ref, o_ref):
    s = _silu(c_ref[...])
    o_ref[...] = _dot(s.astype(BF16), w_ref[...].astype(BF16)) + b_ref[...]


def _ada(c8, w_ada, b_ada):
    depth, d, n = w_ada.shape
    tn = 1024 if n % 1024 == 0 else n
    return pl.pallas_call(
        _ada_kernel,
        grid=(depth, n // tn),
        in_specs=[pl.BlockSpec((8, d), lambda l, j: (0, 0)),
                  pl.BlockSpec((None, d, tn), lambda l, j: (l, 0, j)),
                  pl.BlockSpec((None, 1, tn), lambda l, j: (l, 0, j))],
        out_specs=pl.BlockSpec((None, 8, tn), lambda l, j: (l, 0, j)),
        out_shape=jax.ShapeDtypeStruct((depth, 8, n), F32),
        compiler_params=_cparams(("arbitrary", "arbitrary")),
        name="ada_mod",
    )(c8, w_ada, b_ada.reshape(depth, 1, n))


class _Rows:
    def __init__(self, batch, seq, ctx_len, n_rows, tm):
        self.batch, self.seq, self.ctx_len, self.n_rows, self.tm = batch, seq, ctx_len, n_rows, tm
        self.n_tiles = n_rows // tm
        self.lat_tiles = batch * seq // tm
        self.tiles_per_batch = seq // tm

    def mod_row(self, i):
        return jnp.where(i < self.lat_tiles, i // self.tiles_per_batch, self.batch)


def _pick_tm(seq, ctx_rows, cap):
    for tm in (1024, 512, 256, 128):
        if tm <= cap and seq % tm == 0 and ctx_rows % tm == 0:
            return tm
    raise ValueError("unsupported sequence / context lengths")


def _mod_spec(rows, chunk, d):
    return pl.BlockSpec((None, 1, d), lambda i, *_: (rows.mod_row(i) * 6 + chunk, 0, 0))


def _lnmod_kernel(x_ref, sh_ref, sc_ref, h_ref):
    y = _ln_rows(x_ref[...])
    h_ref[...] = (y * (1.0 + sc_ref[...]) + sh_ref[...]).astype(h_ref.dtype)


def _lnmod(x, mod, rows, chunk0):
    d = x.shape[1]
    tm = rows.tm
    return pl.pallas_call(
        _lnmod_kernel,
        grid=(rows.n_tiles,),
        in_specs=[pl.BlockSpec((tm, d), lambda i: (i, 0)),
                  _mod_spec(rows, chunk0, d), _mod_spec(rows, chunk0 + 1, d)],
        out_specs=pl.BlockSpec((tm, d), lambda i: (i, 0)),
        out_shape=jax.ShapeDtypeStruct((rows.n_rows, d), BF16),
        compiler_params=_cparams(("arbitrary",)),
        name="ln_modulate",
    )(x, mod, mod)


def _mm_raw_kernel(h_ref, w_ref, o_ref):
    o_ref[...] = _dot(h_ref[...], w_ref[...]).astype(o_ref.dtype)


def _mm_gate_kernel(h_ref, w_ref, loglb_ref, log1m_ref, onem_ref, lf_ref, k_ref):
    z = _dot(h_ref[...], w_ref[...])
    e = jnp.exp(-jnp.abs(z))
    r = 1.0 / (1.0 + e)
    log_sig = jnp.minimum(z, 0.0) + jnp.log(r)
    sig_neg = jnp.where(z >= 0.0, e * r, r)
    a = loglb_ref[...]
    b = log1m_ref[...] + log_sig
    lf_ref[...] = jnp.maximum(a, b) + jnp.log(1.0 + jnp.exp(-jnp.abs(a - b)))
    k_ref[...] = onem_ref[...] * sig_neg


def _mm_att_kernel(h_ref, w_ref, cs_ref, ss_ref, cm_ref, sm_ref, o_ref):
    p = _dot(h_ref[...], w_ref[...])
    cs, ss = cs_ref[...], ss_ref[...]
    swa_scale = SWA_HD ** -0.5
    for hd in range(SWA_HEADS + SWA_KV_HEADS):
        c0 = ATT_SQ + hd * SWA_HD
        x = p[:, c0:c0 + SWA_HD]
        y = x * cs + _rope_partner(x, SWA_HD // 4) * ss
        if hd < SWA_HEADS:
            y = y * swa_scale
        o_ref[:, c0:c0 + SWA_HD] = y.astype(o_ref.dtype)
    o_ref[:, ATT_SV:ATT_KR] = p[:, ATT_SV:ATT_KR].astype(o_ref.dtype)
    x = p[:, ATT_KR:ATT_KR + LANES]
    y = x * cm_ref[...] + _rope_partner(x, MLA_ROPE // 4) * sm_ref[...]
    o_ref[:, ATT_KR:ATT_KR + LANES] = y.astype(o_ref.dtype)


def _row_call(kernel, rows, row_inputs, const_inputs, out_widths, out_dtypes, name):
    tm = rows.tm
    in_specs = [pl.BlockSpec((tm, a.shape[1]), lambda i: (i, 0)) for a in row_inputs]
    in_specs += [pl.BlockSpec(a.shape, lambda i, nd=a.ndim: (0,) * nd) for a in const_inputs]
    out_specs = [pl.BlockSpec((tm, w), lambda i: (i, 0)) for w in out_widths]
    out_shape = [jax.ShapeDtypeStruct((rows.n_rows, w), dt) for w, dt in zip(out_widths, out_dtypes)]
    return pl.pallas_call(
        kernel, grid=(rows.n_tiles,), in_specs=in_specs, out_specs=out_specs, out_shape=out_shape,
        compiler_params=_cparams(("arbitrary",)), name=name,
    )(*row_inputs, *const_inputs)


def _hg_chunk(q, k, lf, v, st, reverse):
    c, s = HG_CHUNK, HG_SUB
    ri = lax.broadcasted_iota(jnp.int32, (c, c), 0)
    ci = lax.broadcasted_iota(jnp.int32, (c, c), 1)
    tri = jnp.where((ci >= ri) if reverse else (ci <= ri), 1.0, 0.0).astype(BF16)
    hi = lf.astype(BF16)
    r1 = lf - hi.astype(F32)
    mid = r1.astype(BF16)
    lo = (r1 - mid.astype(F32)).astype(BF16)
    b = _dot(tri, hi) + _dot(tri, mid) + _dot(tri, lo)
    btot = b[0:1, :] if reverse else b[c - 1:c, :]
    o_inter = _dot_nt((q * jnp.exp(b)).astype(BF16), st.astype(BF16))
    k_dec = (k * jnp.exp(btot - b)).astype(BF16)
    v16 = v.astype(BF16)
    st_new = st * jnp.exp(btot) + _dot_tn(v16, k_dec)
    outs = []
    nsub = c // s
    for i in range(nsub):
        r0 = i * s
        if reverse:
            k0, k1 = r0, c
            ref = b[r0 + s:r0 + s + 1, :] if i < nsub - 1 else jnp.zeros_like(btot)
        else:
            k0, k1 = 0, r0 + s
            ref = b[r0 - 1:r0, :] if i > 0 else jnp.zeros_like(btot)
        qt = (q[r0:r0 + s] * jnp.exp(b[r0:r0 + s] - ref)).astype(BF16)
        kt = (k[k0:k1] * jnp.exp(ref - b[k0:k1])).astype(BF16)
        att = _dot_nt(qt, kt)
        rr = lax.broadcasted_iota(jnp.int32, att.shape, 0) + r0
        cc = lax.broadcasted_iota(jnp.int32, att.shape, 1) + k0
        keep = (cc >= rr) if reverse else (cc <= rr)
        att = jnp.where(keep, att, 0.0).astype(BF16)
        outs.append(o_inter[r0:r0 + s] + _dot(att, v16[k0:k1]))
    return jnp.concatenate(outs, axis=0), st_new


def _hg_scan_kernel(*refs, reverse, n_chunks, readout):
    if readout:
        q_ref, k_ref, lf_ref, v_ref, of_ref, g_ref, ng_ref, o_ref, st_ref = refs
    else:
        q_ref, k_ref, lf_ref, v_ref, o_ref, st_ref = refs

    @pl.when(pl.program_id(1) == 0)
    def _():
        st_ref[...] = jnp.zeros_like(st_ref)

    for cc in range(n_chunks):
        c = n_chunks - 1 - cc if reverse else cc
        rs = slice(c * HG_CHUNK, (c + 1) * HG_CHUNK)
        for hd in range(HG_HEADS):
            cs = slice(hd * HG_DK, (hd + 1) * HG_DK)
            o, st = _hg_chunk(q_ref[rs, cs], k_ref[rs, cs], lf_ref[rs, cs], v_ref[rs, cs], st_ref[hd], reverse)
            st_ref[hd] = st
            if readout:
                o = o + of_ref[rs, cs]
                o = o * lax.rsqrt(jnp.mean(o * o, axis=-1, keepdims=True) + RMS_EPS) * ng_ref[...]
                o = o * _silu(g_ref[rs, cs])
            o_ref[rs, cs] = o.astype(o_ref.dtype)


def _hg_scan(q, k, lf, v, batch, seq, ctx_len, reverse, readout_args=None):
    n_rows = q.shape[0]
    tb = ctx_len
    nl = seq // tb
    ctx0 = batch * seq // tb

    def row_block(b, i):
        lat = b * nl + (nl - i if reverse else i - 1)
        return jnp.where(i == 0, ctx0 + b, lat)

    spec = pl.BlockSpec((tb, HG_W), lambda b, i: (row_block(b, i), 0))
    readout = readout_args is not None
    inputs = [q, k, lf, v]
    in_specs = [spec] * 4
    if readout:
        o_f, g, norm_g = readout_args
        inputs += [o_f, g, norm_g]
        in_specs += [spec, spec, pl.BlockSpec((1, HG_DK), lambda b, i: (0, 0))]
    kern = functools.partial(_hg_scan_kernel, reverse=reverse, n_chunks=tb // HG_CHUNK, readout=readout)
    return pl.pallas_call(
        kern, grid=(batch, nl + 1), in_specs=in_specs, out_specs=spec,
        out_shape=jax.ShapeDtypeStruct((n_rows, HG_W), BF16 if readout else F32),
        scratch_shapes=[pltpu.VMEM((HG_HEADS, HG_DK, HG_DK), F32)],
        compiler_params=_cparams(("arbitrary", "arbitrary")),
        name="hgrn2_scan_bwd_readout" if readout else "hgrn2_scan_fwd",
    )(*inputs)


def _mla_proj_kernel(cq_ref, ckv_ref, kr_ref, qn_ref, kvn_ref, wq_ref, wkv_ref, cm_ref, sm_ref,
                     q_ref, k_ref, v_ref):
    def rms(x, g):
        xf = x.astype(F32)
        return (xf * lax.rsqrt(jnp.mean(xf * xf, axis=-1, keepdims=True) + RMS_EPS) * g).astype(BF16)

    scale = (MLA_NOPE + MLA_ROPE) ** -0.5
    qf = _dot(rms(cq_ref[...], qn_ref[...]), wq_ref[...])
    kv = _dot(rms(ckv_ref[...], kvn_ref[...]), wkv_ref[...])
    cm, sm = cm_ref[...], sm_ref[...]
    kr = kr_ref[...]
    for hd in range(MLA_HEADS):
        c0 = hd * MLA_QK_PAD
        q_ref[:, c0:c0 + LANES] = (qf[:, c0:c0 + LANES] * scale).astype(q_ref.dtype)
        x = qf[:, c0 + LANES:c0 + 2 * LANES]
        y = (x * cm + _rope_partner(x, MLA_ROPE // 4) * sm) * scale
        q_ref[:, c0 + LANES:c0 + 2 * LANES] = y.astype(q_ref.dtype)
        k_ref[:, c0:c0 + LANES] = kv[:, hd * LANES:(hd + 1) * LANES].astype(k_ref.dtype)
        k_ref[:, c0 + LANES:c0 + 2 * LANES] = kr
    v_ref[...] = kv[:, MLA_HEADS * LANES:].astype(v_ref.dtype)


def _mla_proj(p_att, q_norm, kv_norm, wq, wkv, cm, sm, rows):
    tm = rows.tm
    n = rows.n_rows
    hq = MLA_HEADS * MLA_QK_PAD

    def col(width, off):
        return pl.BlockSpec((tm, width), lambda i: (i, off // width))

    def whole(a):
        return pl.BlockSpec(a.shape, lambda i: (0, 0))

    return pl.pallas_call(
        _mla_proj_kernel, grid=(rows.n_tiles,),
        in_specs=[col(MLA_Q_LORA, ATT_CQ), col(MLA_KV_LORA, ATT_CKV), col(LANES, ATT_KR),
                  whole(q_norm), whole(kv_norm), whole(wq), whole(wkv),
                  pl.BlockSpec((tm, LANES), lambda i: (i, 0)), pl.BlockSpec((tm, LANES), lambda i: (i, 0))],
        out_specs=[pl.BlockSpec((tm, hq), lambda i: (i, 0)), pl.BlockSpec((tm, hq), lambda i: (i, 0)),
                   pl.BlockSpec((tm, MLA_HEADS * MLA_V), lambda i: (i, 0))],
        out_shape=[jax.ShapeDtypeStruct((n, hq), BF16), jax.ShapeDtypeStruct((n, hq), BF16),
                   jax.ShapeDtypeStruct((n, MLA_HEADS * MLA_V), BF16)],
        compiler_params=_cparams(("arbitrary",)), name="mla_proj",
    )(p_att, p_att, p_att, q_norm, kv_norm, wq, wkv, cm, sm)


def _mla_attn_kernel(q_ref, kl_ref, vl_ref, kc_ref, vc_ref, o_ref, *, nq, ctx_queries):
    q = q_ref[...]

    def attend(with_lat):
        s_c = _dot_nt(q, kc_ref[...])
        m = jnp.max(s_c, axis=-1, keepdims=True)
        if with_lat:
            s_l = _dot_nt(q, kl_ref[...])
            m = jnp.maximum(m, jnp.max(s_l, axis=-1, keepdims=True))
        p_c = jnp.exp(s_c - m)
        den = jnp.sum(p_c, axis=-1, keepdims=True)
        acc = _dot(p_c.astype(BF16), vc_ref[...])
        if with_lat:
            p_l = jnp.exp(s_l - m)
            den = den + jnp.sum(p_l, axis=-1, keepdims=True)
            acc = acc + _dot(p_l.astype(BF16), vl_ref[...])
        o_ref[...] = (acc / den).astype(o_ref.dtype)

    if ctx_queries:
        i = pl.program_id(2)
        pl.when(i < nq)(lambda: attend(True))
        pl.when(i >= nq)(lambda: attend(False))
    else:
        attend(True)


def _mla_attn(q, k, v, batch, seq, ctx_len, ctx_queries):
    tq = ctx_len
    nq = seq // tq
    ctx0 = batch * seq // ctx_len
    q_row = lambda b, h, i: jnp.where(i < nq, b * nq + i, ctx0 + b)
    n_rows = batch * seq + (batch * ctx_len if ctx_queries else 0)
    return pl.pallas_call(
        functools.partial(_mla_attn_kernel, nq=nq, ctx_queries=ctx_queries),
        grid=(batch, MLA_HEADS, nq + (1 if ctx_queries else 0)),
        in_specs=[pl.BlockSpec((tq, MLA_QK_PAD), lambda b, h, i: (q_row(b, h, i), h)),
                  pl.BlockSpec((seq, MLA_QK_PAD), lambda b, h, i: (b, h)),
                  pl.BlockSpec((seq, MLA_V), lambda b, h, i: (b, h)),
                  pl.BlockSpec((ctx_len, MLA_QK_PAD), lambda b, h, i: (ctx0 + b, h)),
                  pl.BlockSpec((ctx_len, MLA_V), lambda b, h, i: (ctx0 + b, h))],
        out_specs=pl.BlockSpec((tq, MLA_V), lambda b, h, i: (q_row(b, h, i), h)),
        out_shape=jax.ShapeDtypeStruct((n_rows, MLA_HEADS * MLA_V), BF16),
        compiler_params=_cparams(("arbitrary", "arbitrary", "arbitrary")),
        name="mla_attn",
    )(q, k, v, k, v)


def _swa_kernel(sink_ref, q_ref, kp_ref, kn_ref, kx_ref, kc_ref, vp_ref, vn_ref, vx_ref, vc_ref, o_ref,
                *, nb, seq, ctx_queries):
    n = pl.program_id(1)
    g = SWA_HEADS // SWA_KV_HEADS
    blk = SWA_BLOCK

    def attend(band):
        if band:
            a = lax.broadcasted_iota(jnp.int32, (g * blk, 3 * blk), 0) & (blk - 1)
            j = lax.broadcasted_iota(jnp.int32, (g * blk, 3 * blk), 1)
            key_pos = (n - 1) * blk + j
            keep = (jnp.abs(j - blk - a) <= SWA_WINDOW) & (key_pos >= 0) & (key_pos < seq)
        for kh in range(SWA_KV_HEADS):
            ks = slice(kh * SWA_HD, (kh + 1) * SWA_HD)
            qs = jnp.concatenate(
                [q_ref[:, (kh * g + gi) * SWA_HD:(kh * g + gi + 1) * SWA_HD] for gi in range(g)], axis=0)
            row = lax.broadcasted_iota(jnp.int32, (g * blk, 1), 0)
            sink = jnp.zeros((g * blk, 1), F32)
            for gi in range(g):
                sink = jnp.where((row >= gi * blk) & (row < (gi + 1) * blk), sink_ref[kh * g + gi], sink)
            s_c = _dot_nt(qs, kc_ref[:, ks])
            m = jnp.maximum(jnp.max(s_c, axis=-1, keepdims=True), sink)
            if band:
                kb = jnp.concatenate([kp_ref[:, ks], kn_ref[:, ks], kx_ref[:, ks]], axis=0)
                vb = jnp.concatenate([vp_ref[:, ks], vn_ref[:, ks], vx_ref[:, ks]], axis=0)
                s_b = jnp.where(keep, _dot_nt(qs, kb), -jnp.inf)
                m = jnp.maximum(m, jnp.max(s_b, axis=-1, keepdims=True))
            p_c = jnp.exp(s_c - m)
            den = jnp.sum(p_c, axis=-1, keepdims=True) + jnp.exp(sink - m)
            acc = _dot(p_c.astype(BF16), vc_ref[:, ks])
            if band:
                p_b = jnp.exp(s_b - m)
                den = den + jnp.sum(p_b, axis=-1, keepdims=True)
                acc = acc + _dot(p_b.astype(BF16), vb)
            o = acc / den
            for gi in range(g):
                c0 = (kh * g + gi) * SWA_HD
                o_ref[:, c0:c0 + SWA_HD] = o[gi * blk:(gi + 1) * blk].astype(o_ref.dtype)

    if ctx_queries:
        pl.when(n < nb)(lambda: attend(True))
        pl.when(n >= nb)(lambda: attend(False))
    else:
        attend(True)


def _swa(p_att, sink, batch, seq, ctx_len, ctx_queries):
    blk = SWA_BLOCK
    nb, ncb = seq // blk, ctx_len // blk
    ctx0 = batch * seq // ctx_len
    kcol, vcol = ATT_SK // SWA_KV_W, ATT_SV // SWA_KV_W
    q_row = lambda b, n: jnp.where(n < nb, b * nb + n, batch * nb + b * ncb + (n - nb))
    n_rows = batch * seq + (batch * ctx_len if ctx_queries else 0)

    def nbr(col, d):
        return pl.BlockSpec((blk, SWA_KV_W), lambda b, n: (b * nb + jnp.clip(n + d, 0, nb - 1), col))

    ctx_k = pl.BlockSpec((ctx_len, SWA_KV_W), lambda b, n: (ctx0 + b, kcol))
    ctx_v = pl.BlockSpec((ctx_len, SWA_KV_W), lambda b, n: (ctx0 + b, vcol))
    return pl.pallas_call(
        functools.partial(_swa_kernel, nb=nb, seq=seq, ctx_queries=ctx_queries),
        grid=(batch, nb + (ncb if ctx_queries else 0)),
        in_specs=[pl.BlockSpec(memory_space=pltpu.SMEM),
                  pl.BlockSpec((blk, SWA_W), lambda b, n: (q_row(b, n), 0)),
                  nbr(kcol, -1), nbr(kcol, 0), nbr(kcol, 1), ctx_k, nbr(vcol, -1), nbr(vcol, 0), nbr(vcol, 1), ctx_v],
        out_specs=pl.BlockSpec((blk, SWA_W), lambda b, n: (q_row(b, n), 0)),
        out_shape=jax.ShapeDtypeStruct((n_rows, SWA_W), BF16),
        compiler_params=_cparams(("arbitrary", "arbitrary")),
        name="swa_attn",
    )(sink, *([p_att] * 9))


def _route(logits, rb):
    aff = _sigmoid(logits)
    sel = aff + rb
    tm = logits.shape[1]
    scores = []
    for g in range(N_GROUPS):
        r = [sel[g * E_PER_GROUP + j:g * E_PER_GROUP + j + 1] for j in range(E_PER_GROUP)]
        best = None
        for i in range(E_PER_GROUP):
            for j in range(i + 1, E_PER_GROUP):
                pair = r[i] + r[j]
                best = pair if best is None else jnp.maximum(best, pair)
        scores.append(best)
    gbest, gi = scores[0], jnp.zeros((1, tm), jnp.int32)
    for g in range(1, N_GROUPS):
        upd = scores[g] > gbest
        gbest = jnp.where(upd, scores[g], gbest)
        gi = jnp.where(upd, g, gi)
    s_in, a_in = [], []
    for j in range(E_PER_GROUP):
        sj, aj = sel[j:j + 1], aff[j:j + 1]
        for g in range(1, N_GROUPS):
            e = g * E_PER_GROUP + j
            sj = jnp.where(gi == g, sel[e:e + 1], sj)
            aj = jnp.where(gi == g, aff[e:e + 1], aj)
        s_in.append(sj)
        a_in.append(aj)
    chosen = []
    for j in range(E_PER_GROUP):
        rank = jnp.zeros((1, tm), jnp.int32)
        for k in range(E_PER_GROUP):
            if k == j:
                continue
            ahead = (s_in[k] >= s_in[j]) if k < j else (s_in[k] > s_in[j])
            rank = rank + ahead.astype(jnp.int32)
        chosen.append(rank < 2)
    w = [jnp.where(chosen[j], a_in[j], 0.0) for j in range(E_PER_GROUP)]
    wsum = w[0] + w[1] + w[2] + w[3]
    gate_in = [wj / wsum * ROUTE_SCALE for wj in w]
    code = sum(jnp.where(chosen[j], 1 << j, 0) for j in range(E_PER_GROUP))
    pair = jnp.zeros((1, tm), jnp.int32)
    for idx, cval in enumerate((3, 5, 9, 6, 10, 12)):
        pair = jnp.where(code == cval, idx, pair)
    bucket = gi * N_PAIRS + pair
    rows = []
    for e in range(N_EXPERTS):
        g, j = divmod(e, E_PER_GROUP)
        rows.append(jnp.where(gi == g, gate_in[j], 0.0))
    return jnp.concatenate(rows, axis=0), bucket


def _out_kernel(ohg_ref, omla_ref, oswa_ref, x_ref, g1_ref, sh2_ref, sc2_ref, lng_ref, lnb_ref,
                w_ref, rwt_ref, rb_ref, x1_ref, h2_ref, gates_ref, bucket_ref, *, alpha):
    o1, o2 = HG_W, HG_W + MLA_HEADS * MLA_V
    mix = (_dot(ohg_ref[...], w_ref[0:o1, :]) + _dot(omla_ref[...], w_ref[o1:o2, :])
           + _dot(oswa_ref[...], w_ref[o2:, :]))
    x1 = _ln_rows(alpha * x_ref[...] + g1_ref[...] * mix) * lng_ref[...] + lnb_ref[...]
    x1_ref[...] = x1
    h2 = (_ln_rows(x1) * (1.0 + sc2_ref[...]) + sh2_ref[...]).astype(BF16)
    h2_ref[...] = h2.astype(h2_ref.dtype)
    gates, bucket = _route(_dot_nt(rwt_ref[...], h2), rb_ref[...])
    gates_ref[...] = gates
    bucket_ref[...] = bucket


def _out_proj(o_hg, o_mla, o_swa, x, mod, ln_g, ln_b, w_out, rwt, rb, rows, alpha):
    tm = rows.tm
    d = x.shape[1]
    n = rows.n_rows
    row = lambda w: pl.BlockSpec((tm, w), lambda i: (i, 0))
    whole = lambda a: pl.BlockSpec(a.shape, lambda i: (0, 0))
    return pl.pallas_call(
        functools.partial(_out_kernel, alpha=alpha), grid=(rows.n_tiles,),
        in_specs=[row(HG_W), row(MLA_HEADS * MLA_V), row(SWA_W), row(d),
                  _mod_spec(rows, 2, d), _mod_spec(rows, 3, d), _mod_spec(rows, 4, d),
                  whole(ln_g), whole(ln_b), whole(w_out), whole(rwt), whole(rb)],
        out_specs=[row(d), row(d), pl.BlockSpec((N_EXPERTS, tm), lambda i: (0, i)),
                   pl.BlockSpec((1, tm), lambda i: (0, i))],
        out_shape=[jax.ShapeDtypeStruct((n, d), F32), jax.ShapeDtypeStruct((n, d), F32),
                   jax.ShapeDtypeStruct((N_EXPERTS, n), F32), jax.ShapeDtypeStruct((1, n), jnp.int32)],
        compiler_params=_cparams(("arbitrary",)), name="out_proj_ln_router",
    )(o_hg, o_mla, o_swa, x, mod, mod, mod, ln_g, ln_b, w_out, rwt, rb)


def _moe_kernel(perm_ref, e1_ref, e2_ref, nused_ref, h_hbm, g_ref, wg1_ref, wu1_ref, wd1_ref,
                wg2_ref, wu2_ref, wd2_ref, y_ref, hbuf, sem, *, tm):
    i = pl.program_id(0)

    @pl.when(i < nused_ref[0])
    def _():
        def row_copy(r):
            return pltpu.make_async_copy(h_hbm.at[pl.ds(perm_ref[i * tm + r], 1)], hbuf.at[pl.ds(r, 1)], sem)

        def issue(r, carry):
            row_copy(r).start()
            return carry

        def drain(r, carry):
            row_copy(r).wait()
            return carry

        lax.fori_loop(0, tm, issue, 0)
        lax.fori_loop(0, tm, drain, 0)
        h = hbuf[...].astype(BF16)
        gates = g_ref[...]

        def expert(wg, wu, wd):
            act = (_silu(_dot(h, wg[...])) * _dot(h, wu[...])).astype(BF16)
            return _dot(act, wd[...])

        y_ref[...] = gates[:, 0:1] * expert(wg1_ref, wu1_ref, wd1_ref) + gates[:, 1:2] * expert(wg2_ref, wu2_ref, wd2_ref)

    @pl.when(i >= nused_ref[0])
    def _():
        y_ref[...] = jnp.zeros_like(y_ref)


def _moe(h2, perm, e1, e2, n_used, gate_sorted, wg, wu, wd, tm):
    n_tiles = perm.shape[0] // tm
    d, f = wg.shape[1], wg.shape[2]
    last = lambda i, nu: jnp.minimum(i, nu[0] - 1)
    wspec_in = lambda sel: pl.BlockSpec((None, d, f), lambda i, p, a, b, nu: ((a, b)[sel][last(i, nu)], 0, 0))
    wspec_dn = lambda sel: pl.BlockSpec((None, f, d), lambda i, p, a, b, nu: ((a, b)[sel][last(i, nu)], 0, 0))
    grid_spec = pltpu.PrefetchScalarGridSpec(
        num_scalar_prefetch=4, grid=(n_tiles,),
        in_specs=[pl.BlockSpec(memory_space=pl.ANY),
                  pl.BlockSpec((tm, 2), lambda i, p, a, b, nu: (last(i, nu), 0)),
                  wspec_in(0), wspec_in(0), wspec_dn(0), wspec_in(1), wspec_in(1), wspec_dn(1)],
        out_specs=pl.BlockSpec((tm, d), lambda i, p, a, b, nu: (i, 0)),
        scratch_shapes=[pltpu.VMEM((tm, d), F32), pltpu.SemaphoreType.DMA(())])
    return pl.pallas_call(
        functools.partial(_moe_kernel, tm=tm), grid_spec=grid_spec,
        out_shape=jax.ShapeDtypeStruct((n_tiles * tm, d), F32),
        compiler_params=_cparams(("arbitrary",)), name="moe_grouped",
    )(perm, e1, e2, n_used, h2, gate_sorted, wg, wu, wd, wg, wu, wd)


def _moe_plan(bucket, gates_t, n_tokens, tm):
    onehot = (bucket[:, None] == jnp.arange(N_BUCKETS, dtype=jnp.int32)[None, :]).astype(jnp.int32)
    csum = jnp.cumsum(onehot, axis=0)
    counts = csum[-1]
    rank = jnp.take_along_axis(csum, bucket[:, None], axis=1)[:, 0] - 1
    padded = (counts + tm - 1) // tm * tm
    ends = jnp.cumsum(padded)
    starts = ends - padded
    pos = starts[bucket] + rank
    n_tiles = n_tokens // tm + N_BUCKETS
    tok = jnp.arange(n_tokens, dtype=jnp.int32)
    perm = jnp.zeros((n_tiles * tm,), jnp.int32).at[pos].set(tok)
    valid = jnp.zeros((n_tiles * tm,), jnp.bool_).at[pos].set(True)
    tile_bucket = jnp.searchsorted(ends, jnp.arange(n_tiles, dtype=jnp.int32) * tm, side="right")
    tile_bucket = jnp.minimum(tile_bucket, N_BUCKETS - 1).astype(jnp.int32)
    grp, pair = tile_bucket // N_PAIRS, tile_bucket % N_PAIRS
    lo = jnp.array([0, 0, 0, 1, 1, 2], jnp.int32)[pair]
    hi = jnp.array([1, 2, 3, 2, 3, 3], jnp.int32)[pair]
    e1, e2 = grp * E_PER_GROUP + lo, grp * E_PER_GROUP + hi
    n_used = (ends[-1] // tm).astype(jnp.int32).reshape(1)
    slot_tile = jnp.arange(n_tiles * tm, dtype=jnp.int32) // tm
    g1 = jnp.where(valid, gates_t[e1[slot_tile], perm], 0.0)
    g2 = jnp.where(valid, gates_t[e2[slot_tile], perm], 0.0)
    return perm, pos.astype(jnp.int32), e1, e2, n_used, jnp.stack([g1, g2], axis=1)


def _ln2_kernel(pos_ref, y_hbm, x1_ref, g2_ref, lng_ref, lnb_ref, *rest, alpha, tm, emit_h):
    if emit_h:
        sh_ref, sc_ref, x2_ref, h_ref, ybuf, sem = rest
    else:
        x2_ref, ybuf, sem = rest
    i = pl.program_id(0)

    def row_copy(r):
        return pltpu.make_async_copy(y_hbm.at[pl.ds(pos_ref[i * tm + r], 1)], ybuf.at[pl.ds(r, 1)], sem)

    def issue(r, carry):
        row_copy(r).start()
        return carry

    def drain(r, carry):
        row_copy(r).wait()
        return carry

    lax.fori_loop(0, tm, issue, 0)
    lax.fori_loop(0, tm, drain, 0)
    x2 = _ln_rows(alpha * x1_ref[...] + g2_ref[...] * ybuf[...]) * lng_ref[...] + lnb_ref[...]
    x2_ref[...] = x2
    if emit_h:
        h_ref[...] = (_ln_rows(x2) * (1.0 + sc_ref[...]) + sh_ref[...]).astype(h_ref.dtype)


def _ln2(pos, y_sorted, x1, mod, mod_next, ln_g, ln_b, rows, alpha):
    tm = rows.tm
    d = x1.shape[1]
    emit_h = mod_next is not None
    row = pl.BlockSpec((tm, d), lambda i, p: (i, 0))
    whole = lambda a: pl.BlockSpec(a.shape, lambda i, p: (0, 0))
    in_specs = [pl.BlockSpec(memory_space=pl.ANY), row, _mod_spec(rows, 5, d), whole(ln_g), whole(ln_b)]
    inputs = [y_sorted, x1, mod, ln_g, ln_b]
    out_specs = [row]
    out_shape = [jax.ShapeDtypeStruct((rows.n_rows, d), F32)]
    if emit_h:
        in_specs += [_mod_spec(rows, 0, d), _mod_spec(rows, 1, d)]
        inputs += [mod_next, mod_next]
        out_specs.append(row)
        out_shape.append(jax.ShapeDtypeStruct((rows.n_rows, d), BF16))
    grid_spec = pltpu.PrefetchScalarGridSpec(
        num_scalar_prefetch=1, grid=(rows.n_tiles,), in_specs=in_specs, out_specs=out_specs,
        scratch_shapes=[pltpu.VMEM((tm, d), F32), pltpu.SemaphoreType.DMA(())])
    out = pl.pallas_call(
        functools.partial(_ln2_kernel, alpha=alpha, tm=tm, emit_h=emit_h), grid_spec=grid_spec,
        out_shape=out_shape, compiler_params=_cparams(("arbitrary",)), name="unpermute_ln2",
    )(pos, *inputs)
    return out if emit_h else (out[0], None)


def _rope_tables(seq, batch, ctx_rows, dim, pad_to):
    rows = seq // GRID_W
    row = jnp.repeat(jnp.arange(rows, dtype=jnp.int32), GRID_W)
    col = jnp.tile(jnp.arange(GRID_W, dtype=jnp.int32), rows)
    nf = dim // 4
    inv_freq = ROPE_BASE ** (-jnp.arange(nf, dtype=F32) / nf)
    ang = jnp.stack([row, col], -1).astype(F32)[:, :, None] * inv_freq
    cos, sin = jnp.cos(ang), jnp.sin(ang)
    c = jnp.stack([cos, cos], axis=2).reshape(seq, dim)
    s = jnp.stack([-sin, sin], axis=2).reshape(seq, dim)
    if pad_to > dim:
        c = jnp.pad(c, ((0, 0), (0, pad_to - dim)))
        s = jnp.pad(s, ((0, 0), (0, pad_to - dim)))
    ctx_c = jnp.zeros((ctx_rows, pad_to), F32).at[:, :dim].set(1.0)
    c = jnp.concatenate([jnp.tile(c, (batch, 1)), ctx_c], axis=0)
    s = jnp.concatenate([jnp.tile(s, (batch, 1)), jnp.zeros((ctx_rows, pad_to), F32)], axis=0)
    return c, s


def _split_w_in(w):
    o1 = 5 * HG_W
    o2 = o1 + MLA_Q_LORA + MLA_KV_LORA + MLA_ROPE
    hg, mla, swa = w[:, :o1], w[:, o1:o2], w[:, o2:]
    w_raw = jnp.concatenate([hg[:, 0:HG_W], hg[:, 3 * HG_W:5 * HG_W]], axis=1)
    w_gate = hg[:, HG_W:3 * HG_W]
    kr = jnp.pad(mla[:, MLA_Q_LORA + MLA_KV_LORA:], ((0, 0), (0, LANES - MLA_ROPE)))
    w_att = jnp.concatenate([swa, mla[:, :MLA_Q_LORA + MLA_KV_LORA], kr], axis=1)
    return w_raw.astype(BF16), w_gate.astype(BF16), w_att.astype(BF16)


def _mla_weights(w_uq, w_ukv):
    qk = MLA_NOPE + MLA_ROPE
    wq = w_uq.reshape(MLA_Q_LORA, MLA_HEADS, qk)
    wq = jnp.pad(wq, ((0, 0), (0, 0), (0, MLA_QK_PAD - qk))).reshape(MLA_Q_LORA, MLA_HEADS * MLA_QK_PAD)
    wkv = w_ukv.reshape(MLA_KV_LORA, MLA_HEADS, MLA_NOPE + MLA_V)
    wkv = jnp.concatenate([wkv[:, :, :MLA_NOPE].reshape(MLA_KV_LORA, -1), wkv[:, :, MLA_NOPE:].reshape(MLA_KV_LORA, -1)], axis=1)
    return wq.astype(BF16), wkv.astype(BF16)


def kernel(x, c, ctx, c_ctx, w_ada, b_ada, w_in, w_out, hg_lb_logits, hg_norm_g, mla_q_norm, mla_kv_norm,
           mla_w_uq, mla_w_ukv, swa_sink, ln1_g, ln1_b, ln2_g, ln2_b, router_w, router_b,
           moe_w_gate, moe_w_up, moe_w_down):
    batch, seq, d = x.shape
    ctx_len = ctx.shape[1]
    depth = w_ada.shape[0]
    alpha = (2.0 * depth) ** 0.25
    n_lat, n_ctx = batch * seq, batch * ctx_len
    n_all = n_lat + n_ctx
    assert batch + 1 <= 8 and ctx_len % (4 * HG_CHUNK) == 0 and seq % ctx_len == 0 and seq % GRID_W == 0

    c8 = jnp.zeros((8, d), F32).at[:batch].set(c).at[batch].set(c_ctx)
    mod_all = _ada(c8, w_ada, b_ada).reshape(depth, 8 * 6, 1, d)

    lb = jnp.cumsum(jax.nn.softmax(hg_lb_logits.astype(F32), axis=0), axis=0)
    lb = (lb - lb[0:1]).reshape(depth, 1, 2 * HG_W)
    log_lb, log_1m, one_m = jnp.log(lb), jnp.log1p(-lb), 1.0 - lb

    cs, ss = _rope_tables(seq, batch, n_ctx, SWA_HD, SWA_HD)
    cm, sm = _rope_tables(seq, batch, n_ctx, MLA_ROPE, LANES)
    rwt = router_w.T.astype(BF16)
    rb = router_b.astype(F32).reshape(N_EXPERTS, 1)

    tm_all = _pick_tm(seq, n_ctx, 512)
    rows_all = _Rows(batch, seq, ctx_len, n_all, tm_all)
    rows_lat = _Rows(batch, seq, ctx_len, n_lat, tm_all)
    moe_tm = 256

    xa = jnp.concatenate([x.reshape(n_lat, d), ctx.reshape(n_ctx, d)], axis=0)
    h = _lnmod(xa, mod_all[0], rows_all, 0)

    for layer in range(depth):
        need_ctx = layer < depth - 1
        mod = mod_all[layer]
        rows = rows_all if need_ctx else rows_lat
        w_raw, w_gate, w_att = _split_w_in(w_in[layer])
        wq, wkv = _mla_weights(mla_w_uq[layer], mla_w_ukv[layer])

        (p_raw,) = _row_call(_mm_raw_kernel, rows_all, [h], [w_raw], [3 * HG_W], [F32], "in_proj_hg_raw")
        lf, kk = _row_call(_mm_gate_kernel, rows_all, [h], [w_gate, log_lb[layer], log_1m[layer], one_m[layer]],
                           [2 * HG_W, 2 * HG_W], [F32, F32], "in_proj_hg_gates")
        tables = [cs, ss, cm, sm]
        p_att = pl.pallas_call(
            _mm_att_kernel, grid=(rows_all.n_tiles,),
            in_specs=[pl.BlockSpec((tm_all, d), lambda i: (i, 0)), pl.BlockSpec(w_att.shape, lambda i: (0, 0))]
                     + [pl.BlockSpec((tm_all, LANES), lambda i: (i, 0))] * 4,
            out_specs=pl.BlockSpec((tm_all, ATT_COLS), lambda i: (i, 0)),
            out_shape=jax.ShapeDtypeStruct((n_all, ATT_COLS), BF16),
            compiler_params=_cparams(("arbitrary",)), name="in_proj_att",
        )(h, w_att, *tables)

        q_hg, v_hg, g_hg = p_raw[:, :HG_W], p_raw[:, HG_W:2 * HG_W], p_raw[:, 2 * HG_W:]
        o_f = _hg_scan(q_hg, kk[:, :HG_W], lf[:, :HG_W], v_hg, batch, seq, ctx_len, False)
        o_hg = _hg_scan(q_hg, kk[:, HG_W:], lf[:, HG_W:], v_hg, batch, seq, ctx_len, True,
                        (o_f, g_hg, hg_norm_g[layer].astype(F32).reshape(1, HG_DK)))

        q_mla, k_mla, v_mla = _mla_proj(p_att, mla_q_norm[layer].astype(F32).reshape(1, -1),
                                        mla_kv_norm[layer].astype(F32).reshape(1, -1), wq, wkv, cm, sm, rows_all)
        n_out = rows.n_rows
        o_mla = _mla_attn(q_mla, k_mla, v_mla, batch, seq, ctx_len, need_ctx)
        o_swa = _swa(p_att, swa_sink[layer].astype(F32), batch, seq, ctx_len, need_ctx)

        x1, h2, gates_t, bucket = _out_proj(
            o_hg, o_mla, o_swa, xa, mod, ln1_g[layer].astype(F32).reshape(1, d), ln1_b[layer].astype(F32).reshape(1, d),
            w_out[layer].astype(BF16), rwt, rb, rows, alpha)

        perm, pos, e1, e2, n_used, gate_sorted = _moe_plan(bucket[0], gates_t, n_out, moe_tm)
        y_sorted = _moe(h2, perm, e1, e2, n_used, gate_sorted, moe_w_gate[layer].astype(BF16),
                        moe_w_up[layer].astype(BF16), moe_w_down[layer].astype(BF16), moe_tm)
        xa, h = _ln2(pos, y_sorted, x1, mod, mod_all[layer + 1] if need_ctx else None,
                     ln2_g[layer].astype(F32).reshape(1, d), ln2_b[layer].astype(F32).reshape(1, d), rows, alpha)

    return xa[:n_lat].reshape(batch, seq, d)
```

```python
import functools
import math

import jax
import jax.numpy as jnp
from jax import lax
from jax.experimental import pallas as pl
from jax.experimental.pallas import tpu as pltpu

F32 = jnp.float32
BF16 = jnp.bfloat16

GRID_W = 64
HG_HEADS = 4
HG_DK = 128
HG_W = HG_HEADS * HG_DK
HG_CHUNK = 64
HG_SUB = 16
MLA_HEADS = 8
MLA_Q_LORA = 512
MLA_KV_LORA = 256
MLA_NOPE = 128
MLA_ROPE = 64
MLA_V = 128
MLA_QK_PAD = 256
MLA_V_PAD = 256
MLA_HEADS_PER_STEP = 2
SWA_HEADS = 4
SWA_KV_HEADS = 2
SWA_HD = 128
SWA_W = SWA_HEADS * SWA_HD
SWA_KV_W = SWA_KV_HEADS * SWA_HD
SWA_WINDOW = 128
SWA_BLOCK = 128
N_EXPERTS = 16
N_GROUPS = 4
E_PER_GROUP = 4
N_PAIRS = 6
N_BUCKETS = N_GROUPS * N_PAIRS
ROUTE_SCALE = 2.5
ROPE_BASE = 10000.0
LN_EPS = 1e-5
RMS_EPS = 1e-6
LANES = 128
VMEM_LIMIT = 56 * 1024 * 1024

ATT_SQ, ATT_SK, ATT_SV = 0, SWA_W, SWA_W + SWA_KV_W
ATT_CQ = SWA_W + 2 * SWA_KV_W
ATT_CKV = ATT_CQ + MLA_Q_LORA
ATT_KR = ATT_CKV + MLA_KV_LORA
ATT_COLS = ATT_KR + LANES


def _cparams(sem):
    return pltpu.CompilerParams(dimension_semantics=sem, vmem_limit_bytes=VMEM_LIMIT)


def _dot(a, b):
    return jnp.dot(a, b, preferred_element_type=F32)


def _dot_nt(a, b):
    return lax.dot_general(a, b, (((1,), (1,)), ((), ())), preferred_element_type=F32)


def _dot_tn(a, b):
    return lax.dot_general(a, b, (((0,), (0,)), ((), ())), preferred_element_type=F32)


def _sigmoid(x):
    return 1.0 / (1.0 + jnp.exp(-x))


def _silu(x):
    return x * _sigmoid(x)


def _ln_rows(x):
    mu = jnp.mean(x, axis=-1, keepdims=True)
    xc = x - mu
    var = jnp.mean(xc * xc, axis=-1, keepdims=True)
    return xc * lax.rsqrt(var + LN_EPS)


def _rope_partner(x, half):
    lane = lax.broadcasted_iota(jnp.int32, x.shape, x.ndim - 1)
    first = (lane & half) == 0
    n = x.shape[-1]
    return jnp.where(first, pltpu.roll(x, n - half, x.ndim - 1), pltpu.roll(x, half, x.ndim - 1))


def _ada_kernel(c_ref, w_ref, b_ref, o_ref):
    s = _silu(c_ref[...])
    o_ref[...] = _dot(s.astype(BF16), w_ref[...].astype(BF16)) + b_ref[...]


def _ada(c8, w_ada, b_ada):
    depth, d, n = w_ada.shape
    tn = 1024 if n % 1024 == 0 else n
    return pl.pallas_call(
        _ada_kernel,
        grid=(depth, n // tn),
        in_specs=[pl.BlockSpec((8, d), lambda l, j: (0, 0)),
                  pl.BlockSpec((None, d, tn), lambda l, j: (l, 0, j)),
                  pl.BlockSpec((None, 1, tn), lambda l, j: (l, 0, j))],
        out_specs=pl.BlockSpec((None, 8, tn), lambda l, j: (l, 0, j)),
        out_shape=jax.ShapeDtypeStruct((depth, 8, n), F32),
        compiler_params=_cparams(("arbitrary", "arbitrary")),
        name="ada_mod",
    )(c8, w_ada, b_ada.reshape(depth, 1, n))


class _Rows:
    def __init__(self, batch, seq, ctx_len, n_rows, tm):
        self.batch, self.seq, self.ctx_len, self.n_rows, self.tm = batch, seq, ctx_len, n_rows, tm
        self.n_tiles = n_rows // tm
        self.lat_tiles = batch * seq // tm
        self.tiles_per_batch = seq // tm

    def mod_row(self, i):
        return jnp.where(i < self.lat_tiles, i // self.tiles_per_batch, self.batch)


def _pick_tm(seq, ctx_rows, cap):
    for tm in (1024, 512, 256, 128):
        if tm <= cap and seq % tm == 0 and ctx_rows % tm == 0:
            return tm
    raise ValueError("unsupported sequence / context lengths")


def _mod_spec(rows, chunk, d):
    return pl.BlockSpec((None, 1, d), lambda i, *_: (rows.mod_row(i) * 6 + chunk, 0, 0))


def _lnmod_kernel(x_ref, sh_ref, sc_ref, h_ref):
    y = _ln_rows(x_ref[...])
    h_ref[...] = (y * (1.0 + sc_ref[...]) + sh_ref[...]).astype(h_ref.dtype)


def _lnmod(x, mod, rows, chunk0):
    d = x.shape[1]
    tm = rows.tm
    return pl.pallas_call(
        _lnmod_kernel,
        grid=(rows.n_tiles,),
        in_specs=[pl.BlockSpec((tm, d), lambda i: (i, 0)),
                  _mod_spec(rows, chunk0, d), _mod_spec(rows, chunk0 + 1, d)],
        out_specs=pl.BlockSpec((tm, d), lambda i: (i, 0)),
        out_shape=jax.ShapeDtypeStruct((rows.n_rows, d), BF16),
        compiler_params=_cparams(("arbitrary",)),
        name="ln_modulate",
    )(x, mod, mod)


def _mm_raw_kernel(h_ref, w_ref, o_ref):
    o_ref[...] = _dot(h_ref[...], w_ref[...]).astype(o_ref.dtype)


def _mm_gate_kernel(h_ref, w_ref, loglb_ref, log1m_ref, onem_ref, lf_ref, k_ref):
    z = _dot(h_ref[...], w_ref[...])
    e = jnp.exp(-jnp.abs(z))
    r = 1.0 / (1.0 + e)
    log_sig = jnp.minimum(z, 0.0) + jnp.log(r)
    sig_neg = jnp.where(z >= 0.0, e * r, r)
    a = loglb_ref[...]
    b = log1m_ref[...] + log_sig
    lf_ref[...] = jnp.maximum(a, b) + jnp.log(1.0 + jnp.exp(-jnp.abs(a - b)))
    k_ref[...] = onem_ref[...] * sig_neg


def _mm_att_kernel(h_ref, w_ref, cs_ref, ss_ref, cm_ref, sm_ref, o_ref):
    p = _dot(h_ref[...], w_ref[...])
    cs, ss = cs_ref[...], ss_ref[...]
    swa_scale = SWA_HD ** -0.5
    for hd in range(SWA_HEADS + SWA_KV_HEADS):
        c0 = ATT_SQ + hd * SWA_HD
        x = p[:, c0:c0 + SWA_HD]
        y = x * cs + _rope_partner(x, SWA_HD // 4) * ss
        if hd < SWA_HEADS:
            y = y * swa_scale
        o_ref[:, c0:c0 + SWA_HD] = y.astype(o_ref.dtype)
    o_ref[:, ATT_SV:ATT_KR] = p[:, ATT_SV:ATT_KR].astype(o_ref.dtype)
    x = p[:, ATT_KR:ATT_KR + LANES]
    y = x * cm_ref[...] + _rope_partner(x, MLA_ROPE // 4) * sm_ref[...]
    o_ref[:, ATT_KR:ATT_KR + LANES] = y.astype(o_ref.dtype)


def _row_call(kernel, rows, row_inputs, const_inputs, out_widths, out_dtypes, name):
    tm = rows.tm
    in_specs = [pl.BlockSpec((tm, a.shape[1]), lambda i: (i, 0)) for a in row_inputs]
    in_specs += [pl.BlockSpec(a.shape, lambda i, nd=a.ndim: (0,) * nd) for a in const_inputs]
    out_specs = [pl.BlockSpec((tm, w), lambda i: (i, 0)) for w in out_widths]
    out_shape = [jax.ShapeDtypeStruct((rows.n_rows, w), dt) for w, dt in zip(out_widths, out_dtypes)]
    return pl.pallas_call(
        kernel, grid=(rows.n_tiles,), in_specs=in_specs, out_specs=out_specs, out_shape=out_shape,
        compiler_params=_cparams(("arbitrary",)), name=name,
    )(*row_inputs, *const_inputs)


def _hg_chunk(q, k, lf, v, st, reverse):
    c, s = HG_CHUNK, HG_SUB
    ri = lax.broadcasted_iota(jnp.int32, (c, c), 0)
    ci = lax.broadcasted_iota(jnp.int32, (c, c), 1)
    tri = jnp.where((ci >= ri) if reverse else (ci <= ri), 1.0, 0.0).astype(BF16)
    hi = lf.astype(BF16)
    r1 = lf - hi.astype(F32)
    mid = r1.astype(BF16)
    lo = (r1 - mid.astype(F32)).astype(BF16)
    b = _dot(tri, hi) + _dot(tri, mid) + _dot(tri, lo)
    btot = b[0:1, :] if reverse else b[c - 1:c, :]
    o_inter = _dot_nt((q * jnp.exp(b)).astype(BF16), st.astype(BF16))
    k_dec = (k * jnp.exp(btot - b)).astype(BF16)
    v16 = v.astype(BF16)
    st_new = st * jnp.exp(btot) + _dot_tn(v16, k_dec)
    outs = []
    nsub = c // s
    for i in range(nsub):
        r0 = i * s
        if reverse:
            k0, k1 = r0, c
            ref = b[r0 + s:r0 + s + 1, :] if i < nsub - 1 else jnp.zeros_like(btot)
        else:
            k0, k1 = 0, r0 + s
            ref = b[r0 - 1:r0, :] if i > 0 else jnp.zeros_like(btot)
        qt = (q[r0:r0 + s] * jnp.exp(b[r0:r0 + s] - ref)).astype(BF16)
        kt = (k[k0:k1] * jnp.exp(ref - b[k0:k1])).astype(BF16)
        att = _dot_nt(qt, kt)
        rr = lax.broadcasted_iota(jnp.int32, att.shape, 0) + r0
        cc = lax.broadcasted_iota(jnp.int32, att.shape, 1) + k0
        keep = (cc >= rr) if reverse else (cc <= rr)
        att = jnp.where(keep, att, 0.0).astype(BF16)
        outs.append(o_inter[r0:r0 + s] + _dot(att, v16[k0:k1]))
    return jnp.concatenate(outs, axis=0), st_new


def _hg_scan_kernel(*refs, reverse, n_chunks, readout):
    if readout:
        q_ref, k_ref, lf_ref, v_ref, of_ref, g_ref, ng_ref, o_ref, st_ref = refs
    else:
        q_ref, k_ref, lf_ref, v_ref, o_ref, st_ref = refs

    @pl.when(pl.program_id(1) == 0)
    def _():
        st_ref[...] = jnp.zeros_like(st_ref)

    for cc in range(n_chunks):
        c = n_chunks - 1 - cc if reverse else cc
        rs = slice(c * HG_CHUNK, (c + 1) * HG_CHUNK)
        for hd in range(HG_HEADS):
            cs = slice(hd * HG_DK, (hd + 1) * HG_DK)
            o, st = _hg_chunk(q_ref[rs, cs], k_ref[rs, cs], lf_ref[rs, cs], v_ref[rs, cs], st_ref[hd], reverse)
            st_ref[hd] = st
            if readout:
                o = o + of_ref[rs, cs]
                o = o * lax.rsqrt(jnp.mean(o * o, axis=-1, keepdims=True) + RMS_EPS) * ng_ref[...]
                o = o * _silu(g_ref[rs, cs])
            o_ref[rs, cs] = o.astype(o_ref.dtype)


def _hg_scan(p_raw, kk, lf, batch, seq, ctx_len, reverse, readout_args=None):
    n_rows = p_raw.shape[0]
    tb = ctx_len
    nl = seq // tb
    ctx0 = batch * seq // tb
    direction = 1 if reverse else 0

    def row_block(b, i):
        lat = b * nl + (nl - i if reverse else i - 1)
        return jnp.where(i == 0, ctx0 + b, lat)

    def col(j):
        return pl.BlockSpec((tb, HG_W), lambda b, i: (row_block(b, i), j))

    spec = col(0)
    readout = readout_args is not None
    inputs = [p_raw, kk, lf, p_raw]
    in_specs = [col(0), col(direction), col(direction), col(1)]
    if readout:
        o_f, norm_g = readout_args
        inputs += [o_f, p_raw, norm_g]
        in_specs += [spec, col(2), pl.BlockSpec((1, HG_DK), lambda b, i: (0, 0))]
    kern = functools.partial(_hg_scan_kernel, reverse=reverse, n_chunks=tb // HG_CHUNK, readout=readout)
    return pl.pallas_call(
        kern, grid=(batch, nl + 1), in_specs=in_specs, out_specs=spec,
        out_shape=jax.ShapeDtypeStruct((n_rows, HG_W), BF16 if readout else F32),
        scratch_shapes=[pltpu.VMEM((HG_HEADS, HG_DK, HG_DK), F32)],
        compiler_params=_cparams(("arbitrary", "arbitrary")),
        name="hgrn2_scan_bwd_readout" if readout else "hgrn2_scan_fwd",
    )(*inputs)


def _mla_proj_kernel(cq_ref, ckv_ref, kr_ref, qn_ref, kvn_ref, wq_ref, wkv_ref, cm_ref, sm_ref,
                     q_ref, k_ref, v_ref):
    def rms(x, g):
        xf = x.astype(F32)
        return (xf * lax.rsqrt(jnp.mean(xf * xf, axis=-1, keepdims=True) + RMS_EPS) * g).astype(BF16)

    scale = (MLA_NOPE + MLA_ROPE) ** -0.5 * math.log2(math.e)
    qf = _dot(rms(cq_ref[...], qn_ref[...]), wq_ref[...])
    kv = _dot(rms(ckv_ref[...], kvn_ref[...]), wkv_ref[...])
    cm, sm = cm_ref[...], sm_ref[...]
    kr = kr_ref[...]
    for hd in range(MLA_HEADS):
        c0 = hd * MLA_QK_PAD
        q_ref[:, c0:c0 + LANES] = (qf[:, c0:c0 + LANES] * scale).astype(q_ref.dtype)
        x = qf[:, c0 + LANES:c0 + 2 * LANES]
        y = (x * cm + _rope_partner(x, MLA_ROPE // 4) * sm) * scale
        q_ref[:, c0 + LANES:c0 + 2 * LANES] = y.astype(q_ref.dtype)
        k_ref[:, c0:c0 + LANES] = kv[:, hd * LANES:(hd + 1) * LANES].astype(k_ref.dtype)
        k_ref[:, c0 + LANES:c0 + 2 * LANES] = kr
    lane = lax.broadcasted_iota(jnp.int32, (kr.shape[0], LANES), 1)
    ones_col = jnp.where(lane == 0, 1.0, 0.0).astype(v_ref.dtype)
    for hd in range(MLA_HEADS):
        c0 = hd * MLA_V_PAD
        v_ref[:, c0:c0 + MLA_V] = kv[:, (MLA_HEADS + hd) * LANES:(MLA_HEADS + hd + 1) * LANES].astype(v_ref.dtype)
        v_ref[:, c0 + MLA_V:c0 + MLA_V_PAD] = ones_col


def _mla_proj(p_att, q_norm, kv_norm, wq, wkv, cm, sm, rows):
    tm = rows.tm
    n = rows.n_rows
    hq = MLA_HEADS * MLA_QK_PAD

    def col(width, off):
        return pl.BlockSpec((tm, width), lambda i: (i, off // width))

    def whole(a):
        return pl.BlockSpec(a.shape, lambda i: (0, 0))

    return pl.pallas_call(
        _mla_proj_kernel, grid=(rows.n_tiles,),
        in_specs=[col(MLA_Q_LORA, ATT_CQ), col(MLA_KV_LORA, ATT_CKV), col(LANES, ATT_KR),
                  whole(q_norm), whole(kv_norm), whole(wq), whole(wkv),
                  pl.BlockSpec((tm, LANES), lambda i: (i, 0)), pl.BlockSpec((tm, LANES), lambda i: (i, 0))],
        out_specs=[pl.BlockSpec((tm, hq), lambda i: (i, 0)), pl.BlockSpec((tm, hq), lambda i: (i, 0)),
                   pl.BlockSpec((tm, MLA_HEADS * MLA_V_PAD), lambda i: (i, 0))],
        out_shape=[jax.ShapeDtypeStruct((n, hq), BF16), jax.ShapeDtypeStruct((n, hq), BF16),
                   jax.ShapeDtypeStruct((n, MLA_HEADS * MLA_V_PAD), BF16)],
        compiler_params=_cparams(("arbitrary",)), name="mla_proj",
    )(p_att, p_att, p_att, q_norm, kv_norm, wq, wkv, cm, sm)


def _lane_tile_fold(x, op):
    out = x[:, 0:LANES]
    for t in range(1, x.shape[1] // LANES):
        out = op(out, x[:, t * LANES:(t + 1) * LANES])
    return out


def _mla_attn_kernel(q_ref, kl_ref, vl_ref, kc_ref, vc_ref, o_ref, *, nq, ck, ctx_queries):
    seq = kl_ref.shape[0]

    def attend(with_lat):
        chunks = [(kc_ref, vc_ref, 0, kc_ref.shape[0])]
        if with_lat:
            chunks += [(kl_ref, vl_ref, c * ck, ck) for c in range(seq // ck)]
        m = [None] * MLA_HEADS_PER_STEP
        acc = [None] * MLA_HEADS_PER_STEP
        for k_ref, v_ref, r0, n in chunks:
            for hd in range(MLA_HEADS_PER_STEP):
                qk = slice(hd * MLA_QK_PAD, (hd + 1) * MLA_QK_PAD)
                s = _dot_nt(q_ref[:, qk], k_ref[r0:r0 + n, qk])
                m_c = jnp.max(_lane_tile_fold(s, jnp.maximum), axis=-1, keepdims=True)
                m_new = m_c if m[hd] is None else jnp.maximum(m[hd], m_c)
                pv = _dot(jnp.exp2(s - m_new).astype(BF16), v_ref[r0:r0 + n, hd * MLA_V_PAD:(hd + 1) * MLA_V_PAD])
                acc[hd] = pv if m[hd] is None else acc[hd] * jnp.exp2(m[hd] - m_new) + pv
                m[hd] = m_new
        for hd in range(MLA_HEADS_PER_STEP):
            o = acc[hd][:, 0:MLA_V] / acc[hd][:, MLA_V:MLA_V + 1]
            o_ref[:, hd * MLA_V:(hd + 1) * MLA_V] = o.astype(o_ref.dtype)

    if ctx_queries:
        i = pl.program_id(2)
        pl.when(i < nq)(lambda: attend(True))
        pl.when(i >= nq)(lambda: attend(False))
    else:
        attend(True)


def _mla_attn(q, k, v, batch, seq, ctx_len, ctx_queries):
    tq = ctx_len
    nq = seq // tq
    ctx0 = batch * seq // ctx_len
    q_row = lambda b, h, i: jnp.where(i < nq, b * nq + i, ctx0 + b)
    n_rows = batch * seq + (batch * ctx_len if ctx_queries else 0)
    ck = 1024 if seq % 1024 == 0 else ctx_len
    hps = MLA_HEADS_PER_STEP
    return pl.pallas_call(
        functools.partial(_mla_attn_kernel, nq=nq, ck=ck, ctx_queries=ctx_queries),
        grid=(batch, MLA_HEADS // hps, nq + (1 if ctx_queries else 0)),
        in_specs=[pl.BlockSpec((tq, hps * MLA_QK_PAD), lambda b, h, i: (q_row(b, h, i), h)),
                  pl.BlockSpec((seq, hps * MLA_QK_PAD), lambda b, h, i: (b, h)),
                  pl.BlockSpec((seq, hps * MLA_V_PAD), lambda b, h, i: (b, h)),
                  pl.BlockSpec((ctx_len, hps * MLA_QK_PAD), lambda b, h, i: (ctx0 + b, h)),
                  pl.BlockSpec((ctx_len, hps * MLA_V_PAD), lambda b, h, i: (ctx0 + b, h))],
        out_specs=pl.BlockSpec((tq, hps * MLA_V), lambda b, h, i: (q_row(b, h, i), h)),
        out_shape=jax.ShapeDtypeStruct((n_rows, MLA_HEADS * MLA_V), BF16),
        compiler_params=_cparams(("arbitrary", "arbitrary", "arbitrary")),
        name="mla_attn",
    )(q, k, v, k, v)


def _swa_kernel(sink_ref, q_ref, kp_ref, kn_ref, kx_ref, kc_ref, vp_ref, vn_ref, vx_ref, vc_ref, o_ref,
                *, nb, seq, ctx_queries):
    n = pl.program_id(1)
    g = SWA_HEADS // SWA_KV_HEADS
    blk = SWA_BLOCK

    def attend(band):
        if band:
            a = lax.broadcasted_iota(jnp.int32, (g * blk, 3 * blk), 0) & (blk - 1)
            j = lax.broadcasted_iota(jnp.int32, (g * blk, 3 * blk), 1)
            key_pos = (n - 1) * blk + j
            keep = (jnp.abs(j - blk - a) <= SWA_WINDOW) & (key_pos >= 0) & (key_pos < seq)
        for kh in range(SWA_KV_HEADS):
            ks = slice(kh * SWA_HD, (kh + 1) * SWA_HD)
            qs = jnp.concatenate(
                [q_ref[:, (kh * g + gi) * SWA_HD:(kh * g + gi + 1) * SWA_HD] for gi in range(g)], axis=0)
            row = lax.broadcasted_iota(jnp.int32, (g * blk, 1), 0)
            sink = jnp.zeros((g * blk, 1), F32)
            for gi in range(g):
                sink = jnp.where((row >= gi * blk) & (row < (gi + 1) * blk), sink_ref[kh * g + gi], sink)
            s_c = _dot_nt(qs, kc_ref[:, ks])
            m = jnp.maximum(jnp.max(s_c, axis=-1, keepdims=True), sink)
            if band:
                kb = jnp.concatenate([kp_ref[:, ks], kn_ref[:, ks], kx_ref[:, ks]], axis=0)
                vb = jnp.concatenate([vp_ref[:, ks], vn_ref[:, ks], vx_ref[:, ks]], axis=0)
                s_b = jnp.where(keep, _dot_nt(qs, kb), -jnp.inf)
                m = jnp.maximum(m, jnp.max(s_b, axis=-1, keepdims=True))
            p_c = jnp.exp(s_c - m)
            den = jnp.sum(p_c, axis=-1, keepdims=True) + jnp.exp(sink - m)
            acc = _dot(p_c.astype(BF16), vc_ref[:, ks])
            if band:
                p_b = jnp.exp(s_b - m)
                den = den + jnp.sum(p_b, axis=-1, keepdims=True)
                acc = acc + _dot(p_b.astype(BF16), vb)
            o = acc / den
            for gi in range(g):
                c0 = (kh * g + gi) * SWA_HD
                o_ref[:, c0:c0 + SWA_HD] = o[gi * blk:(gi + 1) * blk].astype(o_ref.dtype)

    if ctx_queries:
        pl.when(n < nb)(lambda: attend(True))
        pl.when(n >= nb)(lambda: attend(False))
    else:
        attend(True)


def _swa(p_att, sink, batch, seq, ctx_len, ctx_queries):
    blk = SWA_BLOCK
    nb, ncb = seq // blk, ctx_len // blk
    ctx0 = batch * seq // ctx_len
    kcol, vcol = ATT_SK // SWA_KV_W, ATT_SV // SWA_KV_W
    q_row = lambda b, n: jnp.where(n < nb, b * nb + n, batch * nb + b * ncb + (n - nb))
    n_rows = batch * seq + (batch * ctx_len if ctx_queries else 0)

    def nbr(col, d):
        return pl.BlockSpec((blk, SWA_KV_W), lambda b, n: (b * nb + jnp.clip(n + d, 0, nb - 1), col))

    ctx_k = pl.BlockSpec((ctx_len, SWA_KV_W), lambda b, n: (ctx0 + b, kcol))
    ctx_v = pl.BlockSpec((ctx_len, SWA_KV_W), lambda b, n: (ctx0 + b, vcol))
    return pl.pallas_call(
        functools.partial(_swa_kernel, nb=nb, seq=seq, ctx_queries=ctx_queries),
        grid=(batch, nb + (ncb if ctx_queries else 0)),
        in_specs=[pl.BlockSpec(memory_space=pltpu.SMEM),
                  pl.BlockSpec((blk, SWA_W), lambda b, n: (q_row(b, n), 0)),
                  nbr(kcol, -1), nbr(kcol, 0), nbr(kcol, 1), ctx_k, nbr(vcol, -1), nbr(vcol, 0), nbr(vcol, 1), ctx_v],
        out_specs=pl.BlockSpec((blk, SWA_W), lambda b, n: (q_row(b, n), 0)),
        out_shape=jax.ShapeDtypeStruct((n_rows, SWA_W), BF16),
        compiler_params=_cparams(("arbitrary", "arbitrary")),
        name="swa_attn",
    )(sink, *([p_att] * 9))


def _route(logits, rb):
    aff = _sigmoid(logits)
    sel = aff + rb
    tm = logits.shape[1]
    scores = []
    for g in range(N_GROUPS):
        r = [sel[g * E_PER_GROUP + j:g * E_PER_GROUP + j + 1] for j in range(E_PER_GROUP)]
        best = None
        for i in range(E_PER_GROUP):
            for j in range(i + 1, E_PER_GROUP):
                pair = r[i] + r[j]
                best = pair if best is None else jnp.maximum(best, pair)
        scores.append(best)
    gbest, gi = scores[0], jnp.zeros((1, tm), jnp.int32)
    for g in range(1, N_GROUPS):
        upd = scores[g] > gbest
        gbest = jnp.where(upd, scores[g], gbest)
        gi = jnp.where(upd, g, gi)
    s_in, a_in = [], []
    for j in range(E_PER_GROUP):
        sj, aj = sel[j:j + 1], aff[j:j + 1]
        for g in range(1, N_GROUPS):
            e = g * E_PER_GROUP + j
            sj = jnp.where(gi == g, sel[e:e + 1], sj)
            aj = jnp.where(gi == g, aff[e:e + 1], aj)
        s_in.append(sj)
        a_in.append(aj)
    chosen = []
    for j in range(E_PER_GROUP):
        rank = jnp.zeros((1, tm), jnp.int32)
        for k in range(E_PER_GROUP):
            if k == j:
                continue
            ahead = (s_in[k] >= s_in[j]) if k < j else (s_in[k] > s_in[j])
            rank = rank + ahead.astype(jnp.int32)
        chosen.append(rank < 2)
    w = [jnp.where(chosen[j], a_in[j], 0.0) for j in range(E_PER_GROUP)]
    wsum = w[0] + w[1] + w[2] + w[3]
    gate_in = [wj / wsum * ROUTE_SCALE for wj in w]
    code = sum(jnp.where(chosen[j], 1 << j, 0) for j in range(E_PER_GROUP))
    pair = jnp.zeros((1, tm), jnp.int32)
    for idx, cval in enumerate((3, 5, 9, 6, 10, 12)):
        pair = jnp.where(code == cval, idx, pair)
    bucket = gi * N_PAIRS + pair
    g_lo = jnp.zeros((1, tm), F32)
    g_hi = jnp.zeros((1, tm), F32)
    seen = jnp.zeros((1, tm), jnp.bool_)
    for j in range(E_PER_GROUP):
        g_lo = jnp.where(chosen[j] & ~seen, gate_in[j], g_lo)
        g_hi = jnp.where(chosen[j] & seen, gate_in[j], g_hi)
        seen = seen | chosen[j]
    return g_lo, g_hi, bucket


def _out_kernel(ohg_ref, omla_ref, oswa_ref, x_ref, g1_ref, sh2_ref, sc2_ref, lng_ref, lnb_ref,
                w_ref, rwt_ref, rb_ref, x1_ref, h2_ref, bucket_ref, *, alpha):
    o1, o2 = HG_W, HG_W + MLA_HEADS * MLA_V
    d = x_ref.shape[1]
    tm = x_ref.shape[0]
    mix = (_dot(ohg_ref[...], w_ref[0:o1, :]) + _dot(omla_ref[...], w_ref[o1:o2, :])
           + _dot(oswa_ref[...], w_ref[o2:, :]))
    x1 = _ln_rows(alpha * x_ref[...] + g1_ref[...] * mix) * lng_ref[...] + lnb_ref[...]
    x1_ref[...] = x1
    h2 = (_ln_rows(x1) * (1.0 + sc2_ref[...]) + sh2_ref[...]).astype(BF16)
    h2_ref[:, 0:d] = h2.astype(h2_ref.dtype)
    g_lo, g_hi, bucket = _route(_dot_nt(rwt_ref[...], h2), rb_ref[...])
    bucket_ref[...] = bucket
    gate_rows = jnp.concatenate([g_lo, g_hi, jnp.zeros((LANES - 2, tm), F32)], axis=0)
    h2_ref[:, d:d + LANES] = gate_rows.T


def _out_proj(o_hg, o_mla, o_swa, x, mod, ln_g, ln_b, w_out, rwt, rb, rows, alpha):
    tm = rows.tm
    d = x.shape[1]
    n = rows.n_rows
    row = lambda w: pl.BlockSpec((tm, w), lambda i: (i, 0))
    whole = lambda a: pl.BlockSpec(a.shape, lambda i: (0, 0))
    return pl.pallas_call(
        functools.partial(_out_kernel, alpha=alpha), grid=(rows.n_tiles,),
        in_specs=[row(HG_W), row(MLA_HEADS * MLA_V), row(SWA_W), row(d),
                  _mod_spec(rows, 2, d), _mod_spec(rows, 3, d), _mod_spec(rows, 4, d),
                  whole(ln_g), whole(ln_b), whole(w_out), whole(rwt), whole(rb)],
        out_specs=[row(d), row(d + LANES), pl.BlockSpec((1, tm), lambda i: (0, i))],
        out_shape=[jax.ShapeDtypeStruct((n, d), F32), jax.ShapeDtypeStruct((n, d + LANES), F32),
                   jax.ShapeDtypeStruct((1, n), jnp.int32)],
        compiler_params=_cparams(("arbitrary",)), name="out_proj_ln_router",
    )(o_hg, o_mla, o_swa, x, mod, mod, mod, ln_g, ln_b, w_out, rwt, rb)


def _row_gather_start(idx_ref, base, src_hbm, dst, sem, n):
    def issue(r, carry):
        pltpu.make_async_copy(src_hbm.at[pl.ds(idx_ref[base + r], 1)], dst.at[pl.ds(r, 1)], sem).start()
        return carry
    lax.fori_loop(0, n, issue, 0, unroll=8)


def _row_gather_wait(src_hbm, dst, sem, n):
    pltpu.make_async_copy(src_hbm.at[pl.ds(0, n)], dst, sem).wait()


def _moe_kernel(stok_ref, off_ref, e1_ref, e2_ref, nused_ref, h_hbm, wg1_ref, wu1_ref, wd1_ref,
                wg2_ref, wu2_ref, wd2_ref, y_ref, hbuf, sem, *, tm):
    i = pl.program_id(0)
    n_used = nused_ref[0]
    d = y_ref.shape[1]

    def start(t):
        slot = t % 2
        _row_gather_start(stok_ref, off_ref[t], h_hbm, hbuf.at[slot], sem.at[slot], tm)

    @pl.when(i == 0)
    def _():
        start(0)

    @pl.when(i + 1 < n_used)
    def _():
        start(i + 1)

    @pl.when(i < n_used)
    def _():
        slot = i % 2
        _row_gather_wait(h_hbm, hbuf.at[slot], sem.at[slot], tm)
        h = hbuf[slot, :, 0:d].astype(BF16)
        g_lo = hbuf[slot, :, d:d + 1]
        g_hi = hbuf[slot, :, d + 1:d + 2]

        def expert(wg, wu, wd):
            act = (_silu(_dot(h, wg[...])) * _dot(h, wu[...])).astype(BF16)
            return _dot(act, wd[...])

        y_ref[...] = g_lo * expert(wg1_ref, wu1_ref, wd1_ref) + g_hi * expert(wg2_ref, wu2_ref, wd2_ref)

    @pl.when(i >= n_used)
    def _():
        y_ref[...] = jnp.zeros_like(y_ref)


def _moe(h2, stok, off, e1, e2, n_used, wg, wu, wd, tm):
    n_tiles = off.shape[0]
    dx = h2.shape[1]
    d, f = wg.shape[1], wg.shape[2]
    last = lambda i, nu: jnp.minimum(i, nu[0] - 1)
    wspec_in = lambda sel: pl.BlockSpec((None, d, f), lambda i, s, o, a, b, nu: ((a, b)[sel][last(i, nu)], 0, 0))
    wspec_dn = lambda sel: pl.BlockSpec((None, f, d), lambda i, s, o, a, b, nu: ((a, b)[sel][last(i, nu)], 0, 0))
    grid_spec = pltpu.PrefetchScalarGridSpec(
        num_scalar_prefetch=5, grid=(n_tiles,),
        in_specs=[pl.BlockSpec(memory_space=pl.ANY),
                  wspec_in(0), wspec_in(0), wspec_dn(0), wspec_in(1), wspec_in(1), wspec_dn(1)],
        out_specs=pl.BlockSpec((tm, d), lambda i, s, o, a, b, nu: (i, 0)),
        scratch_shapes=[pltpu.VMEM((2, tm, dx), F32), pltpu.SemaphoreType.DMA((2,))])
    return pl.pallas_call(
        functools.partial(_moe_kernel, tm=tm), grid_spec=grid_spec,
        out_shape=jax.ShapeDtypeStruct((n_tiles * tm, d), F32),
        compiler_params=_cparams(("arbitrary",)), name="moe_grouped",
    )(stok, off, e1, e2, n_used, h2, wg, wu, wd, wg, wu, wd)


def _moe_plan(bucket, n_tokens, tm):
    tok = jnp.arange(n_tokens, dtype=jnp.int32)
    buckets = jnp.arange(N_BUCKETS, dtype=jnp.int32)
    skey, stok = lax.sort((bucket * n_tokens + tok, tok), num_keys=1)
    counts = jnp.sum((bucket[:, None] == buckets[None, :]).astype(jnp.int32), axis=0)
    padded = (counts + tm - 1) // tm * tm
    ends = jnp.cumsum(padded)
    shift = (ends - padded) - (jnp.cumsum(counts) - counts)
    sbucket = skey // n_tokens
    slot = tok + jnp.sum(jnp.where(sbucket[:, None] == buckets[None, :], shift[None, :], 0), axis=1)
    _, pos = lax.sort((stok, slot), num_keys=1)
    n_tiles = n_tokens // tm + N_BUCKETS
    tile_start = jnp.arange(n_tiles, dtype=jnp.int32) * tm
    tile_bucket = jnp.sum((ends[None, :] <= tile_start[:, None]).astype(jnp.int32), axis=1)
    tile_bucket = jnp.minimum(tile_bucket, N_BUCKETS - 1)
    tshift = jnp.sum(jnp.where(tile_bucket[:, None] == buckets[None, :], shift[None, :], 0), axis=1)
    off = jnp.clip(tile_start - tshift, 0, n_tokens)
    grp, pair = tile_bucket // N_PAIRS, tile_bucket % N_PAIRS
    lo = jnp.where(pair < 3, 0, jnp.where(pair < 5, 1, 2))
    hi = jnp.where(pair == 0, 1, jnp.where((pair == 1) | (pair == 3), 2, 3))
    e1, e2 = grp * E_PER_GROUP + lo, grp * E_PER_GROUP + hi
    n_used = (ends[-1] // tm).astype(jnp.int32).reshape(1)
    stok = jnp.concatenate([stok, jnp.zeros((tm,), jnp.int32)])
    return stok, off.astype(jnp.int32), pos.astype(jnp.int32), e1.astype(jnp.int32), e2.astype(jnp.int32), n_used


def _ln2_kernel(pos_ref, y_hbm, x1_ref, g2_ref, lng_ref, lnb_ref, *rest, alpha, tm, emit_h):
    if emit_h:
        sh_ref, sc_ref, x2_ref, h_ref, ybuf, sem = rest
    else:
        x2_ref, ybuf, sem = rest
    i = pl.program_id(0)
    n_tiles = pl.num_programs(0)

    def start(t):
        slot = t % 2
        _row_gather_start(pos_ref, t * tm, y_hbm, ybuf.at[slot], sem.at[slot], tm)

    @pl.when(i == 0)
    def _():
        start(0)

    @pl.when(i + 1 < n_tiles)
    def _():
        start(i + 1)

    slot = i % 2
    _row_gather_wait(y_hbm, ybuf.at[slot], sem.at[slot], tm)
    x2 = _ln_rows(alpha * x1_ref[...] + g2_ref[...] * ybuf[slot]) * lng_ref[...] + lnb_ref[...]
    x2_ref[...] = x2
    if emit_h:
        h_ref[...] = (_ln_rows(x2) * (1.0 + sc_ref[...]) + sh_ref[...]).astype(h_ref.dtype)


def _ln2(pos, y_sorted, x1, mod, mod_next, ln_g, ln_b, rows, alpha):
    tm = rows.tm
    d = x1.shape[1]
    emit_h = mod_next is not None
    row = pl.BlockSpec((tm, d), lambda i, p: (i, 0))
    whole = lambda a: pl.BlockSpec(a.shape, lambda i, p: (0, 0))
    in_specs = [pl.BlockSpec(memory_space=pl.ANY), row, _mod_spec(rows, 5, d), whole(ln_g), whole(ln_b)]
    inputs = [y_sorted, x1, mod, ln_g, ln_b]
    out_specs = [row]
    out_shape = [jax.ShapeDtypeStruct((rows.n_rows, d), F32)]
    if emit_h:
        in_specs += [_mod_spec(rows, 0, d), _mod_spec(rows, 1, d)]
        inputs += [mod_next, mod_next]
        out_specs.append(row)
        out_shape.append(jax.ShapeDtypeStruct((rows.n_rows, d), BF16))
    grid_spec = pltpu.PrefetchScalarGridSpec(
        num_scalar_prefetch=1, grid=(rows.n_tiles,), in_specs=in_specs, out_specs=out_specs,
        scratch_shapes=[pltpu.VMEM((2, tm, d), F32), pltpu.SemaphoreType.DMA((2,))])
    out = pl.pallas_call(
        functools.partial(_ln2_kernel, alpha=alpha, tm=tm, emit_h=emit_h), grid_spec=grid_spec,
        out_shape=out_shape, compiler_params=_cparams(("arbitrary",)), name="unpermute_ln2",
    )(pos, *inputs)
    return out if emit_h else (out[0], None)


def _rope_tables(seq, batch, ctx_rows, dim, pad_to):
    rows = seq // GRID_W
    row = jnp.repeat(jnp.arange(rows, dtype=jnp.int32), GRID_W)
    col = jnp.tile(jnp.arange(GRID_W, dtype=jnp.int32), rows)
    nf = dim // 4
    inv_freq = ROPE_BASE ** (-jnp.arange(nf, dtype=F32) / nf)
    ang = jnp.stack([row, col], -1).astype(F32)[:, :, None] * inv_freq
    cos, sin = jnp.cos(ang), jnp.sin(ang)
    c = jnp.stack([cos, cos], axis=2).reshape(seq, dim)
    s = jnp.stack([-sin, sin], axis=2).reshape(seq, dim)
    if pad_to > dim:
        c = jnp.pad(c, ((0, 0), (0, pad_to - dim)))
        s = jnp.pad(s, ((0, 0), (0, pad_to - dim)))
    ctx_c = jnp.zeros((ctx_rows, pad_to), F32).at[:, :dim].set(1.0)
    c = jnp.concatenate([jnp.tile(c, (batch, 1)), ctx_c], axis=0)
    s = jnp.concatenate([jnp.tile(s, (batch, 1)), jnp.zeros((ctx_rows, pad_to), F32)], axis=0)
    return c, s


def _split_w_in(w):
    o1 = 5 * HG_W
    o2 = o1 + MLA_Q_LORA + MLA_KV_LORA + MLA_ROPE
    hg, mla, swa = w[:, :o1], w[:, o1:o2], w[:, o2:]
    w_raw = jnp.concatenate([hg[:, 0:HG_W], hg[:, 3 * HG_W:5 * HG_W]], axis=1)
    w_gate = hg[:, HG_W:3 * HG_W]
    kr = jnp.pad(mla[:, MLA_Q_LORA + MLA_KV_LORA:], ((0, 0), (0, LANES - MLA_ROPE)))
    w_att = jnp.concatenate([swa, mla[:, :MLA_Q_LORA + MLA_KV_LORA], kr], axis=1)
    return w_raw.astype(BF16), w_gate.astype(BF16), w_att.astype(BF16)


def _mla_weights(w_uq, w_ukv):
    qk = MLA_NOPE + MLA_ROPE
    wq = w_uq.reshape(MLA_Q_LORA, MLA_HEADS, qk)
    wq = jnp.pad(wq, ((0, 0), (0, 0), (0, MLA_QK_PAD - qk))).reshape(MLA_Q_LORA, MLA_HEADS * MLA_QK_PAD)
    wkv = w_ukv.reshape(MLA_KV_LORA, MLA_HEADS, MLA_NOPE + MLA_V)
    wkv = jnp.concatenate([wkv[:, :, :MLA_NOPE].reshape(MLA_KV_LORA, -1), wkv[:, :, MLA_NOPE:].reshape(MLA_KV_LORA, -1)], axis=1)
    return wq.astype(BF16), wkv.astype(BF16)


def kernel(x, c, ctx, c_ctx, w_ada, b_ada, w_in, w_out, hg_lb_logits, hg_norm_g, mla_q_norm, mla_kv_norm,
           mla_w_uq, mla_w_ukv, swa_sink, ln1_g, ln1_b, ln2_g, ln2_b, router_w, router_b,
           moe_w_gate, moe_w_up, moe_w_down):
    batch, seq, d = x.shape
    ctx_len = ctx.shape[1]
    depth = w_ada.shape[0]
    alpha = (2.0 * depth) ** 0.25
    n_lat, n_ctx = batch * seq, batch * ctx_len
    n_all = n_lat + n_ctx
    assert batch + 1 <= 8 and ctx_len % (4 * HG_CHUNK) == 0 and seq % ctx_len == 0 and seq % GRID_W == 0

    c8 = jnp.zeros((8, d), F32).at[:batch].set(c).at[batch].set(c_ctx)
    mod_all = _ada(c8, w_ada, b_ada).reshape(depth, 8 * 6, 1, d)

    lb = jnp.cumsum(jax.nn.softmax(hg_lb_logits.astype(F32), axis=0), axis=0)
    lb = (lb - lb[0:1]).reshape(depth, 1, 2 * HG_W)
    log_lb, log_1m, one_m = jnp.log(lb), jnp.log1p(-lb), 1.0 - lb

    cs, ss = _rope_tables(seq, batch, n_ctx, SWA_HD, SWA_HD)
    cm, sm = _rope_tables(seq, batch, n_ctx, MLA_ROPE, LANES)
    rwt = router_w.T.astype(BF16)
    rb = router_b.astype(F32).reshape(N_EXPERTS, 1)

    tm_all = _pick_tm(seq, n_ctx, 512)
    rows_all = _Rows(batch, seq, ctx_len, n_all, tm_all)
    rows_lat = _Rows(batch, seq, ctx_len, n_lat, tm_all)
    moe_tm = 256

    xa = jnp.concatenate([x.reshape(n_lat, d), ctx.reshape(n_ctx, d)], axis=0)
    h = _lnmod(xa, mod_all[0], rows_all, 0)

    for layer in range(depth):
        need_ctx = layer < depth - 1
        mod = mod_all[layer]
        rows = rows_all if need_ctx else rows_lat
        w_raw, w_gate, w_att = _split_w_in(w_in[layer])
        wq, wkv = _mla_weights(mla_w_uq[layer], mla_w_ukv[layer])

        (p_raw,) = _row_call(_mm_raw_kernel, rows_all, [h], [w_raw], [3 * HG_W], [F32], "in_proj_hg_raw")
        lf, kk = _row_call(_mm_gate_kernel, rows_all, [h], [w_gate, log_lb[layer], log_1m[layer], one_m[layer]],
                           [2 * HG_W, 2 * HG_W], [F32, F32], "in_proj_hg_gates")
        tables = [cs, ss, cm, sm]
        p_att = pl.pallas_call(
            _mm_att_kernel, grid=(rows_all.n_tiles,),
            in_specs=[pl.BlockSpec((tm_all, d), lambda i: (i, 0)), pl.BlockSpec(w_att.shape, lambda i: (0, 0))]
                     + [pl.BlockSpec((tm_all, LANES), lambda i: (i, 0))] * 4,
            out_specs=pl.BlockSpec((tm_all, ATT_COLS), lambda i: (i, 0)),
            out_shape=jax.ShapeDtypeStruct((n_all, ATT_COLS), BF16),
            compiler_params=_cparams(("arbitrary",)), name="in_proj_att",
        )(h, w_att, *tables)

        o_f = _hg_scan(p_raw, kk, lf, batch, seq, ctx_len, False)
        o_hg = _hg_scan(p_raw, kk, lf, batch, seq, ctx_len, True,
                        (o_f, hg_norm_g[layer].astype(F32).reshape(1, HG_DK)))

        q_mla, k_mla, v_mla = _mla_proj(p_att, mla_q_norm[layer].astype(F32).reshape(1, -1),
                                        mla_kv_norm[layer].astype(F32).reshape(1, -1), wq, wkv, cm, sm, rows_all)
        n_out = rows.n_rows
        o_mla = _mla_attn(q_mla, k_mla, v_mla, batch, seq, ctx_len, need_ctx)
        o_swa = _swa(p_att, swa_sink[layer].astype(F32), batch, seq, ctx_len, need_ctx)

        x1, h2, bucket = _out_proj(
            o_hg, o_mla, o_swa, xa, mod, ln1_g[layer].astype(F32).reshape(1, d), ln1_b[layer].astype(F32).reshape(1, d),
            w_out[layer].astype(BF16), rwt, rb, rows, alpha)

        stok, off, pos, e1, e2, n_used = _moe_plan(bucket[0], n_out, moe_tm)
        y_sorted = _moe(h2, stok, off, e1, e2, n_used, moe_w_gate[layer].astype(BF16),
                        moe_w_up[layer].astype(BF16), moe_w_down[layer].astype(BF16), moe_tm)
        xa, h = _ln2(pos, y_sorted, x1, mod, mod_all[layer + 1] if need_ctx else None,
                     ln2_g[layer].astype(F32).reshape(1, d), ln2_b[layer].astype(F32).reshape(1, d), rows, alpha)

    return xa[:n_lat].reshape(batch, seq, d)
```

```python
import functools
import math

import jax
import jax.numpy as jnp
from jax import lax
from jax.experimental import pallas as pl
from jax.experimental.pallas import tpu as pltpu

F32 = jnp.float32
BF16 = jnp.bfloat16

GRID_W = 64
HG_HEADS = 4
HG_DK = 128
HG_W = HG_HEADS * HG_DK
HG_CHUNK = 64
HG_SUB = 16
MLA_HEADS = 8
MLA_Q_LORA = 512
MLA_KV_LORA = 256
MLA_NOPE = 128
MLA_ROPE = 64
MLA_V = 128
MLA_QK_PAD = 256
MLA_V_PAD = 256
MLA_HEADS_PER_STEP = 4
SWA_HEADS = 4
SWA_KV_HEADS = 2
SWA_HD = 128
SWA_W = SWA_HEADS * SWA_HD
SWA_KV_W = SWA_KV_HEADS * SWA_HD
SWA_WINDOW = 128
SWA_BLOCK = 128
N_EXPERTS = 16
N_GROUPS = 4
E_PER_GROUP = 4
N_PAIRS = 6
N_BUCKETS = N_GROUPS * N_PAIRS
ROUTE_SCALE = 2.5
ROPE_BASE = 10000.0
LN_EPS = 1e-5
RMS_EPS = 1e-6
LANES = 128
VMEM_LIMIT = 56 * 1024 * 1024

ATT_SQ, ATT_SK, ATT_SV = 0, SWA_W, SWA_W + SWA_KV_W
ATT_CQ = SWA_W + 2 * SWA_KV_W
ATT_CKV = ATT_CQ + MLA_Q_LORA
ATT_KR = ATT_CKV + MLA_KV_LORA
ATT_COLS = ATT_KR + LANES


def _cparams(sem):
    return pltpu.CompilerParams(dimension_semantics=sem, vmem_limit_bytes=VMEM_LIMIT)


def _dot(a, b):
    return jnp.dot(a, b, preferred_element_type=F32)


def _dot_nt(a, b):
    return lax.dot_general(a, b, (((1,), (1,)), ((), ())), preferred_element_type=F32)


def _dot_tn(a, b):
    return lax.dot_general(a, b, (((0,), (0,)), ((), ())), preferred_element_type=F32)


def _sigmoid(x):
    return 1.0 / (1.0 + jnp.exp(-x))


def _silu(x):
    return x * _sigmoid(x)


def _ln_rows(x):
    mu = jnp.mean(x, axis=-1, keepdims=True)
    xc = x - mu
    var = jnp.mean(xc * xc, axis=-1, keepdims=True)
    return xc * lax.rsqrt(var + LN_EPS)


def _rope_partner(x, half):
    lane = lax.broadcasted_iota(jnp.int32, x.shape, x.ndim - 1)
    first = (lane & half) == 0
    n = x.shape[-1]
    return jnp.where(first, pltpu.roll(x, n - half, x.ndim - 1), pltpu.roll(x, half, x.ndim - 1))


def _ada_kernel(c_ref, w_ref, b_ref, o_ref):
    s = _silu(c_ref[...])
    o_ref[...] = _dot(s.astype(BF16), w_ref[...].astype(BF16)) + b_ref[...]


def _ada(c8, w_ada, b_ada):
    depth, d, n = w_ada.shape
    tn = 1024 if n % 1024 == 0 else n
    return pl.pallas_call(
        _ada_kernel,
        grid=(depth, n // tn),
        in_specs=[pl.BlockSpec((8, d), lambda l, j: (0, 0)),
                  pl.BlockSpec((None, d, tn), lambda l, j: (l, 0, j)),
                  pl.BlockSpec((None, 1, tn), lambda l, j: (l, 0, j))],
        out_specs=pl.BlockSpec((None, 8, tn), lambda l, j: (l, 0, j)),
        out_shape=jax.ShapeDtypeStruct((depth, 8, n), F32),
        compiler_params=_cparams(("arbitrary", "arbitrary")),
        name="ada_mod",
    )(c8, w_ada, b_ada.reshape(depth, 1, n))


class _Rows:
    def __init__(self, batch, seq, ctx_len, n_rows, tm):
        self.batch, self.seq, self.ctx_len, self.n_rows, self.tm = batch, seq, ctx_len, n_rows, tm
        self.n_tiles = n_rows // tm
        self.lat_tiles = batch * seq // tm
        self.tiles_per_batch = seq // tm

    def mod_row(self, i):
        return jnp.where(i < self.lat_tiles, i // self.tiles_per_batch, self.batch)


def _pick_tm(seq, ctx_rows, cap):
    for tm in (1024, 512, 256, 128):
        if tm <= cap and seq % tm == 0 and ctx_rows % tm == 0:
            return tm
    raise ValueError("unsupported sequence / context lengths")


def _mod_spec(rows, chunk, d):
    return pl.BlockSpec((None, 1, d), lambda i, *_: (rows.mod_row(i) * 6 + chunk, 0, 0))


def _lnmod_kernel(x_ref, sh_ref, sc_ref, h_ref):
    y = _ln_rows(x_ref[...])
    h_ref[...] = (y * (1.0 + sc_ref[...]) + sh_ref[...]).astype(h_ref.dtype)


def _lnmod(x, mod, rows, chunk0):
    d = x.shape[1]
    tm = rows.tm
    return pl.pallas_call(
        _lnmod_kernel,
        grid=(rows.n_tiles,),
        in_specs=[pl.BlockSpec((tm, d), lambda i: (i, 0)),
                  _mod_spec(rows, chunk0, d), _mod_spec(rows, chunk0 + 1, d)],
        out_specs=pl.BlockSpec((tm, d), lambda i: (i, 0)),
        out_shape=jax.ShapeDtypeStruct((rows.n_rows, d), BF16),
        compiler_params=_cparams(("arbitrary",)),
        name="ln_modulate",
    )(x, mod, mod)


def _mm_raw_kernel(h_ref, w_ref, o_ref):
    o_ref[...] = _dot(h_ref[...], w_ref[...]).astype(o_ref.dtype)


def _col_group_dots(h, w_ref, width):
    n = w_ref.shape[1]
    groups = [slice(c0, min(c0 + width, n)) for c0 in range(0, n, width)]
    return groups, [_dot(h, w_ref[:, g]) for g in groups]


def _mm_gate_kernel(h_ref, w_ref, loglb_ref, log1m_ref, onem_ref, lf_ref, k_ref):
    groups, zs = _col_group_dots(h_ref[...], w_ref, 2 * LANES)
    for g, z in zip(groups, zs):
        e = jnp.exp(-jnp.abs(z))
        r = 1.0 / (1.0 + e)
        log_sig = jnp.minimum(z, 0.0) + jnp.log(r)
        sig_neg = jnp.where(z >= 0.0, e * r, r)
        a = loglb_ref[:, g]
        b = log1m_ref[:, g] + log_sig
        lf_ref[:, g] = jnp.maximum(a, b) + jnp.log(1.0 + jnp.exp(-jnp.abs(a - b)))
        k_ref[:, g] = onem_ref[:, g] * sig_neg


def _mm_att_kernel(h_ref, w_ref, cs_ref, ss_ref, cm_ref, sm_ref, o_ref):
    groups, ps = _col_group_dots(h_ref[...], w_ref, 2 * LANES)
    cs, ss = cs_ref[...], ss_ref[...]
    swa_scale = SWA_HD ** -0.5
    for g, p in zip(groups, ps):
        for c0 in range(g.start, g.stop, LANES):
            x = p[:, c0 - g.start:c0 - g.start + LANES]
            if c0 < ATT_SV:
                x = x * cs + _rope_partner(x, SWA_HD // 4) * ss
                if c0 < ATT_SK:
                    x = x * swa_scale
            elif c0 == ATT_KR:
                x = x * cm_ref[...] + _rope_partner(x, MLA_ROPE // 4) * sm_ref[...]
            o_ref[:, c0:c0 + LANES] = x.astype(o_ref.dtype)


def _row_call(kernel, rows, row_inputs, const_inputs, out_widths, out_dtypes, name):
    tm = rows.tm
    in_specs = [pl.BlockSpec((tm, a.shape[1]), lambda i: (i, 0)) for a in row_inputs]
    in_specs += [pl.BlockSpec(a.shape, lambda i, nd=a.ndim: (0,) * nd) for a in const_inputs]
    out_specs = [pl.BlockSpec((tm, w), lambda i: (i, 0)) for w in out_widths]
    out_shape = [jax.ShapeDtypeStruct((rows.n_rows, w), dt) for w, dt in zip(out_widths, out_dtypes)]
    return pl.pallas_call(
        kernel, grid=(rows.n_tiles,), in_specs=in_specs, out_specs=out_specs, out_shape=out_shape,
        compiler_params=_cparams(("arbitrary",)), name=name,
    )(*row_inputs, *const_inputs)


def _hg_scan_kernel(*refs, reverse, n_chunks, readout):
    if readout:
        q_ref, k_ref, lf_ref, v_ref, of_ref, g_ref, ng_ref, o_ref, st_ref = refs
    else:
        q_ref, k_ref, lf_ref, v_ref, o_ref, st_ref = refs

    @pl.when(pl.program_id(1) == 0)
    def _():
        st_ref[...] = jnp.zeros_like(st_ref)

    c, s = HG_CHUNK, HG_SUB
    nsub = c // s
    chunks = [n_chunks - 1 - cc if reverse else cc for cc in range(n_chunks)]
    items = [(ch, hd) for ch in chunks for hd in range(HG_HEADS)]
    rows = lambda ch: slice(ch * c, (ch + 1) * c)
    cols = lambda hd: slice(hd * HG_DK, (hd + 1) * HG_DK)

    ri = lax.broadcasted_iota(jnp.int32, (c, c), 0)
    ci = lax.broadcasted_iota(jnp.int32, (c, c), 1)
    tri = jnp.where((ci >= ri) if reverse else (ci <= ri), 1.0, 0.0).astype(BF16)
    b_all = {}
    for ch in chunks:
        lf = lf_ref[rows(ch), :]
        hi = lf.astype(BF16)
        r1 = lf - hi.astype(F32)
        mid = r1.astype(BF16)
        lo = (r1 - mid.astype(F32)).astype(BF16)
        b_all[ch] = _dot(tri, hi) + _dot(tri, mid) + _dot(tri, lo)

    qe, k_dec, decay, v16, qt, kt = {}, {}, {}, {}, {}, {}
    for it in items:
        ch, hd = it
        b = b_all[ch][:, cols(hd)]
        q, k = q_ref[rows(ch), cols(hd)], k_ref[rows(ch), cols(hd)]
        btot = b[0:1, :] if reverse else b[c - 1:c, :]
        qe[it] = (q * jnp.exp(b)).astype(BF16)
        k_dec[it] = (k * jnp.exp(btot - b)).astype(BF16)
        decay[it] = jnp.exp(btot)
        v16[it] = v_ref[rows(ch), cols(hd)].astype(BF16)
        for i in range(nsub):
            r0 = i * s
            if reverse:
                k0, k1 = r0, c
                ref = b[r0 + s:r0 + s + 1, :] if i < nsub - 1 else jnp.zeros_like(btot)
            else:
                k0, k1 = 0, r0 + s
                ref = b[r0 - 1:r0, :] if i > 0 else jnp.zeros_like(btot)
            qt[it, i] = (q[r0:r0 + s] * jnp.exp(b[r0:r0 + s] - ref)).astype(BF16)
            kt[it, i] = (k[k0:k1] * jnp.exp(ref - b[k0:k1])).astype(BF16)

    upd = {it: _dot_tn(v16[it], k_dec[it]) for it in items}
    att = {}
    for it in items:
        for i in range(nsub):
            r0 = i * s
            k0 = r0 if reverse else 0
            a = _dot_nt(qt[it, i], kt[it, i])
            rr = lax.broadcasted_iota(jnp.int32, a.shape, 0) + r0
            cc = lax.broadcasted_iota(jnp.int32, a.shape, 1) + k0
            att[it, i] = jnp.where((cc >= rr) if reverse else (cc <= rr), a, 0.0).astype(BF16)

    states = [st_ref[hd] for hd in range(HG_HEADS)]
    o_inter = {}
    for it in items:
        ch, hd = it
        o_inter[it] = _dot_nt(qe[it], states[hd].astype(BF16))
        states[hd] = states[hd] * decay[it] + upd[it]
    for hd in range(HG_HEADS):
        st_ref[hd] = states[hd]

    for it in items:
        ch, hd = it
        outs = []
        for i in range(nsub):
            r0 = i * s
            k0, k1 = (r0, c) if reverse else (0, r0 + s)
            outs.append(o_inter[it][r0:r0 + s] + _dot(att[it, i], v16[it][k0:k1]))
        o = jnp.concatenate(outs, axis=0)
        if readout:
            o = o + of_ref[rows(ch), cols(hd)]
            o = o * lax.rsqrt(jnp.mean(o * o, axis=-1, keepdims=True) + RMS_EPS) * ng_ref[...]
            o = o * _silu(g_ref[rows(ch), cols(hd)])
        o_ref[rows(ch), cols(hd)] = o.astype(o_ref.dtype)


def _hg_scan(p_raw, kk, lf, batch, seq, ctx_len, reverse, readout_args=None):
    n_rows = p_raw.shape[0]
    tb = ctx_len
    nl = seq // tb
    ctx0 = batch * seq // tb
    direction = 1 if reverse else 0

    def row_block(b, i):
        lat = b * nl + (nl - i if reverse else i - 1)
        return jnp.where(i == 0, ctx0 + b, lat)

    def col(j):
        return pl.BlockSpec((tb, HG_W), lambda b, i: (row_block(b, i), j))

    spec = col(0)
    readout = readout_args is not None
    inputs = [p_raw, kk, lf, p_raw]
    in_specs = [col(0), col(direction), col(direction), col(1)]
    if readout:
        o_f, norm_g = readout_args
        inputs += [o_f, p_raw, norm_g]
        in_specs += [spec, col(2), pl.BlockSpec((1, HG_DK), lambda b, i: (0, 0))]
    kern = functools.partial(_hg_scan_kernel, reverse=reverse, n_chunks=tb // HG_CHUNK, readout=readout)
    return pl.pallas_call(
        kern, grid=(batch, nl + 1), in_specs=in_specs, out_specs=spec,
        out_shape=jax.ShapeDtypeStruct((n_rows, HG_W), BF16 if readout else F32),
        scratch_shapes=[pltpu.VMEM((HG_HEADS, HG_DK, HG_DK), F32)],
        compiler_params=_cparams(("arbitrary", "arbitrary")),
        name="hgrn2_scan_bwd_readout" if readout else "hgrn2_scan_fwd",
    )(*inputs)


def _mla_proj_kernel(cq_ref, ckv_ref, kr_ref, qn_ref, kvn_ref, wq_ref, wkv_ref, cm_ref, sm_ref,
                     q_ref, k_ref, v_ref):
    def rms(x, g):
        xf = x.astype(F32)
        return (xf * lax.rsqrt(jnp.mean(xf * xf, axis=-1, keepdims=True) + RMS_EPS) * g).astype(BF16)

    scale = (MLA_NOPE + MLA_ROPE) ** -0.5 * math.log2(math.e)
    _, qs = _col_group_dots(rms(cq_ref[...], qn_ref[...]), wq_ref, MLA_QK_PAD)
    _, kv2 = _col_group_dots(rms(ckv_ref[...], kvn_ref[...]), wkv_ref, 2 * LANES)
    kvs = [g[:, half * LANES:(half + 1) * LANES] for g in kv2 for half in range(2)]
    cm, sm = cm_ref[...], sm_ref[...]
    kr = kr_ref[...]
    lane = lax.broadcasted_iota(jnp.int32, (kr.shape[0], LANES), 1)
    ones_col = jnp.where(lane == 0, 1.0, 0.0).astype(v_ref.dtype)
    for hd in range(MLA_HEADS):
        c0 = hd * MLA_QK_PAD
        q_ref[:, c0:c0 + LANES] = (qs[hd][:, 0:LANES] * scale).astype(q_ref.dtype)
        x = qs[hd][:, LANES:2 * LANES]
        y = (x * cm + _rope_partner(x, MLA_ROPE // 4) * sm) * scale
        q_ref[:, c0 + LANES:c0 + 2 * LANES] = y.astype(q_ref.dtype)
        k_ref[:, c0:c0 + LANES] = kvs[hd].astype(k_ref.dtype)
        k_ref[:, c0 + LANES:c0 + 2 * LANES] = kr
        v0 = hd * MLA_V_PAD
        v_ref[:, v0:v0 + MLA_V] = kvs[MLA_HEADS + hd].astype(v_ref.dtype)
        v_ref[:, v0 + MLA_V:v0 + MLA_V_PAD] = ones_col


def _mla_proj(p_att, q_norm, kv_norm, wq, wkv, cm, sm, rows):
    tm = rows.tm
    n = rows.n_rows
    hq = MLA_HEADS * MLA_QK_PAD

    def col(width, off):
        return pl.BlockSpec((tm, width), lambda i: (i, off // width))

    def whole(a):
        return pl.BlockSpec(a.shape, lambda i: (0, 0))

    return pl.pallas_call(
        _mla_proj_kernel, grid=(rows.n_tiles,),
        in_specs=[col(MLA_Q_LORA, ATT_CQ), col(MLA_KV_LORA, ATT_CKV), col(LANES, ATT_KR),
                  whole(q_norm), whole(kv_norm), whole(wq), whole(wkv),
                  pl.BlockSpec((tm, LANES), lambda i: (i, 0)), pl.BlockSpec((tm, LANES), lambda i: (i, 0))],
        out_specs=[pl.BlockSpec((tm, hq), lambda i: (i, 0)), pl.BlockSpec((tm, hq), lambda i: (i, 0)),
                   pl.BlockSpec((tm, MLA_HEADS * MLA_V_PAD), lambda i: (i, 0))],
        out_shape=[jax.ShapeDtypeStruct((n, hq), BF16), jax.ShapeDtypeStruct((n, hq), BF16),
                   jax.ShapeDtypeStruct((n, MLA_HEADS * MLA_V_PAD), BF16)],
        compiler_params=_cparams(("arbitrary",)), name="mla_proj",
    )(p_att, p_att, p_att, q_norm, kv_norm, wq, wkv, cm, sm)


def _lane_tile_fold(x, op):
    out = x[:, 0:LANES]
    for t in range(1, x.shape[1] // LANES):
        out = op(out, x[:, t * LANES:(t + 1) * LANES])
    return out


def _mla_attn_kernel(q_ref, kl_ref, vl_ref, kc_ref, vc_ref, o_ref, *, nq, ck, ctx_queries):
    seq = kl_ref.shape[0]

    def attend(with_lat):
        chunks = [(kc_ref, vc_ref, 0, kc_ref.shape[0])]
        if with_lat:
            chunks += [(kl_ref, vl_ref, c * ck, ck) for c in range(seq // ck)]
        m = [None] * MLA_HEADS_PER_STEP
        acc = [None] * MLA_HEADS_PER_STEP
        for k_ref, v_ref, r0, n in chunks:
            for hd in range(MLA_HEADS_PER_STEP):
                qk = slice(hd * MLA_QK_PAD, (hd + 1) * MLA_QK_PAD)
                s = _dot_nt(q_ref[:, qk], k_ref[r0:r0 + n, qk])
                m_c = jnp.max(_lane_tile_fold(s, jnp.maximum), axis=-1, keepdims=True)
                m_new = m_c if m[hd] is None else jnp.maximum(m[hd], m_c)
                pv = _dot(jnp.exp2(s - m_new).astype(BF16), v_ref[r0:r0 + n, hd * MLA_V_PAD:(hd + 1) * MLA_V_PAD])
                acc[hd] = pv if m[hd] is None else acc[hd] * jnp.exp2(m[hd] - m_new) + pv
                m[hd] = m_new
        for hd in range(MLA_HEADS_PER_STEP):
            o = acc[hd][:, 0:MLA_V] / acc[hd][:, MLA_V:MLA_V + 1]
            o_ref[:, hd * MLA_V:(hd + 1) * MLA_V] = o.astype(o_ref.dtype)

    if ctx_queries:
        i = pl.program_id(2)
        pl.when(i < nq)(lambda: attend(True))
        pl.when(i >= nq)(lambda: attend(False))
    else:
        attend(True)


def _mla_attn(q, k, v, batch, seq, ctx_len, ctx_queries):
    tq = ctx_len
    nq = seq // tq
    ctx0 = batch * seq // ctx_len
    q_row = lambda b, h, i: jnp.where(i < nq, b * nq + i, ctx0 + b)
    n_rows = batch * seq + (batch * ctx_len if ctx_queries else 0)
    ck = 1024 if seq % 1024 == 0 else ctx_len
    hps = MLA_HEADS_PER_STEP
    return pl.pallas_call(
        functools.partial(_mla_attn_kernel, nq=nq, ck=ck, ctx_queries=ctx_queries),
        grid=(batch, MLA_HEADS // hps, nq + (1 if ctx_queries else 0)),
        in_specs=[pl.BlockSpec((tq, hps * MLA_QK_PAD), lambda b, h, i: (q_row(b, h, i), h)),
                  pl.BlockSpec((seq, hps * MLA_QK_PAD), lambda b, h, i: (b, h)),
                  pl.BlockSpec((seq, hps * MLA_V_PAD), lambda b, h, i: (b, h)),
                  pl.BlockSpec((ctx_len, hps * MLA_QK_PAD), lambda b, h, i: (ctx0 + b, h)),
                  pl.BlockSpec((ctx_len, hps * MLA_V_PAD), lambda b, h, i: (ctx0 + b, h))],
        out_specs=pl.BlockSpec((tq, hps * MLA_V), lambda b, h, i: (q_row(b, h, i), h)),
        out_shape=jax.ShapeDtypeStruct((n_rows, MLA_HEADS * MLA_V), BF16),
        compiler_params=_cparams(("arbitrary", "arbitrary", "arbitrary")),
        name="mla_attn",
    )(q, k, v, k, v)


def _swa_kernel(sink_ref, q_ref, kp_ref, kn_ref, kx_ref, kc_ref, vp_ref, vn_ref, vx_ref, vc_ref, o_ref,
                *, nb, seq, ctx_queries):
    n = pl.program_id(1)
    g = SWA_HEADS // SWA_KV_HEADS
    blk = SWA_BLOCK

    def attend(band):
        if band:
            a = lax.broadcasted_iota(jnp.int32, (g * blk, 3 * blk), 0) & (blk - 1)
            j = lax.broadcasted_iota(jnp.int32, (g * blk, 3 * blk), 1)
            key_pos = (n - 1) * blk + j
            keep = (jnp.abs(j - blk - a) <= SWA_WINDOW) & (key_pos >= 0) & (key_pos < seq)
        heads = range(SWA_KV_HEADS)
        ks = [slice(kh * SWA_HD, (kh + 1) * SWA_HD) for kh in heads]
        row = lax.broadcasted_iota(jnp.int32, (g * blk, 1), 0)
        qs, sink, s_c, s_b = [], [], [], []
        for kh in heads:
            qs.append(jnp.concatenate(
                [q_ref[:, (kh * g + gi) * SWA_HD:(kh * g + gi + 1) * SWA_HD] for gi in range(g)], axis=0))
            sk = jnp.zeros((g * blk, 1), F32)
            for gi in range(g):
                sk = jnp.where((row >= gi * blk) & (row < (gi + 1) * blk), sink_ref[kh * g + gi], sk)
            sink.append(sk)
            s_c.append(_dot_nt(qs[kh], kc_ref[:, ks[kh]]))
            if band:
                kb = jnp.concatenate([kp_ref[:, ks[kh]], kn_ref[:, ks[kh]], kx_ref[:, ks[kh]]], axis=0)
                s_b.append(jnp.where(keep, _dot_nt(qs[kh], kb), -jnp.inf))
        p_c, p_b, den = [], [], []
        for kh in heads:
            m = jnp.maximum(jnp.max(s_c[kh], axis=-1, keepdims=True), sink[kh])
            if band:
                m = jnp.maximum(m, jnp.max(s_b[kh], axis=-1, keepdims=True))
            pc = jnp.exp(s_c[kh] - m)
            dn = jnp.sum(pc, axis=-1, keepdims=True) + jnp.exp(sink[kh] - m)
            if band:
                pb = jnp.exp(s_b[kh] - m)
                dn = dn + jnp.sum(pb, axis=-1, keepdims=True)
                p_b.append(pb.astype(BF16))
            p_c.append(pc.astype(BF16))
            den.append(dn)
        for kh in heads:
            acc = _dot(p_c[kh], vc_ref[:, ks[kh]])
            if band:
                vb = jnp.concatenate([vp_ref[:, ks[kh]], vn_ref[:, ks[kh]], vx_ref[:, ks[kh]]], axis=0)
                acc = acc + _dot(p_b[kh], vb)
            o = acc / den[kh]
            for gi in range(g):
                c0 = (kh * g + gi) * SWA_HD
                o_ref[:, c0:c0 + SWA_HD] = o[gi * blk:(gi + 1) * blk].astype(o_ref.dtype)

    if ctx_queries:
        pl.when(n < nb)(lambda: attend(True))
        pl.when(n >= nb)(lambda: attend(False))
    else:
        attend(True)


def _swa(p_att, sink, batch, seq, ctx_len, ctx_queries):
    blk = SWA_BLOCK
    nb, ncb = seq // blk, ctx_len // blk
    ctx0 = batch * seq // ctx_len
    kcol, vcol = ATT_SK // SWA_KV_W, ATT_SV // SWA_KV_W
    q_row = lambda b, n: jnp.where(n < nb, b * nb + n, batch * nb + b * ncb + (n - nb))
    n_rows = batch * seq + (batch * ctx_len if ctx_queries else 0)

    def nbr(col, d):
        return pl.BlockSpec((blk, SWA_KV_W), lambda b, n: (b * nb + jnp.clip(n + d, 0, nb - 1), col))

    ctx_k = pl.BlockSpec((ctx_len, SWA_KV_W), lambda b, n: (ctx0 + b, kcol))
    ctx_v = pl.BlockSpec((ctx_len, SWA_KV_W), lambda b, n: (ctx0 + b, vcol))
    return pl.pallas_call(
        functools.partial(_swa_kernel, nb=nb, seq=seq, ctx_queries=ctx_queries),
        grid=(batch, nb + (ncb if ctx_queries else 0)),
        in_specs=[pl.BlockSpec(memory_space=pltpu.SMEM),
                  pl.BlockSpec((blk, SWA_W), lambda b, n: (q_row(b, n), 0)),
                  nbr(kcol, -1), nbr(kcol, 0), nbr(kcol, 1), ctx_k, nbr(vcol, -1), nbr(vcol, 0), nbr(vcol, 1), ctx_v],
        out_specs=pl.BlockSpec((blk, SWA_W), lambda b, n: (q_row(b, n), 0)),
        out_shape=jax.ShapeDtypeStruct((n_rows, SWA_W), BF16),
        compiler_params=_cparams(("arbitrary", "arbitrary")),
        name="swa_attn",
    )(sink, *([p_att] * 9))


def _route(logits, rb):
    aff = _sigmoid(logits)
    sel = aff + rb
    tm = logits.shape[1]
    scores = []
    for g in range(N_GROUPS):
        r = [sel[g * E_PER_GROUP + j:g * E_PER_GROUP + j + 1] for j in range(E_PER_GROUP)]
        best = None
        for i in range(E_PER_GROUP):
            for j in range(i + 1, E_PER_GROUP):
                pair = r[i] + r[j]
                best = pair if best is None else jnp.maximum(best, pair)
        scores.append(best)
    gbest, gi = scores[0], jnp.zeros((1, tm), jnp.int32)
    for g in range(1, N_GROUPS):
        upd = scores[g] > gbest
        gbest = jnp.where(upd, scores[g], gbest)
        gi = jnp.where(upd, g, gi)
    s_in, a_in = [], []
    for j in range(E_PER_GROUP):
        sj, aj = sel[j:j + 1], aff[j:j + 1]
        for g in range(1, N_GROUPS):
            e = g * E_PER_GROUP + j
            sj = jnp.where(gi == g, sel[e:e + 1], sj)
            aj = jnp.where(gi == g, aff[e:e + 1], aj)
        s_in.append(sj)
        a_in.append(aj)
    chosen = []
    for j in range(E_PER_GROUP):
        rank = jnp.zeros((1, tm), jnp.int32)
        for k in range(E_PER_GROUP):
            if k == j:
                continue
            ahead = (s_in[k] >= s_in[j]) if k < j else (s_in[k] > s_in[j])
            rank = rank + ahead.astype(jnp.int32)
        chosen.append(rank < 2)
    w = [jnp.where(chosen[j], a_in[j], 0.0) for j in range(E_PER_GROUP)]
    wsum = w[0] + w[1] + w[2] + w[3]
    gate_in = [wj / wsum * ROUTE_SCALE for wj in w]
    code = sum(jnp.where(chosen[j], 1 << j, 0) for j in range(E_PER_GROUP))
    pair = jnp.zeros((1, tm), jnp.int32)
    for idx, cval in enumerate((3, 5, 9, 6, 10, 12)):
        pair = jnp.where(code == cval, idx, pair)
    bucket = gi * N_PAIRS + pair
    g_lo = jnp.zeros((1, tm), F32)
    g_hi = jnp.zeros((1, tm), F32)
    seen = jnp.zeros((1, tm), jnp.bool_)
    for j in range(E_PER_GROUP):
        g_lo = jnp.where(chosen[j] & ~seen, gate_in[j], g_lo)
        g_hi = jnp.where(chosen[j] & seen, gate_in[j], g_hi)
        seen = seen | chosen[j]
    return g_lo, g_hi, bucket


def _out_kernel(ohg_ref, omla_ref, oswa_ref, x_ref, g1_ref, sh2_ref, sc2_ref, lng_ref, lnb_ref,
                w_ref, rwt_ref, rb_ref, x1_ref, h2_ref, bucket_ref, *, alpha):
    o1, o2 = HG_W, HG_W + MLA_HEADS * MLA_V
    d = x_ref.shape[1]
    tm = x_ref.shape[0]
    halves = [slice(0, tm // 2), slice(tm // 2, tm)]
    mixes = [_dot(ohg_ref[rs, :], w_ref[0:o1, :]) + _dot(omla_ref[rs, :], w_ref[o1:o2, :])
             + _dot(oswa_ref[rs, :], w_ref[o2:, :]) for rs in halves]
    for rs, mix in zip(halves, mixes):
        x1 = _ln_rows(alpha * x_ref[rs, :] + g1_ref[...] * mix) * lng_ref[...] + lnb_ref[...]
        x1_ref[rs, :] = x1
        h2 = (_ln_rows(x1) * (1.0 + sc2_ref[...]) + sh2_ref[...]).astype(BF16)
        h2_ref[rs, 0:d] = h2.astype(h2_ref.dtype)
        g_lo, g_hi, bucket = _route(_dot_nt(rwt_ref[...], h2), rb_ref[...])
        bucket_ref[:, rs] = bucket
        gate_rows = jnp.concatenate([g_lo, g_hi, jnp.zeros((LANES - 2, tm // 2), F32)], axis=0)
        h2_ref[rs, d:d + LANES] = gate_rows.T


def _out_proj(o_hg, o_mla, o_swa, x, mod, ln_g, ln_b, w_out, rwt, rb, rows, alpha):
    tm = rows.tm
    d = x.shape[1]
    n = rows.n_rows
    row = lambda w: pl.BlockSpec((tm, w), lambda i: (i, 0))
    whole = lambda a: pl.BlockSpec(a.shape, lambda i: (0, 0))
    return pl.pallas_call(
        functools.partial(_out_kernel, alpha=alpha), grid=(rows.n_tiles,),
        in_specs=[row(HG_W), row(MLA_HEADS * MLA_V), row(SWA_W), row(d),
                  _mod_spec(rows, 2, d), _mod_spec(rows, 3, d), _mod_spec(rows, 4, d),
                  whole(ln_g), whole(ln_b), whole(w_out), whole(rwt), whole(rb)],
        out_specs=[row(d), row(d + LANES), pl.BlockSpec((1, tm), lambda i: (0, i))],
        out_shape=[jax.ShapeDtypeStruct((n, d), F32), jax.ShapeDtypeStruct((n, d + LANES), F32),
                   jax.ShapeDtypeStruct((1, n), jnp.int32)],
        compiler_params=_cparams(("arbitrary",)), name="out_proj_ln_router",
    )(o_hg, o_mla, o_swa, x, mod, mod, mod, ln_g, ln_b, w_out, rwt, rb)


def _row_gather_start(idx_ref, base, src_hbm, dst, sem, n):
    def issue(r, carry):
        pltpu.make_async_copy(src_hbm.at[pl.ds(idx_ref[base + r], 1)], dst.at[pl.ds(r, 1)], sem).start()
        return carry
    lax.fori_loop(0, n, issue, 0, unroll=8)


def _row_gather_wait(src_hbm, dst, sem, n):
    pltpu.make_async_copy(src_hbm.at[pl.ds(0, n)], dst, sem).wait()


def _moe_kernel(stok_ref, off_ref, e1_ref, e2_ref, nused_ref, h_hbm, wg1_ref, wu1_ref, wd1_ref,
                wg2_ref, wu2_ref, wd2_ref, y_ref, hbuf, sem, *, tm):
    i = pl.program_id(0)
    n_used = nused_ref[0]
    d = y_ref.shape[1]

    def start(t):
        slot = t % 2
        _row_gather_start(stok_ref, off_ref[t], h_hbm, hbuf.at[slot], sem.at[slot], tm)

    @pl.when(i == 0)
    def _():
        start(0)

    @pl.when(i + 1 < n_used)
    def _():
        start(i + 1)

    @pl.when(i < n_used)
    def _():
        slot = i % 2
        _row_gather_wait(h_hbm, hbuf.at[slot], sem.at[slot], tm)
        h = hbuf[slot, :, 0:d].astype(BF16)
        g_lo = hbuf[slot, :, d:d + 1]
        g_hi = hbuf[slot, :, d + 1:d + 2]

        act1 = (_silu(_dot(h, wg1_ref[...])) * _dot(h, wu1_ref[...])).astype(BF16)
        act2 = (_silu(_dot(h, wg2_ref[...])) * _dot(h, wu2_ref[...])).astype(BF16)
        y_ref[...] = g_lo * _dot(act1, wd1_ref[...]) + g_hi * _dot(act2, wd2_ref[...])

    @pl.when(i >= n_used)
    def _():
        y_ref[...] = jnp.zeros_like(y_ref)


def _moe(h2, stok, off, e1, e2, n_used, wg, wu, wd, tm):
    n_tiles = off.shape[0]
    dx = h2.shape[1]
    d, f = wg.shape[1], wg.shape[2]
    last = lambda i, nu: jnp.minimum(i, nu[0] - 1)
    wspec_in = lambda sel: pl.BlockSpec((None, d, f), lambda i, s, o, a, b, nu: ((a, b)[sel][last(i, nu)], 0, 0))
    wspec_dn = lambda sel: pl.BlockSpec((None, f, d), lambda i, s, o, a, b, nu: ((a, b)[sel][last(i, nu)], 0, 0))
    grid_spec = pltpu.PrefetchScalarGridSpec(
        num_scalar_prefetch=5, grid=(n_tiles,),
        in_specs=[pl.BlockSpec(memory_space=pl.ANY),
                  wspec_in(0), wspec_in(0), wspec_dn(0), wspec_in(1), wspec_in(1), wspec_dn(1)],
        out_specs=pl.BlockSpec((tm, d), lambda i, s, o, a, b, nu: (i, 0)),
        scratch_shapes=[pltpu.VMEM((2, tm, dx), F32), pltpu.SemaphoreType.DMA((2,))])
    return pl.pallas_call(
        functools.partial(_moe_kernel, tm=tm), grid_spec=grid_spec,
        out_shape=jax.ShapeDtypeStruct((n_tiles * tm, d), F32),
        compiler_params=_cparams(("arbitrary",)), name="moe_grouped",
    )(stok, off, e1, e2, n_used, h2, wg, wu, wd, wg, wu, wd)


def _moe_plan(bucket, n_tokens, tm):
    tok = jnp.arange(n_tokens, dtype=jnp.int32)
    buckets = jnp.arange(N_BUCKETS, dtype=jnp.int32)
    skey, stok = lax.sort((bucket * n_tokens + tok, tok), num_keys=1)
    counts = jnp.sum((bucket[:, None] == buckets[None, :]).astype(jnp.int32), axis=0)
    padded = (counts + tm - 1) // tm * tm
    ends = jnp.cumsum(padded)
    shift = (ends - padded) - (jnp.cumsum(counts) - counts)
    sbucket = skey // n_tokens
    slot = tok + jnp.sum(jnp.where(sbucket[:, None] == buckets[None, :], shift[None, :], 0), axis=1)
    _, pos = lax.sort((stok, slot), num_keys=1)
    n_tiles = n_tokens // tm + N_BUCKETS
    tile_start = jnp.arange(n_tiles, dtype=jnp.int32) * tm
    tile_bucket = jnp.sum((ends[None, :] <= tile_start[:, None]).astype(jnp.int32), axis=1)
    tile_bucket = jnp.minimum(tile_bucket, N_BUCKETS - 1)
    tshift = jnp.sum(jnp.where(tile_bucket[:, None] == buckets[None, :], shift[None, :], 0), axis=1)
    off = jnp.clip(tile_start - tshift, 0, n_tokens)
    grp, pair = tile_bucket // N_PAIRS, tile_bucket % N_PAIRS
    lo = jnp.where(pair < 3, 0, jnp.where(pair < 5, 1, 2))
    hi = jnp.where(pair == 0, 1, jnp.where((pair == 1) | (pair == 3), 2, 3))
    e1, e2 = grp * E_PER_GROUP + lo, grp * E_PER_GROUP + hi
    n_used = (ends[-1] // tm).astype(jnp.int32).reshape(1)
    stok = jnp.concatenate([stok, jnp.zeros((tm,), jnp.int32)])
    return stok, off.astype(jnp.int32), pos.astype(jnp.int32), e1.astype(jnp.int32), e2.astype(jnp.int32), n_used


def _ln2_kernel(pos_ref, y_hbm, x1_ref, g2_ref, lng_ref, lnb_ref, *rest, alpha, tm, emit_h):
    if emit_h:
        sh_ref, sc_ref, x2_ref, h_ref, ybuf, sem = rest
    else:
        x2_ref, ybuf, sem = rest
    i = pl.program_id(0)
    n_tiles = pl.num_programs(0)

    def start(t):
        slot = t % 2
        _row_gather_start(pos_ref, t * tm, y_hbm, ybuf.at[slot], sem.at[slot], tm)

    @pl.when(i == 0)
    def _():
        start(0)

    @pl.when(i + 1 < n_tiles)
    def _():
        start(i + 1)

    slot = i % 2
    _row_gather_wait(y_hbm, ybuf.at[slot], sem.at[slot], tm)
    x2 = _ln_rows(alpha * x1_ref[...] + g2_ref[...] * ybuf[slot]) * lng_ref[...] + lnb_ref[...]
    x2_ref[...] = x2
    if emit_h:
        h_ref[...] = (_ln_rows(x2) * (1.0 + sc_ref[...]) + sh_ref[...]).astype(h_ref.dtype)


def _ln2(pos, y_sorted, x1, mod, mod_next, ln_g, ln_b, rows, alpha):
    tm = rows.tm
    d = x1.shape[1]
    emit_h = mod_next is not None
    row = pl.BlockSpec((tm, d), lambda i, p: (i, 0))
    whole = lambda a: pl.BlockSpec(a.shape, lambda i, p: (0, 0))
    in_specs = [pl.BlockSpec(memory_space=pl.ANY), row, _mod_spec(rows, 5, d), whole(ln_g), whole(ln_b)]
    inputs = [y_sorted, x1, mod, ln_g, ln_b]
    out_specs = [row]
    out_shape = [jax.ShapeDtypeStruct((rows.n_rows, d), F32)]
    if emit_h:
        in_specs += [_mod_spec(rows, 0, d), _mod_spec(rows, 1, d)]
        inputs += [mod_next, mod_next]
        out_specs.append(row)
        out_shape.append(jax.ShapeDtypeStruct((rows.n_rows, d), BF16))
    grid_spec = pltpu.PrefetchScalarGridSpec(
        num_scalar_prefetch=1, grid=(rows.n_tiles,), in_specs=in_specs, out_specs=out_specs,
        scratch_shapes=[pltpu.VMEM((2, tm, d), F32), pltpu.SemaphoreType.DMA((2,))])
    out = pl.pallas_call(
        functools.partial(_ln2_kernel, alpha=alpha, tm=tm, emit_h=emit_h), grid_spec=grid_spec,
        out_shape=out_shape, compiler_params=_cparams(("arbitrary",)), name="unpermute_ln2",
    )(pos, *inputs)
    return out if emit_h else (out[0], None)


def _rope_tables(seq, batch, ctx_rows, dim, pad_to):
    rows = seq // GRID_W
    row = jnp.repeat(jnp.arange(rows, dtype=jnp.int32), GRID_W)
    col = jnp.tile(jnp.arange(GRID_W, dtype=jnp.int32), rows)
    nf = dim // 4
    inv_freq = ROPE_BASE ** (-jnp.arange(nf, dtype=F32) / nf)
    ang = jnp.stack([row, col], -1).astype(F32)[:, :, None] * inv_freq
    cos, sin = jnp.cos(ang), jnp.sin(ang)
    c = jnp.stack([cos, cos], axis=2).reshape(seq, dim)
    s = jnp.stack([-sin, sin], axis=2).reshape(seq, dim)
    if pad_to > dim:
        c = jnp.pad(c, ((0, 0), (0, pad_to - dim)))
        s = jnp.pad(s, ((0, 0), (0, pad_to - dim)))
    ctx_c = jnp.zeros((ctx_rows, pad_to), F32).at[:, :dim].set(1.0)
    c = jnp.concatenate([jnp.tile(c, (batch, 1)), ctx_c], axis=0)
    s = jnp.concatenate([jnp.tile(s, (batch, 1)), jnp.zeros((ctx_rows, pad_to), F32)], axis=0)
    return c, s


def _split_w_in(w):
    o1 = 5 * HG_W
    o2 = o1 + MLA_Q_LORA + MLA_KV_LORA + MLA_ROPE
    hg, mla, swa = w[:, :o1], w[:, o1:o2], w[:, o2:]
    w_raw = jnp.concatenate([hg[:, 0:HG_W], hg[:, 3 * HG_W:5 * HG_W]], axis=1)
    w_gate = hg[:, HG_W:3 * HG_W]
    kr = jnp.pad(mla[:, MLA_Q_LORA + MLA_KV_LORA:], ((0, 0), (0, LANES - MLA_ROPE)))
    w_att = jnp.concatenate([swa, mla[:, :MLA_Q_LORA + MLA_KV_LORA], kr], axis=1)
    return w_raw.astype(BF16), w_gate.astype(BF16), w_att.astype(BF16)


def _mla_weights(w_uq, w_ukv):
    qk = MLA_NOPE + MLA_ROPE
    wq = w_uq.reshape(MLA_Q_LORA, MLA_HEADS, qk)
    wq = jnp.pad(wq, ((0, 0), (0, 0), (0, MLA_QK_PAD - qk))).reshape(MLA_Q_LORA, MLA_HEADS * MLA_QK_PAD)
    wkv = w_ukv.reshape(MLA_KV_LORA, MLA_HEADS, MLA_NOPE + MLA_V)
    wkv = jnp.concatenate([wkv[:, :, :MLA_NOPE].reshape(MLA_KV_LORA, -1), wkv[:, :, MLA_NOPE:].reshape(MLA_KV_LORA, -1)], axis=1)
    return wq.astype(BF16), wkv.astype(BF16)


def kernel(x, c, ctx, c_ctx, w_ada, b_ada, w_in, w_out, hg_lb_logits, hg_norm_g, mla_q_norm, mla_kv_norm,
           mla_w_uq, mla_w_ukv, swa_sink, ln1_g, ln1_b, ln2_g, ln2_b, router_w, router_b,
           moe_w_gate, moe_w_up, moe_w_down):
    batch, seq, d = x.shape
    ctx_len = ctx.shape[1]
    depth = w_ada.shape[0]
    alpha = (2.0 * depth) ** 0.25
    n_lat, n_ctx = batch * seq, batch * ctx_len
    n_all = n_lat + n_ctx
    assert batch + 1 <= 8 and ctx_len % (4 * HG_CHUNK) == 0 and seq % ctx_len == 0 and seq % GRID_W == 0

    c8 = jnp.zeros((8, d), F32).at[:batch].set(c).at[batch].set(c_ctx)
    mod_all = _ada(c8, w_ada, b_ada).reshape(depth, 8 * 6, 1, d)

    lb = jnp.cumsum(jax.nn.softmax(hg_lb_logits.astype(F32), axis=0), axis=0)
    lb = (lb - lb[0:1]).reshape(depth, 1, 2 * HG_W)
    log_lb, log_1m, one_m = jnp.log(lb), jnp.log1p(-lb), 1.0 - lb

    cs, ss = _rope_tables(seq, batch, n_ctx, SWA_HD, SWA_HD)
    cm, sm = _rope_tables(seq, batch, n_ctx, MLA_ROPE, LANES)
    rwt = router_w.T.astype(BF16)
    rb = router_b.astype(F32).reshape(N_EXPERTS, 1)

    tm_all = _pick_tm(seq, n_ctx, 512)
    rows_all = _Rows(batch, seq, ctx_len, n_all, tm_all)
    rows_lat = _Rows(batch, seq, ctx_len, n_lat, tm_all)
    moe_tm = 256

    xa = jnp.concatenate([x.reshape(n_lat, d), ctx.reshape(n_ctx, d)], axis=0)
    h = _lnmod(xa, mod_all[0], rows_all, 0)

    for layer in range(depth):
        need_ctx = layer < depth - 1
        mod = mod_all[layer]
        rows = rows_all if need_ctx else rows_lat
        w_raw, w_gate, w_att = _split_w_in(w_in[layer])
        wq, wkv = _mla_weights(mla_w_uq[layer], mla_w_ukv[layer])

        (p_raw,) = _row_call(_mm_raw_kernel, rows_all, [h], [w_raw], [3 * HG_W], [F32], "in_proj_hg_raw")
        lf, kk = _row_call(_mm_gate_kernel, rows_all, [h], [w_gate, log_lb[layer], log_1m[layer], one_m[layer]],
                           [2 * HG_W, 2 * HG_W], [F32, F32], "in_proj_hg_gates")
        tables = [cs, ss, cm, sm]
        p_att = pl.pallas_call(
            _mm_att_kernel, grid=(rows_all.n_tiles,),
            in_specs=[pl.BlockSpec((tm_all, d), lambda i: (i, 0)), pl.BlockSpec(w_att.shape, lambda i: (0, 0))]
                     + [pl.BlockSpec((tm_all, LANES), lambda i: (i, 0))] * 4,
            out_specs=pl.BlockSpec((tm_all, ATT_COLS), lambda i: (i, 0)),
            out_shape=jax.ShapeDtypeStruct((n_all, ATT_COLS), BF16),
            compiler_params=_cparams(("arbitrary",)), name="in_proj_att",
        )(h, w_att, *tables)

        o_f = _hg_scan(p_raw, kk, lf, batch, seq, ctx_len, False)
        o_hg = _hg_scan(p_raw, kk, lf, batch, seq, ctx_len, True,
                        (o_f, hg_norm_g[layer].astype(F32).reshape(1, HG_DK)))

        q_mla, k_mla, v_mla = _mla_proj(p_att, mla_q_norm[layer].astype(F32).reshape(1, -1),
                                        mla_kv_norm[layer].astype(F32).reshape(1, -1), wq, wkv, cm, sm, rows_all)
        n_out = rows.n_rows
        o_mla = _mla_attn(q_mla, k_mla, v_mla, batch, seq, ctx_len, need_ctx)
        o_swa = _swa(p_att, swa_sink[layer].astype(F32), batch, seq, ctx_len, need_ctx)

        x1, h2, bucket = _out_proj(
            o_hg, o_mla, o_swa, xa, mod, ln1_g[layer].astype(F32).reshape(1, d), ln1_b[layer].astype(F32).reshape(1, d),
            w_out[layer].astype(BF16), rwt, rb, rows, alpha)

        stok, off, pos, e1, e2, n_used = _moe_plan(bucket[0], n_out, moe_tm)
        y_sorted = _moe(h2, stok, off, e1, e2, n_used, moe_w_gate[layer].astype(BF16),
                        moe_w_up[layer].astype(BF16), moe_w_down[layer].astype(BF16), moe_tm)
        xa, h = _ln2(pos, y_sorted, x1, mod, mod_all[layer + 1] if need_ctx else None,
                     ln2_g[layer].astype(F32).reshape(1, d), ln2_b[layer].astype(F32).reshape(1, d), rows, alpha)

    return xa[:n_lat].reshape(batch, seq, d)
```

```python
import functools
import math

import jax
import jax.numpy as jnp
from jax import lax
from jax.experimental import pallas as pl
from jax.experimental.pallas import tpu as pltpu

F32 = jnp.float32
BF16 = jnp.bfloat16

GRID_W = 64
HG_HEADS = 4
HG_DK = 128
HG_W = HG_HEADS * HG_DK
HG_CHUNK = 64
HG_SUB = 16
MLA_HEADS = 8
MLA_Q_LORA = 512
MLA_KV_LORA = 256
MLA_NOPE = 128
MLA_ROPE = 64
MLA_V = 128
MLA_QK_PAD = 256
MLA_V_PAD = 256
MLA_HEADS_PER_STEP = 4
MLA_TQ = 512
SWA_HEADS = 4
SWA_KV_HEADS = 2
SWA_HD = 128
SWA_W = SWA_HEADS * SWA_HD
SWA_KV_W = SWA_KV_HEADS * SWA_HD
SWA_WINDOW = 128
SWA_BLOCK = 128
N_EXPERTS = 16
N_GROUPS = 4
E_PER_GROUP = 4
N_PAIRS = 6
N_BUCKETS = N_GROUPS * N_PAIRS
ROUTE_SCALE = 2.5
ROPE_BASE = 10000.0
LN_EPS = 1e-5
RMS_EPS = 1e-6
LANES = 128
VMEM_LIMIT = 56 * 1024 * 1024

ATT_SQ, ATT_SK, ATT_SV = 0, SWA_W, SWA_W + SWA_KV_W
ATT_CQ = SWA_W + 2 * SWA_KV_W
ATT_CKV = ATT_CQ + MLA_Q_LORA
ATT_KR = ATT_CKV + MLA_KV_LORA
ATT_COLS = ATT_KR + LANES


def _cparams(sem):
    return pltpu.CompilerParams(dimension_semantics=sem, vmem_limit_bytes=VMEM_LIMIT)


def _dot(a, b):
    return jnp.dot(a, b, preferred_element_type=F32)


def _dot_nt(a, b):
    return lax.dot_general(a, b, (((1,), (1,)), ((), ())), preferred_element_type=F32)


def _dot_tn(a, b):
    return lax.dot_general(a, b, (((0,), (0,)), ((), ())), preferred_element_type=F32)


def _sigmoid(x):
    return 1.0 / (1.0 + jnp.exp(-x))


def _silu(x):
    return x * _sigmoid(x)


def _ln_rows(x):
    mu = jnp.mean(x, axis=-1, keepdims=True)
    xc = x - mu
    var = jnp.mean(xc * xc, axis=-1, keepdims=True)
    return xc * lax.rsqrt(var + LN_EPS)


def _rope_partner(x, half):
    lane = lax.broadcasted_iota(jnp.int32, x.shape, x.ndim - 1)
    first = (lane & half) == 0
    n = x.shape[-1]
    return jnp.where(first, pltpu.roll(x, n - half, x.ndim - 1), pltpu.roll(x, half, x.ndim - 1))


def _ada_kernel(c_ref, w_ref, b_ref, o_ref):
    s = _silu(c_ref[...])
    o_ref[...] = _dot(s.astype(BF16), w_ref[...].astype(BF16)) + b_ref[...]


def _ada(c8, w_ada, b_ada):
    depth, d, n = w_ada.shape
    tn = 1024 if n % 1024 == 0 else n
    return pl.pallas_call(
        _ada_kernel,
        grid=(depth, n // tn),
        in_specs=[pl.BlockSpec((8, d), lambda l, j: (0, 0)),
                  pl.BlockSpec((None, d, tn), lambda l, j: (l, 0, j)),
                  pl.BlockSpec((None, 1, tn), lambda l, j: (l, 0, j))],
        out_specs=pl.BlockSpec((None, 8, tn), lambda l, j: (l, 0, j)),
        out_shape=jax.ShapeDtypeStruct((depth, 8, n), F32),
        compiler_params=_cparams(("arbitrary", "arbitrary")),
        name="ada_mod",
    )(c8, w_ada, b_ada.reshape(depth, 1, n))


class _Rows:
    def __init__(self, batch, seq, ctx_len, n_rows, tm):
        self.batch, self.seq, self.ctx_len, self.n_rows, self.tm = batch, seq, ctx_len, n_rows, tm
        self.n_tiles = n_rows // tm
        self.lat_tiles = batch * seq // tm
        self.tiles_per_batch = seq // tm

    def mod_row(self, i):
        return jnp.where(i < self.lat_tiles, i // self.tiles_per_batch, self.batch)


def _pick_tm(seq, ctx_rows, cap):
    for tm in (1024, 512, 256, 128):
        if tm <= cap and seq % tm == 0 and ctx_rows % tm == 0:
            return tm
    raise ValueError("unsupported sequence / context lengths")


def _mod_spec(rows, chunk, d):
    return pl.BlockSpec((None, 1, d), lambda i, *_: (rows.mod_row(i) * 6 + chunk, 0, 0))


def _lnmod_kernel(x_ref, xc_ref, sh_ref, sc_ref, h_ref, *, lat_tiles):
    y = _ln_rows(_lat_or_ctx(x_ref, xc_ref, slice(None), lat_tiles))
    h_ref[...] = (y * (1.0 + sc_ref[...]) + sh_ref[...]).astype(h_ref.dtype)


def _lnmod(x_lat, x_ctx, mod, rows, chunk0):
    d = x_lat.shape[1]
    tm = rows.tm
    return pl.pallas_call(
        functools.partial(_lnmod_kernel, lat_tiles=rows.lat_tiles),
        grid=(rows.n_tiles,),
        in_specs=_dual_specs(rows, d) + [_mod_spec(rows, chunk0, d), _mod_spec(rows, chunk0 + 1, d)],
        out_specs=pl.BlockSpec((tm, d), lambda i: (i, 0)),
        out_shape=jax.ShapeDtypeStruct((rows.n_rows, d), BF16),
        compiler_params=_cparams(("arbitrary",)),
        name="ln_modulate",
    )(x_lat, x_ctx, mod, mod)


def _mm_raw_kernel(h_ref, w_ref, o_ref):
    o_ref[...] = _dot(h_ref[...], w_ref[...]).astype(o_ref.dtype)


def _col_group_dots(h, w_ref, width):
    n = w_ref.shape[1]
    groups = [slice(c0, min(c0 + width, n)) for c0 in range(0, n, width)]
    return groups, [_dot(h, w_ref[:, g]) for g in groups]


def _mm_gate_kernel(h_ref, w_ref, loglb_ref, log1m_ref, onem_ref, lf_ref, k_ref):
    groups, zs = _col_group_dots(h_ref[...], w_ref, 2 * LANES)
    for g, z in zip(groups, zs):
        e = jnp.exp(-jnp.abs(z))
        r = 1.0 / (1.0 + e)
        log_sig = jnp.minimum(z, 0.0) + jnp.log(r)
        sig_neg = jnp.where(z >= 0.0, e * r, r)
        a = loglb_ref[:, g]
        b = log1m_ref[:, g] + log_sig
        lf_ref[:, g] = jnp.maximum(a, b) + jnp.log(1.0 + jnp.exp(-jnp.abs(a - b)))
        k_ref[:, g] = onem_ref[:, g] * sig_neg


def _mm_att_kernel(h_ref, w_ref, cs_ref, ss_ref, cm_ref, sm_ref, o_ref):
    groups, ps = _col_group_dots(h_ref[...], w_ref, 2 * LANES)
    cs, ss = cs_ref[...], ss_ref[...]
    swa_scale = SWA_HD ** -0.5
    for g, p in zip(groups, ps):
        for c0 in range(g.start, g.stop, LANES):
            x = p[:, c0 - g.start:c0 - g.start + LANES]
            if c0 < ATT_SV:
                x = x * cs + _rope_partner(x, SWA_HD // 4) * ss
                if c0 < ATT_SK:
                    x = x * swa_scale
            elif c0 == ATT_KR:
                x = x * cm_ref[...] + _rope_partner(x, MLA_ROPE // 4) * sm_ref[...]
            o_ref[:, c0:c0 + LANES] = x.astype(o_ref.dtype)


def _row_call(kernel, rows, row_inputs, const_inputs, out_widths, out_dtypes, name):
    tm = rows.tm
    in_specs = [pl.BlockSpec((tm, a.shape[1]), lambda i: (i, 0)) for a in row_inputs]
    in_specs += [pl.BlockSpec(a.shape, lambda i, nd=a.ndim: (0,) * nd) for a in const_inputs]
    out_specs = [pl.BlockSpec((tm, w), lambda i: (i, 0)) for w in out_widths]
    out_shape = [jax.ShapeDtypeStruct((rows.n_rows, w), dt) for w, dt in zip(out_widths, out_dtypes)]
    return pl.pallas_call(
        kernel, grid=(rows.n_tiles,), in_specs=in_specs, out_specs=out_specs, out_shape=out_shape,
        compiler_params=_cparams(("arbitrary",)), name=name,
    )(*row_inputs, *const_inputs)


def _hg_scan_kernel(*refs, reverse, n_chunks, readout):
    if readout:
        q_ref, k_ref, lf_ref, v_ref, of_ref, g_ref, ng_ref, o_ref, st_ref = refs
    else:
        q_ref, k_ref, lf_ref, v_ref, o_ref, st_ref = refs

    @pl.when(pl.program_id(1) == 0)
    def _():
        st_ref[...] = jnp.zeros_like(st_ref)

    c, s = HG_CHUNK, HG_SUB
    nsub = c // s
    chunks = [n_chunks - 1 - cc if reverse else cc for cc in range(n_chunks)]
    items = [(ch, hd) for ch in chunks for hd in range(HG_HEADS)]
    rows = lambda ch: slice(ch * c, (ch + 1) * c)
    cols = lambda hd: slice(hd * HG_DK, (hd + 1) * HG_DK)

    ri = lax.broadcasted_iota(jnp.int32, (c, c), 0)
    ci = lax.broadcasted_iota(jnp.int32, (c, c), 1)
    tri = jnp.where((ci >= ri) if reverse else (ci <= ri), 1.0, 0.0).astype(BF16)
    b_all = {}
    for ch in chunks:
        lf = lf_ref[rows(ch), :]
        hi = lf.astype(BF16)
        r1 = lf - hi.astype(F32)
        mid = r1.astype(BF16)
        lo = (r1 - mid.astype(F32)).astype(BF16)
        b_all[ch] = _dot(tri, hi) + _dot(tri, mid) + _dot(tri, lo)

    qe, k_dec, decay, v16, qt, kt = {}, {}, {}, {}, {}, {}
    for it in items:
        ch, hd = it
        b = b_all[ch][:, cols(hd)]
        q, k = q_ref[rows(ch), cols(hd)], k_ref[rows(ch), cols(hd)]
        btot = b[0:1, :] if reverse else b[c - 1:c, :]
        qe[it] = (q * jnp.exp(b)).astype(BF16)
        k_dec[it] = (k * jnp.exp(btot - b)).astype(BF16)
        decay[it] = jnp.exp(btot)
        v16[it] = v_ref[rows(ch), cols(hd)].astype(BF16)
        for i in range(nsub):
            r0 = i * s
            if reverse:
                k0, k1 = r0, c
                ref = b[r0 + s:r0 + s + 1, :] if i < nsub - 1 else jnp.zeros_like(btot)
            else:
                k0, k1 = 0, r0 + s
                ref = b[r0 - 1:r0, :] if i > 0 else jnp.zeros_like(btot)
            qt[it, i] = (q[r0:r0 + s] * jnp.exp(b[r0:r0 + s] - ref)).astype(BF16)
            kt[it, i] = (k[k0:k1] * jnp.exp(ref - b[k0:k1])).astype(BF16)

    upd = {it: _dot_tn(v16[it], k_dec[it]) for it in items}
    att = {}
    for it in items:
        for i in range(nsub):
            r0 = i * s
            k0 = r0 if reverse else 0
            a = _dot_nt(qt[it, i], kt[it, i])
            rr = lax.broadcasted_iota(jnp.int32, a.shape, 0) + r0
            cc = lax.broadcasted_iota(jnp.int32, a.shape, 1) + k0
            att[it, i] = jnp.where((cc >= rr) if reverse else (cc <= rr), a, 0.0).astype(BF16)

    states = [st_ref[hd] for hd in range(HG_HEADS)]
    o_inter = {}
    for it in items:
        ch, hd = it
        o_inter[it] = _dot_nt(qe[it], states[hd].astype(BF16))
        states[hd] = states[hd] * decay[it] + upd[it]
    for hd in range(HG_HEADS):
        st_ref[hd] = states[hd]

    for it in items:
        ch, hd = it
        outs = []
        for i in range(nsub):
            r0 = i * s
            k0, k1 = (r0, c) if reverse else (0, r0 + s)
            outs.append(o_inter[it][r0:r0 + s] + _dot(att[it, i], v16[it][k0:k1]))
        o = jnp.concatenate(outs, axis=0)
        if readout:
            o = o + of_ref[rows(ch), cols(hd)]
            o = o * lax.rsqrt(jnp.mean(o * o, axis=-1, keepdims=True) + RMS_EPS) * ng_ref[...]
            o = o * _silu(g_ref[rows(ch), cols(hd)])
        o_ref[rows(ch), cols(hd)] = o.astype(o_ref.dtype)


def _hg_scan(p_raw, kk, lf, batch, seq, ctx_len, reverse, readout_args=None):
    n_rows = p_raw.shape[0]
    tb = ctx_len
    nl = seq // tb
    ctx0 = batch * seq // tb
    direction = 1 if reverse else 0

    def row_block(b, i):
        lat = b * nl + (nl - i if reverse else i - 1)
        return jnp.where(i == 0, ctx0 + b, lat)

    def col(j):
        return pl.BlockSpec((tb, HG_W), lambda b, i: (row_block(b, i), j))

    spec = col(0)
    readout = readout_args is not None
    inputs = [p_raw, kk, lf, p_raw]
    in_specs = [col(0), col(direction), col(direction), col(1)]
    if readout:
        o_f, norm_g = readout_args
        inputs += [o_f, p_raw, norm_g]
        in_specs += [spec, col(2), pl.BlockSpec((1, HG_DK), lambda b, i: (0, 0))]
    kern = functools.partial(_hg_scan_kernel, reverse=reverse, n_chunks=tb // HG_CHUNK, readout=readout)
    return pl.pallas_call(
        kern, grid=(batch, nl + 1), in_specs=in_specs, out_specs=spec,
        out_shape=jax.ShapeDtypeStruct((n_rows, HG_W), BF16 if readout else F32),
        scratch_shapes=[pltpu.VMEM((HG_HEADS, HG_DK, HG_DK), F32)],
        compiler_params=_cparams(("arbitrary", "arbitrary")),
        name="hgrn2_scan_bwd_readout" if readout else "hgrn2_scan_fwd",
    )(*inputs)


def _mla_proj_kernel(cq_ref, ckv_ref, kr_ref, qn_ref, kvn_ref, wq_ref, wkv_ref, cm_ref, sm_ref,
                     q_ref, k_ref, v_ref):
    def rms(x, g):
        xf = x.astype(F32)
        return (xf * lax.rsqrt(jnp.mean(xf * xf, axis=-1, keepdims=True) + RMS_EPS) * g).astype(BF16)

    scale = (MLA_NOPE + MLA_ROPE) ** -0.5 * math.log2(math.e)
    _, qs = _col_group_dots(rms(cq_ref[...], qn_ref[...]), wq_ref, MLA_QK_PAD)
    _, kv2 = _col_group_dots(rms(ckv_ref[...], kvn_ref[...]), wkv_ref, 2 * LANES)
    kvs = [g[:, half * LANES:(half + 1) * LANES] for g in kv2 for half in range(2)]
    cm, sm = cm_ref[...], sm_ref[...]
    kr = kr_ref[...]
    lane = lax.broadcasted_iota(jnp.int32, (kr.shape[0], LANES), 1)
    ones_col = jnp.where(lane == 0, 1.0, 0.0).astype(v_ref.dtype)
    for hd in range(MLA_HEADS):
        c0 = hd * MLA_QK_PAD
        q_ref[:, c0:c0 + LANES] = (qs[hd][:, 0:LANES] * scale).astype(q_ref.dtype)
        x = qs[hd][:, LANES:2 * LANES]
        y = (x * cm + _rope_partner(x, MLA_ROPE // 4) * sm) * scale
        q_ref[:, c0 + LANES:c0 + 2 * LANES] = y.astype(q_ref.dtype)
        k_ref[:, c0:c0 + LANES] = kvs[hd].astype(k_ref.dtype)
        k_ref[:, c0 + LANES:c0 + 2 * LANES] = kr
        v0 = hd * MLA_V_PAD
        v_ref[:, v0:v0 + MLA_V] = kvs[MLA_HEADS + hd].astype(v_ref.dtype)
        v_ref[:, v0 + MLA_V:v0 + MLA_V_PAD] = ones_col


def _mla_proj(p_att, q_norm, kv_norm, wq, wkv, cm, sm, rows):
    tm = rows.tm
    n = rows.n_rows
    hq = MLA_HEADS * MLA_QK_PAD

    def col(width, off):
        return pl.BlockSpec((tm, width), lambda i: (i, off // width))

    def whole(a):
        return pl.BlockSpec(a.shape, lambda i: (0, 0))

    return pl.pallas_call(
        _mla_proj_kernel, grid=(rows.n_tiles,),
        in_specs=[col(MLA_Q_LORA, ATT_CQ), col(MLA_KV_LORA, ATT_CKV), col(LANES, ATT_KR),
                  whole(q_norm), whole(kv_norm), whole(wq), whole(wkv),
                  pl.BlockSpec((tm, LANES), lambda i: (i, 0)), pl.BlockSpec((tm, LANES), lambda i: (i, 0))],
        out_specs=[pl.BlockSpec((tm, hq), lambda i: (i, 0)), pl.BlockSpec((tm, hq), lambda i: (i, 0)),
                   pl.BlockSpec((tm, MLA_HEADS * MLA_V_PAD), lambda i: (i, 0))],
        out_shape=[jax.ShapeDtypeStruct((n, hq), BF16), jax.ShapeDtypeStruct((n, hq), BF16),
                   jax.ShapeDtypeStruct((n, MLA_HEADS * MLA_V_PAD), BF16)],
        compiler_params=_cparams(("arbitrary",)), name="mla_proj",
    )(p_att, p_att, p_att, q_norm, kv_norm, wq, wkv, cm, sm)


def _lane_tile_fold(x, op):
    out = x[:, 0:LANES]
    for t in range(1, x.shape[1] // LANES):
        out = op(out, x[:, t * LANES:(t + 1) * LANES])
    return out


def _mla_attn_kernel(*refs, ck, with_lat):
    if with_lat:
        q_ref, kl_ref, vl_ref, kc_ref, vc_ref, o_ref = refs
    else:
        q_ref, kc_ref, vc_ref, o_ref = refs
    chunks = [(kc_ref, vc_ref, 0, kc_ref.shape[0])]
    if with_lat:
        chunks += [(kl_ref, vl_ref, c * ck, ck) for c in range(kl_ref.shape[0] // ck)]
    m = [None] * MLA_HEADS_PER_STEP
    acc = [None] * MLA_HEADS_PER_STEP
    for k_ref, v_ref, r0, n in chunks:
        for hd in range(MLA_HEADS_PER_STEP):
            qk = slice(hd * MLA_QK_PAD, (hd + 1) * MLA_QK_PAD)
            s = _dot_nt(q_ref[:, qk], k_ref[r0:r0 + n, qk])
            m_c = jnp.max(_lane_tile_fold(s, jnp.maximum), axis=-1, keepdims=True)
            m_new = m_c if m[hd] is None else jnp.maximum(m[hd], m_c)
            pv = _dot(jnp.exp2(s - m_new).astype(BF16), v_ref[r0:r0 + n, hd * MLA_V_PAD:(hd + 1) * MLA_V_PAD])
            acc[hd] = pv if m[hd] is None else acc[hd] * jnp.exp2(m[hd] - m_new) + pv
            m[hd] = m_new
    for hd in range(MLA_HEADS_PER_STEP):
        o = acc[hd][:, 0:MLA_V] / acc[hd][:, MLA_V:MLA_V + 1]
        o_ref[:, hd * MLA_V:(hd + 1) * MLA_V] = o.astype(o_ref.dtype)


def _mla_attn(q, k, v, batch, seq, ctx_len, latent):
    ctx0 = batch * seq // ctx_len
    ck = 1024 if seq % 1024 == 0 else ctx_len
    hps = MLA_HEADS_PER_STEP
    if latent:
        tq = MLA_TQ if seq % MLA_TQ == 0 else ctx_len
        nq = seq // tq
        q_row = lambda b, h, i: b * nq + i
        o_row, n_rows = q_row, batch * seq
    else:
        tq, nq = ctx_len, 1
        q_row = lambda b, h, i: ctx0 + b
        o_row, n_rows = (lambda b, h, i: b), batch * ctx_len
    in_specs = [pl.BlockSpec((tq, hps * MLA_QK_PAD), lambda b, h, i: (q_row(b, h, i), h))]
    inputs = [q]
    if latent:
        in_specs += [pl.BlockSpec((seq, hps * MLA_QK_PAD), lambda b, h, i: (b, h)),
                     pl.BlockSpec((seq, hps * MLA_V_PAD), lambda b, h, i: (b, h))]
        inputs += [k, v]
    in_specs += [pl.BlockSpec((ctx_len, hps * MLA_QK_PAD), lambda b, h, i: (ctx0 + b, h)),
                 pl.BlockSpec((ctx_len, hps * MLA_V_PAD), lambda b, h, i: (ctx0 + b, h))]
    inputs += [k, v]
    return pl.pallas_call(
        functools.partial(_mla_attn_kernel, ck=ck, with_lat=latent),
        grid=(batch, MLA_HEADS // hps, nq), in_specs=in_specs,
        out_specs=pl.BlockSpec((tq, hps * MLA_V), lambda b, h, i: (o_row(b, h, i), h)),
        out_shape=jax.ShapeDtypeStruct((n_rows, MLA_HEADS * MLA_V), BF16),
        compiler_params=_cparams(("arbitrary", "arbitrary", "arbitrary")),
        name="mla_attn_lat" if latent else "mla_attn_ctx",
    )(*inputs)


def _swa_kernel(sink_ref, q_ref, kp_ref, kn_ref, kx_ref, kc_ref, vp_ref, vn_ref, vx_ref, vc_ref, o_ref,
                *, nb, seq, ctx_queries):
    n = pl.program_id(1)
    g = SWA_HEADS // SWA_KV_HEADS
    blk = SWA_BLOCK

    def attend(band):
        if band:
            a = lax.broadcasted_iota(jnp.int32, (g * blk, 3 * blk), 0) & (blk - 1)
            j = lax.broadcasted_iota(jnp.int32, (g * blk, 3 * blk), 1)
            key_pos = (n - 1) * blk + j
            keep = (jnp.abs(j - blk - a) <= SWA_WINDOW) & (key_pos >= 0) & (key_pos < seq)
        heads = range(SWA_KV_HEADS)
        ks = [slice(kh * SWA_HD, (kh + 1) * SWA_HD) for kh in heads]
        row = lax.broadcasted_iota(jnp.int32, (g * blk, 1), 0)
        qs, sink, s_c, s_b = [], [], [], []
        for kh in heads:
            qs.append(jnp.concatenate(
                [q_ref[:, (kh * g + gi) * SWA_HD:(kh * g + gi + 1) * SWA_HD] for gi in range(g)], axis=0))
            sk = jnp.zeros((g * blk, 1), F32)
            for gi in range(g):
                sk = jnp.where((row >= gi * blk) & (row < (gi + 1) * blk), sink_ref[kh * g + gi], sk)
            sink.append(sk)
            s_c.append(_dot_nt(qs[kh], kc_ref[:, ks[kh]]))
            if band:
                kb = jnp.concatenate([kp_ref[:, ks[kh]], kn_ref[:, ks[kh]], kx_ref[:, ks[kh]]], axis=0)
                s_b.append(jnp.where(keep, _dot_nt(qs[kh], kb), -jnp.inf))
        p_c, p_b, den = [], [], []
        for kh in heads:
            m = jnp.maximum(jnp.max(s_c[kh], axis=-1, keepdims=True), sink[kh])
            if band:
                m = jnp.maximum(m, jnp.max(s_b[kh], axis=-1, keepdims=True))
            pc = jnp.exp(s_c[kh] - m)
            dn = jnp.sum(pc, axis=-1, keepdims=True) + jnp.exp(sink[kh] - m)
            if band:
                pb = jnp.exp(s_b[kh] - m)
                dn = dn + jnp.sum(pb, axis=-1, keepdims=True)
                p_b.append(pb.astype(BF16))
            p_c.append(pc.astype(BF16))
            den.append(dn)
        for kh in heads:
            acc = _dot(p_c[kh], vc_ref[:, ks[kh]])
            if band:
                vb = jnp.concatenate([vp_ref[:, ks[kh]], vn_ref[:, ks[kh]], vx_ref[:, ks[kh]]], axis=0)
                acc = acc + _dot(p_b[kh], vb)
            o = acc / den[kh]
            for gi in range(g):
                c0 = (kh * g + gi) * SWA_HD
                o_ref[:, c0:c0 + SWA_HD] = o[gi * blk:(gi + 1) * blk].astype(o_ref.dtype)

    if ctx_queries:
        pl.when(n < nb)(lambda: attend(True))
        pl.when(n >= nb)(lambda: attend(False))
    else:
        attend(True)


def _swa(p_att, sink, batch, seq, ctx_len, ctx_queries):
    blk = SWA_BLOCK
    nb, ncb = seq // blk, ctx_len // blk
    ctx0 = batch * seq // ctx_len
    kcol, vcol = ATT_SK // SWA_KV_W, ATT_SV // SWA_KV_W
    q_row = lambda b, n: jnp.where(n < nb, b * nb + n, batch * nb + b * ncb + (n - nb))
    n_rows = batch * seq + (batch * ctx_len if ctx_queries else 0)

    def nbr(col, d):
        return pl.BlockSpec((blk, SWA_KV_W), lambda b, n: (b * nb + jnp.clip(n + d, 0, nb - 1), col))

    ctx_k = pl.BlockSpec((ctx_len, SWA_KV_W), lambda b, n: (ctx0 + b, kcol))
    ctx_v = pl.BlockSpec((ctx_len, SWA_KV_W), lambda b, n: (ctx0 + b, vcol))
    return pl.pallas_call(
        functools.partial(_swa_kernel, nb=nb, seq=seq, ctx_queries=ctx_queries),
        grid=(batch, nb + (ncb if ctx_queries else 0)),
        in_specs=[pl.BlockSpec(memory_space=pltpu.SMEM),
                  pl.BlockSpec((blk, SWA_W), lambda b, n: (q_row(b, n), 0)),
                  nbr(kcol, -1), nbr(kcol, 0), nbr(kcol, 1), ctx_k, nbr(vcol, -1), nbr(vcol, 0), nbr(vcol, 1), ctx_v],
        out_specs=pl.BlockSpec((blk, SWA_W), lambda b, n: (q_row(b, n), 0)),
        out_shape=jax.ShapeDtypeStruct((n_rows, SWA_W), BF16),
        compiler_params=_cparams(("arbitrary", "arbitrary")),
        name="swa_attn",
    )(sink, *([p_att] * 9))


def _route(logits, rb):
    aff = _sigmoid(logits)
    sel = aff + rb
    tm = logits.shape[1]
    scores = []
    for g in range(N_GROUPS):
        r = [sel[g * E_PER_GROUP + j:g * E_PER_GROUP + j + 1] for j in range(E_PER_GROUP)]
        best = None
        for i in range(E_PER_GROUP):
            for j in range(i + 1, E_PER_GROUP):
                pair = r[i] + r[j]
                best = pair if best is None else jnp.maximum(best, pair)
        scores.append(best)
    gbest, gi = scores[0], jnp.zeros((1, tm), jnp.int32)
    for g in range(1, N_GROUPS):
        upd = scores[g] > gbest
        gbest = jnp.where(upd, scores[g], gbest)
        gi = jnp.where(upd, g, gi)
    s_in, a_in = [], []
    for j in range(E_PER_GROUP):
        sj, aj = sel[j:j + 1], aff[j:j + 1]
        for g in range(1, N_GROUPS):
            e = g * E_PER_GROUP + j
            sj = jnp.where(gi == g, sel[e:e + 1], sj)
            aj = jnp.where(gi == g, aff[e:e + 1], aj)
        s_in.append(sj)
        a_in.append(aj)
    chosen = []
    for j in range(E_PER_GROUP):
        rank = jnp.zeros((1, tm), jnp.int32)
        for k in range(E_PER_GROUP):
            if k == j:
                continue
            ahead = (s_in[k] >= s_in[j]) if k < j else (s_in[k] > s_in[j])
            rank = rank + ahead.astype(jnp.int32)
        chosen.append(rank < 2)
    w = [jnp.where(chosen[j], a_in[j], 0.0) for j in range(E_PER_GROUP)]
    wsum = w[0] + w[1] + w[2] + w[3]
    gate_in = [wj / wsum * ROUTE_SCALE for wj in w]
    code = sum(jnp.where(chosen[j], 1 << j, 0) for j in range(E_PER_GROUP))
    pair = jnp.zeros((1, tm), jnp.int32)
    for idx, cval in enumerate((3, 5, 9, 6, 10, 12)):
        pair = jnp.where(code == cval, idx, pair)
    bucket = gi * N_PAIRS + pair
    g_lo = jnp.zeros((1, tm), F32)
    g_hi = jnp.zeros((1, tm), F32)
    seen = jnp.zeros((1, tm), jnp.bool_)
    for j in range(E_PER_GROUP):
        g_lo = jnp.where(chosen[j] & ~seen, gate_in[j], g_lo)
        g_hi = jnp.where(chosen[j] & seen, gate_in[j], g_hi)
        seen = seen | chosen[j]
    return g_lo, g_hi, bucket


def _lat_or_ctx(lat_ref, ctx_ref, rs, lat_tiles):
    if ctx_ref is None:
        return lat_ref[rs, :]
    return jnp.where(pl.program_id(0) < lat_tiles, lat_ref[rs, :], ctx_ref[rs, :])


def _dual_specs(rows, width):
    return [pl.BlockSpec((rows.tm, width), lambda i, *_: (jnp.minimum(i, rows.lat_tiles - 1), 0)),
            pl.BlockSpec((rows.tm, width), lambda i, *_: (jnp.maximum(i - rows.lat_tiles, 0), 0),
                         pipeline_mode=pl.Buffered(1))]


def _out_kernel(*refs, alpha, dual, lat_tiles):
    if dual:
        (ohg_ref, omla_ref, omlac_ref, oswa_ref, x_ref, xc_ref, g1_ref, sh2_ref, sc2_ref, lng_ref, lnb_ref,
         w_ref, rwt_ref, rb_ref, x1_ref, h2_ref, bucket_ref) = refs
    else:
        (ohg_ref, omla_ref, oswa_ref, x_ref, g1_ref, sh2_ref, sc2_ref, lng_ref, lnb_ref,
         w_ref, rwt_ref, rb_ref, x1_ref, h2_ref, bucket_ref) = refs
        omlac_ref = xc_ref = None
    o1, o2 = HG_W, HG_W + MLA_HEADS * MLA_V
    d = x_ref.shape[1]
    tm = x_ref.shape[0]
    halves = [slice(0, tm // 2), slice(tm // 2, tm)]
    mixes = [_dot(ohg_ref[rs, :], w_ref[0:o1, :])
             + _dot(_lat_or_ctx(omla_ref, omlac_ref, rs, lat_tiles), w_ref[o1:o2, :])
             + _dot(oswa_ref[rs, :], w_ref[o2:, :]) for rs in halves]
    for rs, mix in zip(halves, mixes):
        x_in = _lat_or_ctx(x_ref, xc_ref, rs, lat_tiles)
        x1 = _ln_rows(alpha * x_in + g1_ref[...] * mix) * lng_ref[...] + lnb_ref[...]
        x1_ref[rs, :] = x1
        h2 = (_ln_rows(x1) * (1.0 + sc2_ref[...]) + sh2_ref[...]).astype(BF16)
        h2_ref[rs, 0:d] = h2.astype(h2_ref.dtype)
        g_lo, g_hi, bucket = _route(_dot_nt(rwt_ref[...], h2), rb_ref[...])
        bucket_ref[:, rs] = bucket
        gate_rows = jnp.concatenate([g_lo, g_hi, jnp.zeros((LANES - 2, tm // 2), F32)], axis=0)
        h2_ref[rs, d:d + LANES] = gate_rows.T


def _out_proj(o_hg, o_mla, o_swa, x, mod, ln_g, ln_b, w_out, rwt, rb, rows, alpha):
    tm = rows.tm
    dual = isinstance(x, tuple)
    d = (x[0] if dual else x).shape[1]
    n = rows.n_rows
    row = lambda w: pl.BlockSpec((tm, w), lambda i: (i, 0))
    whole = lambda a: pl.BlockSpec(a.shape, lambda i: (0, 0), pipeline_mode=pl.Buffered(1))
    mla_w = MLA_HEADS * MLA_V
    if dual:
        in_specs = [row(HG_W)] + _dual_specs(rows, mla_w) + [row(SWA_W)] + _dual_specs(rows, d)
        inputs = [o_hg, *o_mla, o_swa, *x]
    else:
        in_specs = [row(HG_W), row(mla_w), row(SWA_W), row(d)]
        inputs = [o_hg, o_mla, o_swa, x]
    in_specs += [_mod_spec(rows, 2, d), _mod_spec(rows, 3, d), _mod_spec(rows, 4, d),
                 whole(ln_g), whole(ln_b), whole(w_out), whole(rwt), whole(rb)]
    inputs += [mod, mod, mod, ln_g, ln_b, w_out, rwt, rb]
    return pl.pallas_call(
        functools.partial(_out_kernel, alpha=alpha, dual=dual, lat_tiles=rows.lat_tiles), grid=(rows.n_tiles,),
        in_specs=in_specs,
        out_specs=[row(d), row(d + LANES), pl.BlockSpec((1, tm), lambda i: (0, i))],
        out_shape=[jax.ShapeDtypeStruct((n, d), F32), jax.ShapeDtypeStruct((n, d + LANES), F32),
                   jax.ShapeDtypeStruct((1, n), jnp.int32)],
        compiler_params=_cparams(("arbitrary",)), name="out_proj_ln_router",
    )(*inputs)


def _row_gather_start(idx_ref, base, src_hbm, dst, sem, n, static_rows=False):
    def issue(r, carry):
        pltpu.make_async_copy(src_hbm.at[pl.ds(idx_ref[base + r], 1)], dst.at[pl.ds(r, 1)], sem).start()
        return carry
    if static_rows:
        for r in range(n):
            issue(r, 0)
    else:
        lax.fori_loop(0, n, issue, 0, unroll=8)


def _row_gather_start_next(t, n_valid, idx_ref, base, src_hbm, buf, sem, n):
    for half in range(2):
        @pl.when((t < n_valid) & (t % 2 == half))
        def _():
            _row_gather_start(idx_ref, base, src_hbm, buf.at[half], sem.at[half], n, static_rows=True)


def _row_gather_wait(src_hbm, dst, sem, n):
    pltpu.make_async_copy(src_hbm.at[pl.ds(0, n)], dst, sem).wait()


def _moe_kernel(stok_ref, off_ref, e1_ref, e2_ref, nused_ref, h_hbm, wg1_ref, wu1_ref, wd1_ref,
                wg2_ref, wu2_ref, wd2_ref, y_ref, hbuf, sem, *, tm):
    i = pl.program_id(0)
    n_used = nused_ref[0]
    d = y_ref.shape[1]

    @pl.when(i == 0)
    def _():
        _row_gather_start(stok_ref, off_ref[0], h_hbm, hbuf.at[0], sem.at[0], tm)

    nxt = jnp.minimum(i + 1, pl.num_programs(0) - 1)
    _row_gather_start_next(i + 1, n_used, stok_ref, off_ref[nxt], h_hbm, hbuf, sem, tm)

    @pl.when(i < n_used)
    def _():
        slot = i % 2
        _row_gather_wait(h_hbm, hbuf.at[slot], sem.at[slot], tm)
        h = hbuf[slot, :, 0:d].astype(BF16)
        g_lo = hbuf[slot, :, d:d + 1]
        g_hi = hbuf[slot, :, d + 1:d + 2]

        act1 = (_silu(_dot(h, wg1_ref[...])) * _dot(h, wu1_ref[...])).astype(BF16)
        act2 = (_silu(_dot(h, wg2_ref[...])) * _dot(h, wu2_ref[...])).astype(BF16)
        y_ref[...] = g_lo * _dot(act1, wd1_ref[...]) + g_hi * _dot(act2, wd2_ref[...])

    @pl.when(i >= n_used)
    def _():
        y_ref[...] = jnp.zeros_like(y_ref)


def _moe(h2, stok, off, e1, e2, n_used, wg, wu, wd, tm):
    n_tiles = off.shape[0]
    dx = h2.shape[1]
    d, f = wg.shape[1], wg.shape[2]
    last = lambda i, nu: jnp.minimum(i, nu[0] - 1)
    wspec_in = lambda sel: pl.BlockSpec((None, d, f), lambda i, s, o, a, b, nu: ((a, b)[sel][last(i, nu)], 0, 0))
    wspec_dn = lambda sel: pl.BlockSpec((None, f, d), lambda i, s, o, a, b, nu: ((a, b)[sel][last(i, nu)], 0, 0))
    grid_spec = pltpu.PrefetchScalarGridSpec(
        num_scalar_prefetch=5, grid=(n_tiles,),
        in_specs=[pl.BlockSpec(memory_space=pl.ANY),
                  wspec_in(0), wspec_in(0), wspec_dn(0), wspec_in(1), wspec_in(1), wspec_dn(1)],
        out_specs=pl.BlockSpec((tm, d), lambda i, s, o, a, b, nu: (i, 0)),
        scratch_shapes=[pltpu.VMEM((2, tm, dx), F32), pltpu.SemaphoreType.DMA((2,))])
    return pl.pallas_call(
        functools.partial(_moe_kernel, tm=tm), grid_spec=grid_spec,
        out_shape=jax.ShapeDtypeStruct((n_tiles * tm, d), F32),
        compiler_params=_cparams(("arbitrary",)), name="moe_grouped",
    )(stok, off, e1, e2, n_used, h2, wg, wu, wd, wg, wu, wd)


def _moe_plan(bucket, n_tokens, tm):
    tok = jnp.arange(n_tokens, dtype=jnp.int32)
    buckets = jnp.arange(N_BUCKETS, dtype=jnp.int32)
    skey, stok = lax.sort((bucket * n_tokens + tok, tok), num_keys=1)
    counts = jnp.sum((bucket[:, None] == buckets[None, :]).astype(jnp.int32), axis=0)
    padded = (counts + tm - 1) // tm * tm
    ends = jnp.cumsum(padded)
    shift = (ends - padded) - (jnp.cumsum(counts) - counts)
    sbucket = skey // n_tokens
    slot = tok + jnp.sum(jnp.where(sbucket[:, None] == buckets[None, :], shift[None, :], 0), axis=1)
    _, pos = lax.sort((stok, slot), num_keys=1)
    n_tiles = n_tokens // tm + N_BUCKETS
    tile_start = jnp.arange(n_tiles, dtype=jnp.int32) * tm
    tile_bucket = jnp.sum((ends[None, :] <= tile_start[:, None]).astype(jnp.int32), axis=1)
    tile_bucket = jnp.minimum(tile_bucket, N_BUCKETS - 1)
    tshift = jnp.sum(jnp.where(tile_bucket[:, None] == buckets[None, :], shift[None, :], 0), axis=1)
    off = jnp.clip(tile_start - tshift, 0, n_tokens)
    grp, pair = tile_bucket // N_PAIRS, tile_bucket % N_PAIRS
    lo = jnp.where(pair < 3, 0, jnp.where(pair < 5, 1, 2))
    hi = jnp.where(pair == 0, 1, jnp.where((pair == 1) | (pair == 3), 2, 3))
    e1, e2 = grp * E_PER_GROUP + lo, grp * E_PER_GROUP + hi
    n_used = (ends[-1] // tm).astype(jnp.int32).reshape(1)
    stok = jnp.concatenate([stok, jnp.zeros((tm,), jnp.int32)])
    return stok, off.astype(jnp.int32), pos.astype(jnp.int32), e1.astype(jnp.int32), e2.astype(jnp.int32), n_used


def _ln2_kernel(pos_ref, y_hbm, x1_ref, g2_ref, lng_ref, lnb_ref, *rest, alpha, tm, emit_h):
    if emit_h:
        sh_ref, sc_ref, x2_ref, h_ref, ybuf, sem = rest
    else:
        x2_ref, ybuf, sem = rest
    i = pl.program_id(0)
    n_tiles = pl.num_programs(0)

    @pl.when(i == 0)
    def _():
        _row_gather_start(pos_ref, 0, y_hbm, ybuf.at[0], sem.at[0], tm)

    _row_gather_start_next(i + 1, n_tiles, pos_ref, (i + 1) * tm, y_hbm, ybuf, sem, tm)

    slot = i % 2
    _row_gather_wait(y_hbm, ybuf.at[slot], sem.at[slot], tm)
    x2 = _ln_rows(alpha * x1_ref[...] + g2_ref[...] * ybuf[slot]) * lng_ref[...] + lnb_ref[...]
    x2_ref[...] = x2
    if emit_h:
        h_ref[...] = (_ln_rows(x2) * (1.0 + sc_ref[...]) + sh_ref[...]).astype(h_ref.dtype)


def _ln2(pos, y_sorted, x1, mod, mod_next, ln_g, ln_b, rows, alpha):
    tm = rows.tm
    d = x1.shape[1]
    emit_h = mod_next is not None
    row = pl.BlockSpec((tm, d), lambda i, p: (i, 0))
    whole = lambda a: pl.BlockSpec(a.shape, lambda i, p: (0, 0))
    in_specs = [pl.BlockSpec(memory_space=pl.ANY), row, _mod_spec(rows, 5, d), whole(ln_g), whole(ln_b)]
    inputs = [y_sorted, x1, mod, ln_g, ln_b]
    out_specs = [row]
    out_shape = [jax.ShapeDtypeStruct((rows.n_rows, d), F32)]
    if emit_h:
        in_specs += [_mod_spec(rows, 0, d), _mod_spec(rows, 1, d)]
        inputs += [mod_next, mod_next]
        out_specs.append(row)
        out_shape.append(jax.ShapeDtypeStruct((rows.n_rows, d), BF16))
    grid_spec = pltpu.PrefetchScalarGridSpec(
        num_scalar_prefetch=1, grid=(rows.n_tiles,), in_specs=in_specs, out_specs=out_specs,
        scratch_shapes=[pltpu.VMEM((2, tm, d), F32), pltpu.SemaphoreType.DMA((2,))])
    out = pl.pallas_call(
        functools.partial(_ln2_kernel, alpha=alpha, tm=tm, emit_h=emit_h), grid_spec=grid_spec,
        out_shape=out_shape, compiler_params=_cparams(("arbitrary",)), name="unpermute_ln2",
    )(pos, *inputs)
    return out if emit_h else (out[0], None)


def _rope_tables(seq, batch, ctx_rows, dim, pad_to):
    rows = seq // GRID_W
    row = jnp.repeat(jnp.arange(rows, dtype=jnp.int32), GRID_W)
    col = jnp.tile(jnp.arange(GRID_W, dtype=jnp.int32), rows)
    nf = dim // 4
    inv_freq = ROPE_BASE ** (-jnp.arange(nf, dtype=F32) / nf)
    ang = jnp.stack([row, col], -1).astype(F32)[:, :, None] * inv_freq
    cos, sin = jnp.cos(ang), jnp.sin(ang)
    c = jnp.stack([cos, cos], axis=2).reshape(seq, dim)
    s = jnp.stack([-sin, sin], axis=2).reshape(seq, dim)
    if pad_to > dim:
        c = jnp.pad(c, ((0, 0), (0, pad_to - dim)))
        s = jnp.pad(s, ((0, 0), (0, pad_to - dim)))
    ctx_c = jnp.zeros((ctx_rows, pad_to), F32).at[:, :dim].set(1.0)
    c = jnp.concatenate([jnp.tile(c, (batch, 1)), ctx_c], axis=0)
    s = jnp.concatenate([jnp.tile(s, (batch, 1)), jnp.zeros((ctx_rows, pad_to), F32)], axis=0)
    return c, s


def _split_w_in(w):
    o1 = 5 * HG_W
    o2 = o1 + MLA_Q_LORA + MLA_KV_LORA + MLA_ROPE
    hg, mla, swa = w[:, :o1], w[:, o1:o2], w[:, o2:]
    w_raw = jnp.concatenate([hg[:, 0:HG_W], hg[:, 3 * HG_W:5 * HG_W]], axis=1)
    w_gate = hg[:, HG_W:3 * HG_W]
    kr = jnp.pad(mla[:, MLA_Q_LORA + MLA_KV_LORA:], ((0, 0), (0, LANES - MLA_ROPE)))
    w_att = jnp.concatenate([swa, mla[:, :MLA_Q_LORA + MLA_KV_LORA], kr], axis=1)
    return w_raw.astype(BF16), w_gate.astype(BF16), w_att.astype(BF16)


def _mla_weights(w_uq, w_ukv):
    qk = MLA_NOPE + MLA_ROPE
    wq = w_uq.reshape(MLA_Q_LORA, MLA_HEADS, qk)
    wq = jnp.pad(wq, ((0, 0), (0, 0), (0, MLA_QK_PAD - qk))).reshape(MLA_Q_LORA, MLA_HEADS * MLA_QK_PAD)
    wkv = w_ukv.reshape(MLA_KV_LORA, MLA_HEADS, MLA_NOPE + MLA_V)
    wkv = jnp.concatenate([wkv[:, :, :MLA_NOPE].reshape(MLA_KV_LORA, -1), wkv[:, :, MLA_NOPE:].reshape(MLA_KV_LORA, -1)], axis=1)
    return wq.astype(BF16), wkv.astype(BF16)


def kernel(x, c, ctx, c_ctx, w_ada, b_ada, w_in, w_out, hg_lb_logits, hg_norm_g, mla_q_norm, mla_kv_norm,
           mla_w_uq, mla_w_ukv, swa_sink, ln1_g, ln1_b, ln2_g, ln2_b, router_w, router_b,
           moe_w_gate, moe_w_up, moe_w_down):
    batch, seq, d = x.shape
    ctx_len = ctx.shape[1]
    depth = w_ada.shape[0]
    alpha = (2.0 * depth) ** 0.25
    n_lat, n_ctx = batch * seq, batch * ctx_len
    n_all = n_lat + n_ctx
    assert batch + 1 <= 8 and ctx_len % (4 * HG_CHUNK) == 0 and seq % ctx_len == 0 and seq % GRID_W == 0

    c8 = jnp.zeros((8, d), F32).at[:batch].set(c).at[batch].set(c_ctx)
    mod_all = _ada(c8, w_ada, b_ada).reshape(depth, 8 * 6, 1, d)

    lb = jnp.cumsum(jax.nn.softmax(hg_lb_logits.astype(F32), axis=0), axis=0)
    lb = (lb - lb[0:1]).reshape(depth, 1, 2 * HG_W)
    log_lb, log_1m, one_m = jnp.log(lb), jnp.log1p(-lb), 1.0 - lb

    cs, ss = _rope_tables(seq, batch, n_ctx, SWA_HD, SWA_HD)
    cm, sm = _rope_tables(seq, batch, n_ctx, MLA_ROPE, LANES)
    rwt = router_w.T.astype(BF16)
    rb = router_b.astype(F32).reshape(N_EXPERTS, 1)

    tm_all = _pick_tm(seq, n_ctx, 512)
    rows_all = _Rows(batch, seq, ctx_len, n_all, tm_all)
    rows_lat = _Rows(batch, seq, ctx_len, n_lat, tm_all)
    moe_tm = 256

    xa = (x.reshape(n_lat, d), ctx.reshape(n_ctx, d))
    h = _lnmod(xa[0], xa[1], mod_all[0], rows_all, 0)

    for layer in range(depth):
        need_ctx = layer < depth - 1
        mod = mod_all[layer]
        rows = rows_all if need_ctx else rows_lat
        w_raw, w_gate, w_att = _split_w_in(w_in[layer])
        wq, wkv = _mla_weights(mla_w_uq[layer], mla_w_ukv[layer])

        (p_raw,) = _row_call(_mm_raw_kernel, rows_all, [h], [w_raw], [3 * HG_W], [F32], "in_proj_hg_raw")
        lf, kk = _row_call(_mm_gate_kernel, rows_all, [h], [w_gate, log_lb[layer], log_1m[layer], one_m[layer]],
                           [2 * HG_W, 2 * HG_W], [F32, F32], "in_proj_hg_gates")
        tables = [cs, ss, cm, sm]
        p_att = pl.pallas_call(
            _mm_att_kernel, grid=(rows_all.n_tiles,),
            in_specs=[pl.BlockSpec((tm_all, d), lambda i: (i, 0)), pl.BlockSpec(w_att.shape, lambda i: (0, 0))]
                     + [pl.BlockSpec((tm_all, LANES), lambda i: (i, 0))] * 4,
            out_specs=pl.BlockSpec((tm_all, ATT_COLS), lambda i: (i, 0)),
            out_shape=jax.ShapeDtypeStruct((n_all, ATT_COLS), BF16),
            compiler_params=_cparams(("arbitrary",)), name="in_proj_att",
        )(h, w_att, *tables)

        o_f = _hg_scan(p_raw, kk, lf, batch, seq, ctx_len, False)
        o_hg = _hg_scan(p_raw, kk, lf, batch, seq, ctx_len, True,
                        (o_f, hg_norm_g[layer].astype(F32).reshape(1, HG_DK)))

        q_mla, k_mla, v_mla = _mla_proj(p_att, mla_q_norm[layer].astype(F32).reshape(1, -1),
                                        mla_kv_norm[layer].astype(F32).reshape(1, -1), wq, wkv, cm, sm, rows_all)
        n_out = rows.n_rows
        o_mla = _mla_attn(q_mla, k_mla, v_mla, batch, seq, ctx_len, True)
        o_swa = _swa(p_att, swa_sink[layer].astype(F32), batch, seq, ctx_len, need_ctx)
        dual = isinstance(xa, tuple)
        if need_ctx:
            o_mla = (o_mla, _mla_attn(q_mla, k_mla, v_mla, batch, seq, ctx_len, False))
            if not dual:
                xa = (xa[:n_lat], xa[n_lat:])
        elif dual:
            xa = xa[0]

        x1, h2, bucket = _out_proj(
            o_hg, o_mla, o_swa, xa, mod, ln1_g[layer].astype(F32).reshape(1, d), ln1_b[layer].astype(F32).reshape(1, d),
            w_out[layer].astype(BF16), rwt, rb, rows, alpha)

        stok, off, pos, e1, e2, n_used = _moe_plan(bucket[0], n_out, moe_tm)
        y_sorted = _moe(h2, stok, off, e1, e2, n_used, moe_w_gate[layer].astype(BF16),
                        moe_w_up[layer].astype(BF16), moe_w_down[layer].astype(BF16), moe_tm)
        xa, h = _ln2(pos, y_sorted, x1, mod, mod_all[layer + 1] if need_ctx else None,
                     ln2_g[layer].astype(F32).reshape(1, d), ln2_b[layer].astype(F32).reshape(1, d), rows, alpha)

    return xa[:n_lat].reshape(batch, seq, d)
```

```python
import functools
import math

import jax
import jax.numpy as jnp
from jax import lax
from jax.experimental import pallas as pl
from jax.experimental.pallas import tpu as pltpu

F32 = jnp.float32
BF16 = jnp.bfloat16

GRID_W = 64
HG_HEADS = 4
HG_DK = 128
HG_W = HG_HEADS * HG_DK
HG_CHUNK = 64
HG_SUB = 16
MLA_HEADS = 8
MLA_Q_LORA = 512
MLA_KV_LORA = 256
MLA_NOPE = 128
MLA_ROPE = 64
MLA_V = 128
MLA_QK_PAD = 256
MLA_V_PAD = 256
MLA_HEADS_PER_STEP = 4
MLA_TQ = 512
SWA_HEADS = 4
SWA_KV_HEADS = 2
SWA_HD = 128
SWA_W = SWA_HEADS * SWA_HD
SWA_KV_W = SWA_KV_HEADS * SWA_HD
SWA_WINDOW = 128
SWA_BLOCK = 128
N_EXPERTS = 16
N_GROUPS = 4
E_PER_GROUP = 4
N_PAIRS = 6
N_BUCKETS = N_GROUPS * N_PAIRS
ROUTE_SCALE = 2.5
ROPE_BASE = 10000.0
LN_EPS = 1e-5
RMS_EPS = 1e-6
LANES = 128
VMEM_LIMIT = 56 * 1024 * 1024

ATT_SQ, ATT_SK, ATT_SV = 0, SWA_W, SWA_W + SWA_KV_W
ATT_CQ = SWA_W + 2 * SWA_KV_W
ATT_CKV = ATT_CQ + MLA_Q_LORA
ATT_KR = ATT_CKV + MLA_KV_LORA
ATT_COLS = ATT_KR + LANES


def _cparams(sem):
    return pltpu.CompilerParams(dimension_semantics=sem, vmem_limit_bytes=VMEM_LIMIT)


def _dot(a, b):
    return jnp.dot(a, b, preferred_element_type=F32)


def _dot_nt(a, b):
    return lax.dot_general(a, b, (((1,), (1,)), ((), ())), preferred_element_type=F32)


def _dot_tn(a, b):
    return lax.dot_general(a, b, (((0,), (0,)), ((), ())), preferred_element_type=F32)


def _sigmoid(x):
    return 1.0 / (1.0 + jnp.exp(-x))


def _silu(x):
    return x * _sigmoid(x)


def _ln_rows(x):
    mu = jnp.mean(x, axis=-1, keepdims=True)
    xc = x - mu
    var = jnp.mean(xc * xc, axis=-1, keepdims=True)
    return xc * lax.rsqrt(var + LN_EPS)


def _rope_partner(x, half):
    lane = lax.broadcasted_iota(jnp.int32, x.shape, x.ndim - 1)
    first = (lane & half) == 0
    n = x.shape[-1]
    return jnp.where(first, pltpu.roll(x, n - half, x.ndim - 1), pltpu.roll(x, half, x.ndim - 1))


def _ada_kernel(c_ref, w_ref, b_ref, o_ref):
    s = _silu(c_ref[...])
    o_ref[...] = _dot(s.astype(BF16), w_ref[...].astype(BF16)) + b_ref[...]


def _ada(c8, w_ada, b_ada):
    depth, d, n = w_ada.shape
    tn = 1024 if n % 1024 == 0 else n
    return pl.pallas_call(
        _ada_kernel,
        grid=(depth, n // tn),
        in_specs=[pl.BlockSpec((8, d), lambda l, j: (0, 0)),
                  pl.BlockSpec((None, d, tn), lambda l, j: (l, 0, j)),
                  pl.BlockSpec((None, 1, tn), lambda l, j: (l, 0, j))],
        out_specs=pl.BlockSpec((None, 8, tn), lambda l, j: (l, 0, j)),
        out_shape=jax.ShapeDtypeStruct((depth, 8, n), F32),
        compiler_params=_cparams(("arbitrary", "arbitrary")),
        name="ada_mod",
    )(c8, w_ada, b_ada.reshape(depth, 1, n))


class _Rows:
    def __init__(self, batch, seq, ctx_len, n_rows, tm):
        self.batch, self.seq, self.ctx_len, self.n_rows, self.tm = batch, seq, ctx_len, n_rows, tm
        self.n_tiles = n_rows // tm
        self.lat_tiles = batch * seq // tm
        self.tiles_per_batch = seq // tm

    def mod_row(self, i):
        return jnp.where(i < self.lat_tiles, i // self.tiles_per_batch, self.batch)


def _pick_tm(seq, ctx_rows, cap):
    for tm in (1024, 512, 256, 128):
        if tm <= cap and seq % tm == 0 and ctx_rows % tm == 0:
            return tm
    raise ValueError("unsupported sequence / context lengths")


def _mod_spec(rows, chunk, d):
    return pl.BlockSpec((None, 1, d), lambda i, *_: (rows.mod_row(i) * 6 + chunk, 0, 0))


def _lnmod_kernel(x_ref, xc_ref, sh_ref, sc_ref, h_ref, *, lat_tiles):
    y = _ln_rows(_lat_or_ctx(x_ref, xc_ref, slice(None), lat_tiles))
    h_ref[...] = (y * (1.0 + sc_ref[...]) + sh_ref[...]).astype(h_ref.dtype)


def _lnmod(x_lat, x_ctx, mod, rows, chunk0):
    d = x_lat.shape[1]
    tm = rows.tm
    return pl.pallas_call(
        functools.partial(_lnmod_kernel, lat_tiles=rows.lat_tiles),
        grid=(rows.n_tiles,),
        in_specs=_dual_specs(rows, d) + [_mod_spec(rows, chunk0, d), _mod_spec(rows, chunk0 + 1, d)],
        out_specs=pl.BlockSpec((tm, d), lambda i: (i, 0)),
        out_shape=jax.ShapeDtypeStruct((rows.n_rows, d), BF16),
        compiler_params=_cparams(("arbitrary",)),
        name="ln_modulate",
    )(x_lat, x_ctx, mod, mod)


def _mm_raw_kernel(h_ref, w_ref, o_ref):
    o_ref[...] = _dot(h_ref[...], w_ref[...]).astype(o_ref.dtype)


def _col_group_dots(h, w_ref, width):
    n = w_ref.shape[1]
    groups = [slice(c0, min(c0 + width, n)) for c0 in range(0, n, width)]
    return groups, [_dot(h, w_ref[:, g]) for g in groups]


def _mm_gate_kernel(h_ref, w_ref, loglb_ref, log1m_ref, onem_ref, lf_ref, k_ref):
    groups, zs = _col_group_dots(h_ref[...], w_ref, 2 * LANES)
    for g, z in zip(groups, zs):
        e = jnp.exp(-jnp.abs(z))
        r = 1.0 / (1.0 + e)
        log_sig = jnp.minimum(z, 0.0) + jnp.log(r)
        sig_neg = jnp.where(z >= 0.0, e * r, r)
        a = loglb_ref[:, g]
        b = log1m_ref[:, g] + log_sig
        lf_ref[:, g] = jnp.maximum(a, b) + jnp.log(1.0 + jnp.exp(-jnp.abs(a - b)))
        k_ref[:, g] = onem_ref[:, g] * sig_neg


def _mm_att_kernel(h_ref, w_ref, cs_ref, ss_ref, cm_ref, sm_ref, o_ref):
    groups, ps = _col_group_dots(h_ref[...], w_ref, 2 * LANES)
    cs, ss = cs_ref[...], ss_ref[...]
    swa_scale = SWA_HD ** -0.5
    for g, p in zip(groups, ps):
        for c0 in range(g.start, g.stop, LANES):
            x = p[:, c0 - g.start:c0 - g.start + LANES]
            if c0 < ATT_SV:
                x = x * cs + _rope_partner(x, SWA_HD // 4) * ss
                if c0 < ATT_SK:
                    x = x * swa_scale
            elif c0 == ATT_KR:
                x = x * cm_ref[...] + _rope_partner(x, MLA_ROPE // 4) * sm_ref[...]
            o_ref[:, c0:c0 + LANES] = x.astype(o_ref.dtype)


def _const_spec(a):
    if isinstance(a, tuple):
        stacked, layer = a
        return pl.BlockSpec((None,) + stacked.shape[1:], lambda i, *_: (layer,) + (0,) * (stacked.ndim - 1))
    return pl.BlockSpec(a.shape, lambda i, *_: (0,) * a.ndim)


def _row_call(kernel, rows, row_inputs, const_inputs, out_widths, out_dtypes, name):
    tm = rows.tm
    in_specs = [pl.BlockSpec((tm, a.shape[1]), lambda i: (i, 0)) for a in row_inputs]
    in_specs += [_const_spec(a) for a in const_inputs]
    const_inputs = [a[0] if isinstance(a, tuple) else a for a in const_inputs]
    out_specs = [pl.BlockSpec((tm, w), lambda i: (i, 0)) for w in out_widths]
    out_shape = [jax.ShapeDtypeStruct((rows.n_rows, w), dt) for w, dt in zip(out_widths, out_dtypes)]
    return pl.pallas_call(
        kernel, grid=(rows.n_tiles,), in_specs=in_specs, out_specs=out_specs, out_shape=out_shape,
        compiler_params=_cparams(("arbitrary",)), name=name,
    )(*row_inputs, *const_inputs)


def _hg_scan_kernel(*refs, reverse, n_chunks, readout):
    if readout:
        q_ref, k_ref, lf_ref, v_ref, of_ref, g_ref, ng_ref, o_ref, st_ref = refs
    else:
        q_ref, k_ref, lf_ref, v_ref, o_ref, st_ref = refs

    @pl.when(pl.program_id(1) == 0)
    def _():
        st_ref[...] = jnp.zeros_like(st_ref)

    c, s = HG_CHUNK, HG_SUB
    nsub = c // s
    chunks = [n_chunks - 1 - cc if reverse else cc for cc in range(n_chunks)]
    items = [(ch, hd) for ch in chunks for hd in range(HG_HEADS)]
    rows = lambda ch: slice(ch * c, (ch + 1) * c)
    cols = lambda hd: slice(hd * HG_DK, (hd + 1) * HG_DK)

    ri = lax.broadcasted_iota(jnp.int32, (c, c), 0)
    ci = lax.broadcasted_iota(jnp.int32, (c, c), 1)
    tri = jnp.where((ci >= ri) if reverse else (ci <= ri), 1.0, 0.0).astype(BF16)
    b_all = {}
    for ch in chunks:
        lf = lf_ref[rows(ch), :]
        hi = lf.astype(BF16)
        r1 = lf - hi.astype(F32)
        mid = r1.astype(BF16)
        lo = (r1 - mid.astype(F32)).astype(BF16)
        b_all[ch] = _dot(tri, hi) + _dot(tri, mid) + _dot(tri, lo)

    qe, k_dec, decay, v16, qt, kt = {}, {}, {}, {}, {}, {}
    for it in items:
        ch, hd = it
        b = b_all[ch][:, cols(hd)]
        q, k = q_ref[rows(ch), cols(hd)], k_ref[rows(ch), cols(hd)]
        btot = b[0:1, :] if reverse else b[c - 1:c, :]
        qe[it] = (q * jnp.exp(b)).astype(BF16)
        k_dec[it] = (k * jnp.exp(btot - b)).astype(BF16)
        decay[it] = jnp.exp(btot)
        v16[it] = v_ref[rows(ch), cols(hd)].astype(BF16)
        for i in range(nsub):
            r0 = i * s
            if reverse:
                k0, k1 = r0, c
                ref = b[r0 + s:r0 + s + 1, :] if i < nsub - 1 else jnp.zeros_like(btot)
            else:
                k0, k1 = 0, r0 + s
                ref = b[r0 - 1:r0, :] if i > 0 else jnp.zeros_like(btot)
            qt[it, i] = (q[r0:r0 + s] * jnp.exp(b[r0:r0 + s] - ref)).astype(BF16)
            kt[it, i] = (k[k0:k1] * jnp.exp(ref - b[k0:k1])).astype(BF16)

    upd = {it: _dot_tn(v16[it], k_dec[it]) for it in items}
    att = {}
    for it in items:
        for i in range(nsub):
            r0 = i * s
            k0 = r0 if reverse else 0
            a = _dot_nt(qt[it, i], kt[it, i])
            rr = lax.broadcasted_iota(jnp.int32, a.shape, 0) + r0
            cc = lax.broadcasted_iota(jnp.int32, a.shape, 1) + k0
            att[it, i] = jnp.where((cc >= rr) if reverse else (cc <= rr), a, 0.0).astype(BF16)

    states = [st_ref[hd] for hd in range(HG_HEADS)]
    o_inter = {}
    for it in items:
        ch, hd = it
        o_inter[it] = _dot_nt(qe[it], states[hd].astype(BF16))
        states[hd] = states[hd] * decay[it] + upd[it]
    for hd in range(HG_HEADS):
        st_ref[hd] = states[hd]

    for it in items:
        ch, hd = it
        outs = []
        for i in range(nsub):
            r0 = i * s
            k0, k1 = (r0, c) if reverse else (0, r0 + s)
            outs.append(o_inter[it][r0:r0 + s] + _dot(att[it, i], v16[it][k0:k1]))
        o = jnp.concatenate(outs, axis=0)
        if readout:
            o = o + of_ref[rows(ch), cols(hd)]
            o = o * lax.rsqrt(jnp.mean(o * o, axis=-1, keepdims=True) + RMS_EPS) * ng_ref[...]
            o = o * _silu(g_ref[rows(ch), cols(hd)])
        o_ref[rows(ch), cols(hd)] = o.astype(o_ref.dtype)


def _hg_scan(p_raw, kk, lf, batch, seq, ctx_len, reverse, readout_args=None):
    n_rows = p_raw.shape[0]
    tb = ctx_len
    nl = seq // tb
    ctx0 = batch * seq // tb
    direction = 1 if reverse else 0

    def row_block(b, i):
        lat = b * nl + (nl - i if reverse else i - 1)
        return jnp.where(i == 0, ctx0 + b, lat)

    def col(j):
        return pl.BlockSpec((tb, HG_W), lambda b, i: (row_block(b, i), j))

    spec = col(0)
    readout = readout_args is not None
    inputs = [p_raw, kk, lf, p_raw]
    in_specs = [col(0), col(direction), col(direction), col(1)]
    if readout:
        o_f, norm_g = readout_args
        inputs += [o_f, p_raw, norm_g]
        in_specs += [spec, col(2), pl.BlockSpec((1, HG_DK), lambda b, i: (0, 0))]
    kern = functools.partial(_hg_scan_kernel, reverse=reverse, n_chunks=tb // HG_CHUNK, readout=readout)
    return pl.pallas_call(
        kern, grid=(batch, nl + 1), in_specs=in_specs, out_specs=spec,
        out_shape=jax.ShapeDtypeStruct((n_rows, HG_W), BF16 if readout else F32),
        scratch_shapes=[pltpu.VMEM((HG_HEADS, HG_DK, HG_DK), F32)],
        compiler_params=_cparams(("arbitrary", "arbitrary")),
        name="hgrn2_scan_bwd_readout" if readout else "hgrn2_scan_fwd",
    )(*inputs)


def _mla_proj_kernel(cq_ref, ckv_ref, kr_ref, qn_ref, kvn_ref, wq_ref, wkv_ref, cm_ref, sm_ref,
                     q_ref, k_ref, v_ref):
    def rms(x, g):
        xf = x.astype(F32)
        return (xf * lax.rsqrt(jnp.mean(xf * xf, axis=-1, keepdims=True) + RMS_EPS) * g).astype(BF16)

    scale = (MLA_NOPE + MLA_ROPE) ** -0.5 * math.log2(math.e)
    _, qs = _col_group_dots(rms(cq_ref[...], qn_ref[...]), wq_ref, MLA_QK_PAD)
    _, kv2 = _col_group_dots(rms(ckv_ref[...], kvn_ref[...]), wkv_ref, 2 * LANES)
    kvs = [g[:, half * LANES:(half + 1) * LANES] for g in kv2 for half in range(2)]
    cm, sm = cm_ref[...], sm_ref[...]
    kr = kr_ref[...]
    lane = lax.broadcasted_iota(jnp.int32, (kr.shape[0], LANES), 1)
    ones_col = jnp.where(lane == 0, 1.0, 0.0).astype(v_ref.dtype)
    for hd in range(MLA_HEADS):
        c0 = hd * MLA_QK_PAD
        q_ref[:, c0:c0 + LANES] = (qs[hd][:, 0:LANES] * scale).astype(q_ref.dtype)
        x = qs[hd][:, LANES:2 * LANES]
        y = (x * cm + _rope_partner(x, MLA_ROPE // 4) * sm) * scale
        q_ref[:, c0 + LANES:c0 + 2 * LANES] = y.astype(q_ref.dtype)
        k_ref[:, c0:c0 + LANES] = kvs[hd].astype(k_ref.dtype)
        k_ref[:, c0 + LANES:c0 + 2 * LANES] = kr
        v0 = hd * MLA_V_PAD
        v_ref[:, v0:v0 + MLA_V] = kvs[MLA_HEADS + hd].astype(v_ref.dtype)
        v_ref[:, v0 + MLA_V:v0 + MLA_V_PAD] = ones_col


def _mla_proj(p_att, q_norm, kv_norm, wq, wkv, cm, sm, rows):
    tm = rows.tm
    n = rows.n_rows
    hq = MLA_HEADS * MLA_QK_PAD

    def col(width, off):
        return pl.BlockSpec((tm, width), lambda i: (i, off // width))

    def whole(a):
        return pl.BlockSpec(a.shape, lambda i: (0, 0))

    return pl.pallas_call(
        _mla_proj_kernel, grid=(rows.n_tiles,),
        in_specs=[col(MLA_Q_LORA, ATT_CQ), col(MLA_KV_LORA, ATT_CKV), col(LANES, ATT_KR),
                  whole(q_norm), whole(kv_norm), whole(wq), whole(wkv),
                  pl.BlockSpec((tm, LANES), lambda i: (i, 0)), pl.BlockSpec((tm, LANES), lambda i: (i, 0))],
        out_specs=[pl.BlockSpec((tm, hq), lambda i: (i, 0)), pl.BlockSpec((tm, hq), lambda i: (i, 0)),
                   pl.BlockSpec((tm, MLA_HEADS * MLA_V_PAD), lambda i: (i, 0))],
        out_shape=[jax.ShapeDtypeStruct((n, hq), BF16), jax.ShapeDtypeStruct((n, hq), BF16),
                   jax.ShapeDtypeStruct((n, MLA_HEADS * MLA_V_PAD), BF16)],
        compiler_params=_cparams(("arbitrary",)), name="mla_proj",
    )(p_att, p_att, p_att, q_norm, kv_norm, wq, wkv, cm, sm)


def _lane_tile_fold(x, op):
    out = x[:, 0:LANES]
    for t in range(1, x.shape[1] // LANES):
        out = op(out, x[:, t * LANES:(t + 1) * LANES])
    return out


def _mla_attn_kernel(*refs, ck, with_lat):
    if with_lat:
        q_ref, kl_ref, vl_ref, kc_ref, vc_ref, o_ref = refs
    else:
        q_ref, kc_ref, vc_ref, o_ref = refs
    chunks = [(kc_ref, vc_ref, 0, kc_ref.shape[0])]
    if with_lat:
        chunks += [(kl_ref, vl_ref, c * ck, ck) for c in range(kl_ref.shape[0] // ck)]
    m = [None] * MLA_HEADS_PER_STEP
    acc = [None] * MLA_HEADS_PER_STEP
    for k_ref, v_ref, r0, n in chunks:
        for hd in range(MLA_HEADS_PER_STEP):
            qk = slice(hd * MLA_QK_PAD, (hd + 1) * MLA_QK_PAD)
            s = _dot_nt(q_ref[:, qk], k_ref[r0:r0 + n, qk])
            m_c = jnp.max(_lane_tile_fold(s, jnp.maximum), axis=-1, keepdims=True)
            m_new = m_c if m[hd] is None else jnp.maximum(m[hd], m_c)
            pv = _dot(jnp.exp2(s - m_new).astype(BF16), v_ref[r0:r0 + n, hd * MLA_V_PAD:(hd + 1) * MLA_V_PAD])
            acc[hd] = pv if m[hd] is None else acc[hd] * jnp.exp2(m[hd] - m_new) + pv
            m[hd] = m_new
    for hd in range(MLA_HEADS_PER_STEP):
        o = acc[hd][:, 0:MLA_V] / acc[hd][:, MLA_V:MLA_V + 1]
        o_ref[:, hd * MLA_V:(hd + 1) * MLA_V] = o.astype(o_ref.dtype)


def _mla_attn(q, k, v, batch, seq, ctx_len, latent):
    ctx0 = batch * seq // ctx_len
    ck = 1024 if seq % 1024 == 0 else ctx_len
    hps = MLA_HEADS_PER_STEP
    if latent:
        tq = MLA_TQ if seq % MLA_TQ == 0 else ctx_len
        nq = seq // tq
        q_row = lambda b, h, i: b * nq + i
        o_row, n_rows = q_row, batch * seq
    else:
        tq, nq = ctx_len, 1
        q_row = lambda b, h, i: ctx0 + b
        o_row, n_rows = (lambda b, h, i: b), batch * ctx_len
    in_specs = [pl.BlockSpec((tq, hps * MLA_QK_PAD), lambda b, h, i: (q_row(b, h, i), h))]
    inputs = [q]
    if latent:
        in_specs += [pl.BlockSpec((seq, hps * MLA_QK_PAD), lambda b, h, i: (b, h)),
                     pl.BlockSpec((seq, hps * MLA_V_PAD), lambda b, h, i: (b, h))]
        inputs += [k, v]
    in_specs += [pl.BlockSpec((ctx_len, hps * MLA_QK_PAD), lambda b, h, i: (ctx0 + b, h)),
                 pl.BlockSpec((ctx_len, hps * MLA_V_PAD), lambda b, h, i: (ctx0 + b, h))]
    inputs += [k, v]
    return pl.pallas_call(
        functools.partial(_mla_attn_kernel, ck=ck, with_lat=latent),
        grid=(batch, MLA_HEADS // hps, nq), in_specs=in_specs,
        out_specs=pl.BlockSpec((tq, hps * MLA_V), lambda b, h, i: (o_row(b, h, i), h)),
        out_shape=jax.ShapeDtypeStruct((n_rows, MLA_HEADS * MLA_V), BF16),
        compiler_params=_cparams(("arbitrary", "arbitrary", "arbitrary")),
        name="mla_attn_lat" if latent else "mla_attn_ctx",
    )(*inputs)


def _swa_kernel(sink_ref, q_ref, kp_ref, kn_ref, kx_ref, kc_ref, vp_ref, vn_ref, vx_ref, vc_ref, o_ref,
                *, nb, seq, ctx_queries):
    n = pl.program_id(1)
    g = SWA_HEADS // SWA_KV_HEADS
    blk = SWA_BLOCK

    def attend(band):
        if band:
            a = lax.broadcasted_iota(jnp.int32, (g * blk, 3 * blk), 0) & (blk - 1)
            j = lax.broadcasted_iota(jnp.int32, (g * blk, 3 * blk), 1)
            key_pos = (n - 1) * blk + j
            keep = (jnp.abs(j - blk - a) <= SWA_WINDOW) & (key_pos >= 0) & (key_pos < seq)
        heads = range(SWA_KV_HEADS)
        ks = [slice(kh * SWA_HD, (kh + 1) * SWA_HD) for kh in heads]
        row = lax.broadcasted_iota(jnp.int32, (g * blk, 1), 0)
        qs, sink, s_c, s_b = [], [], [], []
        for kh in heads:
            qs.append(jnp.concatenate(
                [q_ref[:, (kh * g + gi) * SWA_HD:(kh * g + gi + 1) * SWA_HD] for gi in range(g)], axis=0))
            sk = jnp.zeros((g * blk, 1), F32)
            for gi in range(g):
                sk = jnp.where((row >= gi * blk) & (row < (gi + 1) * blk), sink_ref[kh * g + gi], sk)
            sink.append(sk)
            s_c.append(_dot_nt(qs[kh], kc_ref[:, ks[kh]]))
            if band:
                kb = jnp.concatenate([kp_ref[:, ks[kh]], kn_ref[:, ks[kh]], kx_ref[:, ks[kh]]], axis=0)
                s_b.append(jnp.where(keep, _dot_nt(qs[kh], kb), -jnp.inf))
        p_c, p_b, den = [], [], []
        for kh in heads:
            m = jnp.maximum(jnp.max(s_c[kh], axis=-1, keepdims=True), sink[kh])
            if band:
                m = jnp.maximum(m, jnp.max(s_b[kh], axis=-1, keepdims=True))
            pc = jnp.exp(s_c[kh] - m)
            dn = jnp.sum(pc, axis=-1, keepdims=True) + jnp.exp(sink[kh] - m)
            if band:
                pb = jnp.exp(s_b[kh] - m)
                dn = dn + jnp.sum(pb, axis=-1, keepdims=True)
                p_b.append(pb.astype(BF16))
            p_c.append(pc.astype(BF16))
            den.append(dn)
        for kh in heads:
            acc = _dot(p_c[kh], vc_ref[:, ks[kh]])
            if band:
                vb = jnp.concatenate([vp_ref[:, ks[kh]], vn_ref[:, ks[kh]], vx_ref[:, ks[kh]]], axis=0)
                acc = acc + _dot(p_b[kh], vb)
            o = acc / den[kh]
            for gi in range(g):
                c0 = (kh * g + gi) * SWA_HD
                o_ref[:, c0:c0 + SWA_HD] = o[gi * blk:(gi + 1) * blk].astype(o_ref.dtype)

    if ctx_queries:
        pl.when(n < nb)(lambda: attend(True))
        pl.when(n >= nb)(lambda: attend(False))
    else:
        attend(True)


def _swa(p_att, sink, batch, seq, ctx_len, ctx_queries):
    blk = SWA_BLOCK
    nb, ncb = seq // blk, ctx_len // blk
    ctx0 = batch * seq // ctx_len
    kcol, vcol = ATT_SK // SWA_KV_W, ATT_SV // SWA_KV_W
    q_row = lambda b, n: jnp.where(n < nb, b * nb + n, batch * nb + b * ncb + (n - nb))
    n_rows = batch * seq + (batch * ctx_len if ctx_queries else 0)

    def nbr(col, d):
        return pl.BlockSpec((blk, SWA_KV_W), lambda b, n: (b * nb + jnp.clip(n + d, 0, nb - 1), col))

    ctx_k = pl.BlockSpec((ctx_len, SWA_KV_W), lambda b, n: (ctx0 + b, kcol))
    ctx_v = pl.BlockSpec((ctx_len, SWA_KV_W), lambda b, n: (ctx0 + b, vcol))
    return pl.pallas_call(
        functools.partial(_swa_kernel, nb=nb, seq=seq, ctx_queries=ctx_queries),
        grid=(batch, nb + (ncb if ctx_queries else 0)),
        in_specs=[pl.BlockSpec(memory_space=pltpu.SMEM),
                  pl.BlockSpec((blk, SWA_W), lambda b, n: (q_row(b, n), 0)),
                  nbr(kcol, -1), nbr(kcol, 0), nbr(kcol, 1), ctx_k, nbr(vcol, -1), nbr(vcol, 0), nbr(vcol, 1), ctx_v],
        out_specs=pl.BlockSpec((blk, SWA_W), lambda b, n: (q_row(b, n), 0)),
        out_shape=jax.ShapeDtypeStruct((n_rows, SWA_W), BF16),
        compiler_params=_cparams(("arbitrary", "arbitrary")),
        name="swa_attn",
    )(sink, *([p_att] * 9))


def _route(logits, rb):
    aff = _sigmoid(logits)
    sel = aff + rb
    tm = logits.shape[1]
    scores = []
    for g in range(N_GROUPS):
        r = [sel[g * E_PER_GROUP + j:g * E_PER_GROUP + j + 1] for j in range(E_PER_GROUP)]
        best = None
        for i in range(E_PER_GROUP):
            for j in range(i + 1, E_PER_GROUP):
                pair = r[i] + r[j]
                best = pair if best is None else jnp.maximum(best, pair)
        scores.append(best)
    gbest, gi = scores[0], jnp.zeros((1, tm), jnp.int32)
    for g in range(1, N_GROUPS):
        upd = scores[g] > gbest
        gbest = jnp.where(upd, scores[g], gbest)
        gi = jnp.where(upd, g, gi)
    s_in, a_in = [], []
    for j in range(E_PER_GROUP):
        sj, aj = sel[j:j + 1], aff[j:j + 1]
        for g in range(1, N_GROUPS):
            e = g * E_PER_GROUP + j
            sj = jnp.where(gi == g, sel[e:e + 1], sj)
            aj = jnp.where(gi == g, aff[e:e + 1], aj)
        s_in.append(sj)
        a_in.append(aj)
    chosen = []
    for j in range(E_PER_GROUP):
        rank = jnp.zeros((1, tm), jnp.int32)
        for k in range(E_PER_GROUP):
            if k == j:
                continue
            ahead = (s_in[k] >= s_in[j]) if k < j else (s_in[k] > s_in[j])
            rank = rank + ahead.astype(jnp.int32)
        chosen.append(rank < 2)
    w = [jnp.where(chosen[j], a_in[j], 0.0) for j in range(E_PER_GROUP)]
    wsum = w[0] + w[1] + w[2] + w[3]
    gate_in = [wj / wsum * ROUTE_SCALE for wj in w]
    code = sum(jnp.where(chosen[j], 1 << j, 0) for j in range(E_PER_GROUP))
    pair = jnp.zeros((1, tm), jnp.int32)
    for idx, cval in enumerate((3, 5, 9, 6, 10, 12)):
        pair = jnp.where(code == cval, idx, pair)
    bucket = gi * N_PAIRS + pair
    g_lo = jnp.zeros((1, tm), F32)
    g_hi = jnp.zeros((1, tm), F32)
    seen = jnp.zeros((1, tm), jnp.bool_)
    for j in range(E_PER_GROUP):
        g_lo = jnp.where(chosen[j] & ~seen, gate_in[j], g_lo)
        g_hi = jnp.where(chosen[j] & seen, gate_in[j], g_hi)
        seen = seen | chosen[j]
    return g_lo, g_hi, bucket


def _lat_or_ctx(lat_ref, ctx_ref, rs, lat_tiles):
    if ctx_ref is None:
        return lat_ref[rs, :]
    return jnp.where(pl.program_id(0) < lat_tiles, lat_ref[rs, :], ctx_ref[rs, :])


def _dual_specs(rows, width):
    return [pl.BlockSpec((rows.tm, width), lambda i, *_: (jnp.minimum(i, rows.lat_tiles - 1), 0)),
            pl.BlockSpec((rows.tm, width), lambda i, *_: (jnp.maximum(i - rows.lat_tiles, 0), 0),
                         pipeline_mode=pl.Buffered(1))]


def _out_kernel(*refs, alpha, dual, lat_tiles):
    if dual:
        (ohg_ref, omla_ref, omlac_ref, oswa_ref, x_ref, xc_ref, g1_ref, sh2_ref, sc2_ref, lng_ref, lnb_ref,
         w_ref, rwt_ref, rb_ref, x1_ref, h2_ref, bucket_ref) = refs
    else:
        (ohg_ref, omla_ref, oswa_ref, x_ref, g1_ref, sh2_ref, sc2_ref, lng_ref, lnb_ref,
         w_ref, rwt_ref, rb_ref, x1_ref, h2_ref, bucket_ref) = refs
        omlac_ref = xc_ref = None
    o1, o2 = HG_W, HG_W + MLA_HEADS * MLA_V
    d = x_ref.shape[1]
    tm = x_ref.shape[0]
    halves = [slice(0, tm // 2), slice(tm // 2, tm)]
    mixes = [_dot(ohg_ref[rs, :], w_ref[0:o1, :])
             + _dot(_lat_or_ctx(omla_ref, omlac_ref, rs, lat_tiles), w_ref[o1:o2, :])
             + _dot(oswa_ref[rs, :], w_ref[o2:, :]) for rs in halves]
    for rs, mix in zip(halves, mixes):
        x_in = _lat_or_ctx(x_ref, xc_ref, rs, lat_tiles)
        x1 = _ln_rows(alpha * x_in + g1_ref[...] * mix) * lng_ref[...] + lnb_ref[...]
        x1_ref[rs, :] = x1
        h2 = (_ln_rows(x1) * (1.0 + sc2_ref[...]) + sh2_ref[...]).astype(BF16)
        h2_ref[rs, 0:d] = h2.astype(h2_ref.dtype)
        g_lo, g_hi, bucket = _route(_dot_nt(rwt_ref[...], h2), rb_ref[...])
        bucket_ref[:, rs] = bucket
        gate_rows = jnp.concatenate([g_lo, g_hi, jnp.zeros((LANES - 2, tm // 2), F32)], axis=0)
        h2_ref[rs, d:d + LANES] = gate_rows.T


def _out_proj(o_hg, o_mla, o_swa, x, mod, ln_g, ln_b, w_out, rwt, rb, rows, alpha):
    tm = rows.tm
    dual = isinstance(x, tuple)
    d = (x[0] if dual else x).shape[1]
    n = rows.n_rows
    row = lambda w: pl.BlockSpec((tm, w), lambda i: (i, 0))
    whole = lambda a: pl.BlockSpec(a.shape, lambda i: (0, 0), pipeline_mode=pl.Buffered(1))
    mla_w = MLA_HEADS * MLA_V
    if dual:
        in_specs = [row(HG_W)] + _dual_specs(rows, mla_w) + [row(SWA_W)] + _dual_specs(rows, d)
        inputs = [o_hg, *o_mla, o_swa, *x]
    else:
        in_specs = [row(HG_W), row(mla_w), row(SWA_W), row(d)]
        inputs = [o_hg, o_mla, o_swa, x]
    in_specs += [_mod_spec(rows, 2, d), _mod_spec(rows, 3, d), _mod_spec(rows, 4, d),
                 whole(ln_g), whole(ln_b), whole(w_out), whole(rwt), whole(rb)]
    inputs += [mod, mod, mod, ln_g, ln_b, w_out, rwt, rb]
    return pl.pallas_call(
        functools.partial(_out_kernel, alpha=alpha, dual=dual, lat_tiles=rows.lat_tiles), grid=(rows.n_tiles,),
        in_specs=in_specs,
        out_specs=[row(d), row(d + LANES), pl.BlockSpec((1, tm), lambda i: (0, i))],
        out_shape=[jax.ShapeDtypeStruct((n, d), F32), jax.ShapeDtypeStruct((n, d + LANES), F32),
                   jax.ShapeDtypeStruct((1, n), jnp.int32)],
        compiler_params=_cparams(("arbitrary",)), name="out_proj_ln_router",
    )(*inputs)


def _row_gather_start(idx_ref, base, src_hbm, dst, sem, n, static_rows=False):
    def issue(r, carry):
        pltpu.make_async_copy(src_hbm.at[pl.ds(idx_ref[base + r], 1)], dst.at[pl.ds(r, 1)], sem).start()
        return carry
    if static_rows:
        for r in range(n):
            issue(r, 0)
    else:
        lax.fori_loop(0, n, issue, 0, unroll=8)


def _row_gather_start_next(t, n_valid, idx_ref, base, src_hbm, buf, sem, n):
    for half in range(2):
        @pl.when((t < n_valid) & (t % 2 == half))
        def _():
            _row_gather_start(idx_ref, base, src_hbm, buf.at[half], sem.at[half], n, static_rows=True)


def _row_gather_wait(src_hbm, dst, sem, n):
    pltpu.make_async_copy(src_hbm.at[pl.ds(0, n)], dst, sem).wait()


def _moe_kernel(stok_ref, off_ref, e1_ref, e2_ref, nused_ref, h_hbm, wg1_ref, wu1_ref, wd1_ref,
                wg2_ref, wu2_ref, wd2_ref, y_ref, hbuf, sem, *, tm):
    i = pl.program_id(0)
    n_used = nused_ref[0]
    d = y_ref.shape[1]

    @pl.when(i == 0)
    def _():
        _row_gather_start(stok_ref, off_ref[0], h_hbm, hbuf.at[0], sem.at[0], tm)

    nxt = jnp.minimum(i + 1, pl.num_programs(0) - 1)
    _row_gather_start_next(i + 1, n_used, stok_ref, off_ref[nxt], h_hbm, hbuf, sem, tm)

    @pl.when(i < n_used)
    def _():
        slot = i % 2
        _row_gather_wait(h_hbm, hbuf.at[slot], sem.at[slot], tm)
        h = hbuf[slot, :, 0:d].astype(BF16)
        g_lo = hbuf[slot, :, d:d + 1]
        g_hi = hbuf[slot, :, d + 1:d + 2]

        act1 = (_silu(_dot(h, wg1_ref[...])) * _dot(h, wu1_ref[...])).astype(BF16)
        act2 = (_silu(_dot(h, wg2_ref[...])) * _dot(h, wu2_ref[...])).astype(BF16)
        y_ref[...] = g_lo * _dot(act1, wd1_ref[...]) + g_hi * _dot(act2, wd2_ref[...])

    @pl.when(i >= n_used)
    def _():
        y_ref[...] = jnp.zeros_like(y_ref)


def _moe(h2, stok, off, e1, e2, n_used, wg, wu, wd, layer, tm):
    n_tiles = off.shape[0]
    dx = h2.shape[1]
    d, f = wg.shape[2], wg.shape[3]
    last = lambda i, nu: jnp.minimum(i, nu[0] - 1)
    wspec_in = lambda sel: pl.BlockSpec(
        (None, None, d, f), lambda i, s, o, a, b, nu: (layer, (a, b)[sel][last(i, nu)], 0, 0))
    wspec_dn = lambda sel: pl.BlockSpec(
        (None, None, f, d), lambda i, s, o, a, b, nu: (layer, (a, b)[sel][last(i, nu)], 0, 0))
    grid_spec = pltpu.PrefetchScalarGridSpec(
        num_scalar_prefetch=5, grid=(n_tiles,),
        in_specs=[pl.BlockSpec(memory_space=pl.ANY),
                  wspec_in(0), wspec_in(0), wspec_dn(0), wspec_in(1), wspec_in(1), wspec_dn(1)],
        out_specs=pl.BlockSpec((tm, d), lambda i, s, o, a, b, nu: (i, 0)),
        scratch_shapes=[pltpu.VMEM((2, tm, dx), F32), pltpu.SemaphoreType.DMA((2,))])
    return pl.pallas_call(
        functools.partial(_moe_kernel, tm=tm), grid_spec=grid_spec,
        out_shape=jax.ShapeDtypeStruct((n_tiles * tm, d), F32),
        compiler_params=_cparams(("arbitrary",)), name="moe_grouped",
    )(stok, off, e1, e2, n_used, h2, wg, wu, wd, wg, wu, wd)


def _moe_plan(bucket, n_tokens, tm):
    tok = jnp.arange(n_tokens, dtype=jnp.int32)
    buckets = jnp.arange(N_BUCKETS, dtype=jnp.int32)
    skey, stok = lax.sort((bucket * n_tokens + tok, tok), num_keys=1)
    counts = jnp.sum((bucket[:, None] == buckets[None, :]).astype(jnp.int32), axis=0)
    padded = (counts + tm - 1) // tm * tm
    ends = jnp.cumsum(padded)
    shift = (ends - padded) - (jnp.cumsum(counts) - counts)
    sbucket = skey // n_tokens
    slot = tok + jnp.sum(jnp.where(sbucket[:, None] == buckets[None, :], shift[None, :], 0), axis=1)
    _, pos = lax.sort((stok, slot), num_keys=1)
    n_tiles = n_tokens // tm + N_BUCKETS
    tile_start = jnp.arange(n_tiles, dtype=jnp.int32) * tm
    tile_bucket = jnp.sum((ends[None, :] <= tile_start[:, None]).astype(jnp.int32), axis=1)
    tile_bucket = jnp.minimum(tile_bucket, N_BUCKETS - 1)
    tshift = jnp.sum(jnp.where(tile_bucket[:, None] == buckets[None, :], shift[None, :], 0), axis=1)
    off = jnp.clip(tile_start - tshift, 0, n_tokens)
    grp, pair = tile_bucket // N_PAIRS, tile_bucket % N_PAIRS
    lo = jnp.where(pair < 3, 0, jnp.where(pair < 5, 1, 2))
    hi = jnp.where(pair == 0, 1, jnp.where((pair == 1) | (pair == 3), 2, 3))
    e1, e2 = grp * E_PER_GROUP + lo, grp * E_PER_GROUP + hi
    n_used = (ends[-1] // tm).astype(jnp.int32).reshape(1)
    stok = jnp.concatenate([stok, jnp.zeros((tm,), jnp.int32)])
    return stok, off.astype(jnp.int32), pos.astype(jnp.int32), e1.astype(jnp.int32), e2.astype(jnp.int32), n_used


def _ln2_kernel(pos_ref, y_hbm, x1_ref, g2_ref, lng_ref, lnb_ref, *rest, alpha, tm, emit_h):
    if emit_h:
        sh_ref, sc_ref, x2_ref, h_ref, ybuf, sem = rest
    else:
        x2_ref, ybuf, sem = rest
    i = pl.program_id(0)
    n_tiles = pl.num_programs(0)

    @pl.when(i == 0)
    def _():
        _row_gather_start(pos_ref, 0, y_hbm, ybuf.at[0], sem.at[0], tm)

    _row_gather_start_next(i + 1, n_tiles, pos_ref, (i + 1) * tm, y_hbm, ybuf, sem, tm)

    slot = i % 2
    _row_gather_wait(y_hbm, ybuf.at[slot], sem.at[slot], tm)
    x2 = _ln_rows(alpha * x1_ref[...] + g2_ref[...] * ybuf[slot]) * lng_ref[...] + lnb_ref[...]
    x2_ref[...] = x2
    if emit_h:
        h_ref[...] = (_ln_rows(x2) * (1.0 + sc_ref[...]) + sh_ref[...]).astype(h_ref.dtype)


def _ln2(pos, y_sorted, x1, mod, mod_next, ln_g, ln_b, rows, alpha):
    tm = rows.tm
    d = x1.shape[1]
    emit_h = mod_next is not None
    row = pl.BlockSpec((tm, d), lambda i, p: (i, 0))
    whole = lambda a: pl.BlockSpec(a.shape, lambda i, p: (0, 0))
    in_specs = [pl.BlockSpec(memory_space=pl.ANY), row, _mod_spec(rows, 5, d), whole(ln_g), whole(ln_b)]
    inputs = [y_sorted, x1, mod, ln_g, ln_b]
    out_specs = [row]
    out_shape = [jax.ShapeDtypeStruct((rows.n_rows, d), F32)]
    if emit_h:
        in_specs += [_mod_spec(rows, 0, d), _mod_spec(rows, 1, d)]
        inputs += [mod_next, mod_next]
        out_specs.append(row)
        out_shape.append(jax.ShapeDtypeStruct((rows.n_rows, d), BF16))
    grid_spec = pltpu.PrefetchScalarGridSpec(
        num_scalar_prefetch=1, grid=(rows.n_tiles,), in_specs=in_specs, out_specs=out_specs,
        scratch_shapes=[pltpu.VMEM((2, tm, d), F32), pltpu.SemaphoreType.DMA((2,))])
    out = pl.pallas_call(
        functools.partial(_ln2_kernel, alpha=alpha, tm=tm, emit_h=emit_h), grid_spec=grid_spec,
        out_shape=out_shape, compiler_params=_cparams(("arbitrary",)), name="unpermute_ln2",
    )(pos, *inputs)
    return out if emit_h else (out[0], None)


def _rope_tables(seq, batch, ctx_rows, dim, pad_to):
    rows = seq // GRID_W
    row = jnp.repeat(jnp.arange(rows, dtype=jnp.int32), GRID_W)
    col = jnp.tile(jnp.arange(GRID_W, dtype=jnp.int32), rows)
    nf = dim // 4
    inv_freq = ROPE_BASE ** (-jnp.arange(nf, dtype=F32) / nf)
    ang = jnp.stack([row, col], -1).astype(F32)[:, :, None] * inv_freq
    cos, sin = jnp.cos(ang), jnp.sin(ang)
    c = jnp.stack([cos, cos], axis=2).reshape(seq, dim)
    s = jnp.stack([-sin, sin], axis=2).reshape(seq, dim)
    if pad_to > dim:
        c = jnp.pad(c, ((0, 0), (0, pad_to - dim)))
        s = jnp.pad(s, ((0, 0), (0, pad_to - dim)))
    ctx_c = jnp.zeros((ctx_rows, pad_to), F32).at[:, :dim].set(1.0)
    c = jnp.concatenate([jnp.tile(c, (batch, 1)), ctx_c], axis=0)
    s = jnp.concatenate([jnp.tile(s, (batch, 1)), jnp.zeros((ctx_rows, pad_to), F32)], axis=0)
    return c, s


def _w_in_prep_kernel(w_ref, raw_ref, gate_ref, att_ref):
    o1 = 5 * HG_W
    o2 = o1 + MLA_Q_LORA + MLA_KV_LORA
    o3 = o2 + MLA_ROPE
    cast = lambda a, b: w_ref[:, a:b].astype(BF16)
    raw_ref[:, 0:HG_W] = cast(0, HG_W)
    raw_ref[:, HG_W:3 * HG_W] = cast(3 * HG_W, o1)
    gate_ref[...] = cast(HG_W, 3 * HG_W)
    att_ref[:, ATT_SQ:ATT_CQ] = cast(o3, o3 + SWA_W + 2 * SWA_KV_W)
    att_ref[:, ATT_CQ:ATT_KR] = cast(o1, o2)
    att_ref[:, ATT_KR:ATT_KR + MLA_ROPE] = cast(o2, o3)
    att_ref[:, ATT_KR + MLA_ROPE:ATT_COLS] = jnp.zeros((w_ref.shape[0], LANES - MLA_ROPE), BF16)


def _w_in_prep(w_in):
    depth, d, n = w_in.shape
    tr = 256 if d % 256 == 0 else d
    widths = (3 * HG_W, 2 * HG_W, ATT_COLS)
    return pl.pallas_call(
        _w_in_prep_kernel, grid=(depth, d // tr),
        in_specs=[pl.BlockSpec((None, tr, n), lambda l, i: (l, i, 0))],
        out_specs=[pl.BlockSpec((None, tr, w), lambda l, i: (l, i, 0)) for w in widths],
        out_shape=[jax.ShapeDtypeStruct((depth, d, w), BF16) for w in widths],
        compiler_params=_cparams(("arbitrary", "arbitrary")), name="w_in_prep",
    )(w_in)


def _mla_weights(w_uq, w_ukv):
    qk = MLA_NOPE + MLA_ROPE
    wq = w_uq.reshape(MLA_Q_LORA, MLA_HEADS, qk)
    wq = jnp.pad(wq, ((0, 0), (0, 0), (0, MLA_QK_PAD - qk))).reshape(MLA_Q_LORA, MLA_HEADS * MLA_QK_PAD)
    wkv = w_ukv.reshape(MLA_KV_LORA, MLA_HEADS, MLA_NOPE + MLA_V)
    wkv = jnp.concatenate([wkv[:, :, :MLA_NOPE].reshape(MLA_KV_LORA, -1), wkv[:, :, MLA_NOPE:].reshape(MLA_KV_LORA, -1)], axis=1)
    return wq.astype(BF16), wkv.astype(BF16)


def kernel(x, c, ctx, c_ctx, w_ada, b_ada, w_in, w_out, hg_lb_logits, hg_norm_g, mla_q_norm, mla_kv_norm,
           mla_w_uq, mla_w_ukv, swa_sink, ln1_g, ln1_b, ln2_g, ln2_b, router_w, router_b,
           moe_w_gate, moe_w_up, moe_w_down):
    batch, seq, d = x.shape
    ctx_len = ctx.shape[1]
    depth = w_ada.shape[0]
    alpha = (2.0 * depth) ** 0.25
    n_lat, n_ctx = batch * seq, batch * ctx_len
    n_all = n_lat + n_ctx
    assert batch + 1 <= 8 and ctx_len % (4 * HG_CHUNK) == 0 and seq % ctx_len == 0 and seq % GRID_W == 0

    c8 = jnp.zeros((8, d), F32).at[:batch].set(c).at[batch].set(c_ctx)
    mod_all = _ada(c8, w_ada, b_ada).reshape(depth, 8 * 6, 1, d)

    lb = jnp.cumsum(jax.nn.softmax(hg_lb_logits.astype(F32), axis=0), axis=0)
    lb = (lb - lb[0:1]).reshape(depth, 1, 2 * HG_W)
    log_lb, log_1m, one_m = jnp.log(lb), jnp.log1p(-lb), 1.0 - lb

    cs, ss = _rope_tables(seq, batch, n_ctx, SWA_HD, SWA_HD)
    cm, sm = _rope_tables(seq, batch, n_ctx, MLA_ROPE, LANES)
    rwt = router_w.T.astype(BF16)
    rb = router_b.astype(F32).reshape(N_EXPERTS, 1)

    tm_all = _pick_tm(seq, n_ctx, 512)
    rows_all = _Rows(batch, seq, ctx_len, n_all, tm_all)
    rows_lat = _Rows(batch, seq, ctx_len, n_lat, tm_all)
    moe_tm = 256

    xa = (x.reshape(n_lat, d), ctx.reshape(n_ctx, d))
    h = _lnmod(xa[0], xa[1], mod_all[0], rows_all, 0)
    w_raw_all, w_gate_all, w_att_all = _w_in_prep(w_in)
    wg_all, wu_all, wd_all = moe_w_gate.astype(BF16), moe_w_up.astype(BF16), moe_w_down.astype(BF16)

    for layer in range(depth):
        need_ctx = layer < depth - 1
        mod = mod_all[layer]
        rows = rows_all if need_ctx else rows_lat
        wq, wkv = _mla_weights(mla_w_uq[layer], mla_w_ukv[layer])

        (p_raw,) = _row_call(_mm_raw_kernel, rows_all, [h], [(w_raw_all, layer)], [3 * HG_W], [F32], "in_proj_hg_raw")
        lf, kk = _row_call(_mm_gate_kernel, rows_all, [h],
                           [(w_gate_all, layer), log_lb[layer], log_1m[layer], one_m[layer]],
                           [2 * HG_W, 2 * HG_W], [F32, F32], "in_proj_hg_gates")
        tables = [cs, ss, cm, sm]
        p_att = pl.pallas_call(
            _mm_att_kernel, grid=(rows_all.n_tiles,),
            in_specs=[pl.BlockSpec((tm_all, d), lambda i: (i, 0)), _const_spec((w_att_all, layer))]
                     + [pl.BlockSpec((tm_all, LANES), lambda i: (i, 0))] * 4,
            out_specs=pl.BlockSpec((tm_all, ATT_COLS), lambda i: (i, 0)),
            out_shape=jax.ShapeDtypeStruct((n_all, ATT_COLS), BF16),
            compiler_params=_cparams(("arbitrary",)), name="in_proj_att",
        )(h, w_att_all, *tables)

        o_f = _hg_scan(p_raw, kk, lf, batch, seq, ctx_len, False)
        o_hg = _hg_scan(p_raw, kk, lf, batch, seq, ctx_len, True,
                        (o_f, hg_norm_g[layer].astype(F32).reshape(1, HG_DK)))

        q_mla, k_mla, v_mla = _mla_proj(p_att, mla_q_norm[layer].astype(F32).reshape(1, -1),
                                        mla_kv_norm[layer].astype(F32).reshape(1, -1), wq, wkv, cm, sm, rows_all)
        n_out = rows.n_rows
        o_mla = _mla_attn(q_mla, k_mla, v_mla, batch, seq, ctx_len, True)
        o_swa = _swa(p_att, swa_sink[layer].astype(F32), batch, seq, ctx_len, need_ctx)
        dual = isinstance(xa, tuple)
        if need_ctx:
            o_mla = (o_mla, _mla_attn(q_mla, k_mla, v_mla, batch, seq, ctx_len, False))
            if not dual:
                xa = (xa[:n_lat], xa[n_lat:])
        elif dual:
            xa = xa[0]

        x1, h2, bucket = _out_proj(
            o_hg, o_mla, o_swa, xa, mod, ln1_g[layer].astype(F32).reshape(1, d), ln1_b[layer].astype(F32).reshape(1, d),
            w_out[layer].astype(BF16), rwt, rb, rows, alpha)

        stok, off, pos, e1, e2, n_used = _moe_plan(bucket[0], n_out, moe_tm)
        y_sorted = _moe(h2, stok, off, e1, e2, n_used, wg_all, wu_all, wd_all, layer, moe_tm)
        xa, h = _ln2(pos, y_sorted, x1, mod, mod_all[layer + 1] if need_ctx else None,
                     ln2_g[layer].astype(F32).reshape(1, d), ln2_b[layer].astype(F32).reshape(1, d), rows, alpha)

    return xa[:n_lat].reshape(batch, seq, d)
```

```python
import functools
import math

import jax
import jax.numpy as jnp
from jax import lax
from jax.experimental import pallas as pl
from jax.experimental.pallas import tpu as pltpu

F32 = jnp.float32
BF16 = jnp.bfloat16

GRID_W = 64
HG_HEADS = 4
HG_DK = 128
HG_W = HG_HEADS * HG_DK
HG_CHUNK = 64
HG_SUB = 16
MLA_HEADS = 8
MLA_Q_LORA = 512
MLA_KV_LORA = 256
MLA_NOPE = 128
MLA_ROPE = 64
MLA_V = 128
MLA_QK_PAD = 256
MLA_V_PAD = 256
MLA_HEADS_PER_STEP = 4
MLA_TQ = 512
SWA_HEADS = 4
SWA_KV_HEADS = 2
SWA_HD = 128
SWA_W = SWA_HEADS * SWA_HD
SWA_KV_W = SWA_KV_HEADS * SWA_HD
SWA_WINDOW = 128
SWA_BLOCK = 128
SWA_BLOCKS_PER_STEP = 2
N_EXPERTS = 16
N_GROUPS = 4
E_PER_GROUP = 4
N_PAIRS = 6
N_BUCKETS = N_GROUPS * N_PAIRS
ROUTE_SCALE = 2.5
ROPE_BASE = 10000.0
LN_EPS = 1e-5
RMS_EPS = 1e-6
LANES = 128
VMEM_LIMIT = 56 * 1024 * 1024

ATT_SQ, ATT_SK, ATT_SV = 0, SWA_W, SWA_W + SWA_KV_W
ATT_CQ = SWA_W + 2 * SWA_KV_W
ATT_CKV = ATT_CQ + MLA_Q_LORA
ATT_KR = ATT_CKV + MLA_KV_LORA
ATT_COLS = ATT_KR + LANES


def _cparams(sem):
    return pltpu.CompilerParams(dimension_semantics=sem, vmem_limit_bytes=VMEM_LIMIT)


def _dot(a, b):
    return jnp.dot(a, b, preferred_element_type=F32)


def _dot_nt(a, b):
    return lax.dot_general(a, b, (((1,), (1,)), ((), ())), preferred_element_type=F32)


def _dot_tn(a, b):
    return lax.dot_general(a, b, (((0,), (0,)), ((), ())), preferred_element_type=F32)


def _sigmoid(x):
    return 1.0 / (1.0 + jnp.exp(-x))


def _silu(x):
    return x * _sigmoid(x)


def _ln_rows(x):
    mu = jnp.mean(x, axis=-1, keepdims=True)
    xc = x - mu
    var = jnp.mean(xc * xc, axis=-1, keepdims=True)
    return xc * lax.rsqrt(var + LN_EPS)


def _rope_partner(x, half):
    lane = lax.broadcasted_iota(jnp.int32, x.shape, x.ndim - 1)
    first = (lane & half) == 0
    n = x.shape[-1]
    return jnp.where(first, pltpu.roll(x, n - half, x.ndim - 1), pltpu.roll(x, half, x.ndim - 1))


def _ada_kernel(c_ref, w_ref, b_ref, o_ref):
    s = _silu(c_ref[...])
    o_ref[...] = _dot(s.astype(BF16), w_ref[...].astype(BF16)) + b_ref[...]


def _ada(c8, w_ada, b_ada):
    depth, d, n = w_ada.shape
    tn = 1024 if n % 1024 == 0 else n
    return pl.pallas_call(
        _ada_kernel,
        grid=(depth, n // tn),
        in_specs=[pl.BlockSpec((8, d), lambda l, j: (0, 0)),
                  pl.BlockSpec((None, d, tn), lambda l, j: (l, 0, j)),
                  pl.BlockSpec((None, 1, tn), lambda l, j: (l, 0, j))],
        out_specs=pl.BlockSpec((None, 8, tn), lambda l, j: (l, 0, j)),
        out_shape=jax.ShapeDtypeStruct((depth, 8, n), F32),
        compiler_params=_cparams(("arbitrary", "arbitrary")),
        name="ada_mod",
    )(c8, w_ada, b_ada.reshape(depth, 1, n))


class _Rows:
    def __init__(self, batch, seq, ctx_len, n_rows, tm):
        self.batch, self.seq, self.ctx_len, self.n_rows, self.tm = batch, seq, ctx_len, n_rows, tm
        self.n_tiles = n_rows // tm
        self.lat_tiles = batch * seq // tm
        self.tiles_per_batch = seq // tm

    def mod_row(self, i):
        return jnp.where(i < self.lat_tiles, i // self.tiles_per_batch, self.batch)


def _pick_tm(seq, ctx_rows, cap):
    for tm in (1024, 512, 256, 128):
        if tm <= cap and seq % tm == 0 and ctx_rows % tm == 0:
            return tm
    raise ValueError("unsupported sequence / context lengths")


def _mod_spec(rows, chunk, d):
    return pl.BlockSpec((None, 1, d), lambda i, *_: (rows.mod_row(i) * 6 + chunk, 0, 0))


def _lnmod_kernel(x_ref, xc_ref, sh_ref, sc_ref, h_ref, *, lat_tiles):
    y = _ln_rows(_lat_or_ctx(x_ref, xc_ref, slice(None), lat_tiles))
    h_ref[...] = (y * (1.0 + sc_ref[...]) + sh_ref[...]).astype(h_ref.dtype)


def _lnmod(x_lat, x_ctx, mod, rows, chunk0):
    d = x_lat.shape[1]
    tm = rows.tm
    return pl.pallas_call(
        functools.partial(_lnmod_kernel, lat_tiles=rows.lat_tiles),
        grid=(rows.n_tiles,),
        in_specs=_dual_specs(rows, d) + [_mod_spec(rows, chunk0, d), _mod_spec(rows, chunk0 + 1, d)],
        out_specs=pl.BlockSpec((tm, d), lambda i: (i, 0)),
        out_shape=jax.ShapeDtypeStruct((rows.n_rows, d), BF16),
        compiler_params=_cparams(("arbitrary",)),
        name="ln_modulate",
    )(x_lat, x_ctx, mod, mod)


def _col_group_dots(h, w_ref, width):
    n = w_ref.shape[1]
    groups = [slice(c0, min(c0 + width, n)) for c0 in range(0, n, width)]
    return groups, [_dot(h, w_ref[:, g]) for g in groups]


def _in_proj_kernel(h_ref, wgate_ref, watt_ref, wraw_ref, loglb_ref, log1m_ref, onem_ref,
                    cs_ref, ss_ref, cm_ref, sm_ref, lf_ref, k_ref, att_ref, raw_ref):
    h = h_ref[...]
    g_groups, zs = _col_group_dots(h, wgate_ref, 2 * LANES)
    a_groups, ps = _col_group_dots(h, watt_ref, 2 * LANES)
    r_groups, rs = _col_group_dots(h, wraw_ref, 2 * LANES)
    for g, z in zip(g_groups, zs):
        e = jnp.exp(-jnp.abs(z))
        r = 1.0 / (1.0 + e)
        log_sig = jnp.minimum(z, 0.0) + jnp.log(r)
        sig_neg = jnp.where(z >= 0.0, e * r, r)
        a = loglb_ref[:, g]
        b = log1m_ref[:, g] + log_sig
        lf_ref[:, g] = jnp.maximum(a, b) + jnp.log(1.0 + jnp.exp(-jnp.abs(a - b)))
        k_ref[:, g] = onem_ref[:, g] * sig_neg
    cs, ss = cs_ref[...], ss_ref[...]
    swa_scale = SWA_HD ** -0.5
    for g, p in zip(a_groups, ps):
        for c0 in range(g.start, g.stop, LANES):
            x = p[:, c0 - g.start:c0 - g.start + LANES]
            if c0 < ATT_SV:
                x = x * cs + _rope_partner(x, SWA_HD // 4) * ss
                if c0 < ATT_SK:
                    x = x * swa_scale
            elif c0 == ATT_KR:
                x = x * cm_ref[...] + _rope_partner(x, MLA_ROPE // 4) * sm_ref[...]
            att_ref[:, c0:c0 + LANES] = x.astype(att_ref.dtype)
    for g, r in zip(r_groups, rs):
        raw_ref[:, g] = r


def _in_proj(h, w_gate_all, w_att_all, w_raw_all, layer, log_lb, log_1m, one_m, tables, rows):
    tm = rows.tm
    d = h.shape[1]
    n = rows.n_rows
    single = lambda spec: pl.BlockSpec(spec.block_shape, spec.index_map, pipeline_mode=pl.Buffered(1))
    consts = [(w_gate_all, layer), (w_att_all, layer), (w_raw_all, layer), log_lb, log_1m, one_m]
    row = lambda w: pl.BlockSpec((tm, w), lambda i: (i, 0))
    widths = (2 * HG_W, 2 * HG_W, ATT_COLS, 3 * HG_W)
    dtypes = (F32, F32, BF16, F32)
    return pl.pallas_call(
        _in_proj_kernel, grid=(rows.n_tiles,),
        in_specs=[row(d)] + [single(_const_spec(a)) for a in consts] + [row(LANES)] * 4,
        out_specs=[row(w) for w in widths],
        out_shape=[jax.ShapeDtypeStruct((n, w), dt) for w, dt in zip(widths, dtypes)],
        compiler_params=_cparams(("arbitrary",)), name="in_proj",
    )(h, w_gate_all, w_att_all, w_raw_all, log_lb, log_1m, one_m, *tables)


def _const_spec(a):
    if isinstance(a, tuple):
        stacked, layer = a
        return pl.BlockSpec((None,) + stacked.shape[1:], lambda i, *_: (layer,) + (0,) * (stacked.ndim - 1))
    return pl.BlockSpec(a.shape, lambda i, *_: (0,) * a.ndim)


def _hg_scan_kernel(*refs, reverse, n_chunks, readout):
    if readout:
        q_ref, k_ref, lf_ref, v_ref, of_ref, g_ref, ng_ref, o_ref, st_ref = refs
    else:
        q_ref, k_ref, lf_ref, v_ref, o_ref, st_ref = refs

    @pl.when(pl.program_id(1) == 0)
    def _():
        st_ref[...] = jnp.zeros_like(st_ref)

    c, s = HG_CHUNK, HG_SUB
    nsub = c // s
    chunks = [n_chunks - 1 - cc if reverse else cc for cc in range(n_chunks)]
    items = [(ch, hd) for ch in chunks for hd in range(HG_HEADS)]
    rows = lambda ch: slice(ch * c, (ch + 1) * c)
    cols = lambda hd: slice(hd * HG_DK, (hd + 1) * HG_DK)

    ri = lax.broadcasted_iota(jnp.int32, (c, c), 0)
    ci = lax.broadcasted_iota(jnp.int32, (c, c), 1)
    tri = jnp.where((ci >= ri) if reverse else (ci <= ri), 1.0, 0.0).astype(BF16)
    b_all = {}
    for ch in chunks:
        lf = lf_ref[rows(ch), :]
        hi = lf.astype(BF16)
        r1 = lf - hi.astype(F32)
        mid = r1.astype(BF16)
        lo = (r1 - mid.astype(F32)).astype(BF16)
        b_all[ch] = _dot(tri, hi) + _dot(tri, mid) + _dot(tri, lo)

    qe, k_dec, decay, v16, qt, kt = {}, {}, {}, {}, {}, {}
    for it in items:
        ch, hd = it
        b = b_all[ch][:, cols(hd)]
        q, k = q_ref[rows(ch), cols(hd)], k_ref[rows(ch), cols(hd)]
        btot = b[0:1, :] if reverse else b[c - 1:c, :]
        qe[it] = (q * jnp.exp(b)).astype(BF16)
        k_dec[it] = (k * jnp.exp(btot - b)).astype(BF16)
        decay[it] = jnp.exp(btot)
        v16[it] = v_ref[rows(ch), cols(hd)].astype(BF16)
        for i in range(nsub):
            r0 = i * s
            if reverse:
                k0, k1 = r0, c
                ref = b[r0 + s:r0 + s + 1, :] if i < nsub - 1 else jnp.zeros_like(btot)
            else:
                k0, k1 = 0, r0 + s
                ref = b[r0 - 1:r0, :] if i > 0 else jnp.zeros_like(btot)
            qt[it, i] = (q[r0:r0 + s] * jnp.exp(b[r0:r0 + s] - ref)).astype(BF16)
            kt[it, i] = (k[k0:k1] * jnp.exp(ref - b[k0:k1])).astype(BF16)

    upd = {it: _dot_tn(v16[it], k_dec[it]) for it in items}
    att = {}
    for it in items:
        for i in range(nsub):
            r0 = i * s
            k0 = r0 if reverse else 0
            a = _dot_nt(qt[it, i], kt[it, i])
            rr = lax.broadcasted_iota(jnp.int32, a.shape, 0) + r0
            cc = lax.broadcasted_iota(jnp.int32, a.shape, 1) + k0
            att[it, i] = jnp.where((cc >= rr) if reverse else (cc <= rr), a, 0.0).astype(BF16)

    states = [st_ref[hd] for hd in range(HG_HEADS)]
    o_inter = {}
    for it in items:
        ch, hd = it
        o_inter[it] = _dot_nt(qe[it], states[hd].astype(BF16))
        states[hd] = states[hd] * decay[it] + upd[it]
    for hd in range(HG_HEADS):
        st_ref[hd] = states[hd]

    for it in items:
        ch, hd = it
        outs = []
        for i in range(nsub):
            r0 = i * s
            k0, k1 = (r0, c) if reverse else (0, r0 + s)
            outs.append(o_inter[it][r0:r0 + s] + _dot(att[it, i], v16[it][k0:k1]))
        o = jnp.concatenate(outs, axis=0)
        if readout:
            o = o + of_ref[rows(ch), cols(hd)]
            o = o * lax.rsqrt(jnp.mean(o * o, axis=-1, keepdims=True) + RMS_EPS) * ng_ref[...]
            o = o * _silu(g_ref[rows(ch), cols(hd)])
        o_ref[rows(ch), cols(hd)] = o.astype(o_ref.dtype)


def _hg_scan(p_raw, kk, lf, batch, seq, ctx_len, reverse, readout_args=None):
    n_rows = p_raw.shape[0]
    tb = ctx_len
    nl = seq // tb
    ctx0 = batch * seq // tb
    direction = 1 if reverse else 0

    def row_block(b, i):
        lat = b * nl + (nl - i if reverse else i - 1)
        return jnp.where(i == 0, ctx0 + b, lat)

    def col(j):
        return pl.BlockSpec((tb, HG_W), lambda b, i: (row_block(b, i), j))

    spec = col(0)
    readout = readout_args is not None
    inputs = [p_raw, kk, lf, p_raw]
    in_specs = [col(0), col(direction), col(direction), col(1)]
    if readout:
        o_f, norm_g = readout_args
        inputs += [o_f, p_raw, norm_g]
        in_specs += [spec, col(2), pl.BlockSpec((1, HG_DK), lambda b, i: (0, 0))]
    kern = functools.partial(_hg_scan_kernel, reverse=reverse, n_chunks=tb // HG_CHUNK, readout=readout)
    return pl.pallas_call(
        kern, grid=(batch, nl + 1), in_specs=in_specs, out_specs=spec,
        out_shape=jax.ShapeDtypeStruct((n_rows, HG_W), BF16 if readout else F32),
        scratch_shapes=[pltpu.VMEM((HG_HEADS, HG_DK, HG_DK), F32)],
        compiler_params=_cparams(("arbitrary", "arbitrary")),
        name="hgrn2_scan_bwd_readout" if readout else "hgrn2_scan_fwd",
    )(*inputs)


def _mla_proj_kernel(cq_ref, ckv_ref, kr_ref, qn_ref, kvn_ref, wq_ref, wkv_ref, cm_ref, sm_ref,
                     q_ref, k_ref, v_ref):
    def rms(x, g):
        xf = x.astype(F32)
        return (xf * lax.rsqrt(jnp.mean(xf * xf, axis=-1, keepdims=True) + RMS_EPS) * g).astype(BF16)

    scale = (MLA_NOPE + MLA_ROPE) ** -0.5 * math.log2(math.e)
    _, qs = _col_group_dots(rms(cq_ref[...], qn_ref[...]), wq_ref, MLA_QK_PAD)
    _, kv2 = _col_group_dots(rms(ckv_ref[...], kvn_ref[...]), wkv_ref, 2 * LANES)
    kvs = [g[:, half * LANES:(half + 1) * LANES] for g in kv2 for half in range(2)]
    cm, sm = cm_ref[...], sm_ref[...]
    kr = kr_ref[...]
    lane = lax.broadcasted_iota(jnp.int32, (kr.shape[0], LANES), 1)
    ones_col = jnp.where(lane == 0, 1.0, 0.0).astype(v_ref.dtype)
    for hd in range(MLA_HEADS):
        c0 = hd * MLA_QK_PAD
        q_ref[:, c0:c0 + LANES] = (qs[hd][:, 0:LANES] * scale).astype(q_ref.dtype)
        x = qs[hd][:, LANES:2 * LANES]
        y = (x * cm + _rope_partner(x, MLA_ROPE // 4) * sm) * scale
        q_ref[:, c0 + LANES:c0 + 2 * LANES] = y.astype(q_ref.dtype)
        k_ref[:, c0:c0 + LANES] = kvs[hd].astype(k_ref.dtype)
        k_ref[:, c0 + LANES:c0 + 2 * LANES] = kr
        v0 = hd * MLA_V_PAD
        v_ref[:, v0:v0 + MLA_V] = kvs[MLA_HEADS + hd].astype(v_ref.dtype)
        v_ref[:, v0 + MLA_V:v0 + MLA_V_PAD] = ones_col


def _mla_proj(p_att, q_norm, kv_norm, wq, wkv, cm, sm, rows):
    tm = rows.tm
    n = rows.n_rows
    hq = MLA_HEADS * MLA_QK_PAD

    def col(width, off):
        return pl.BlockSpec((tm, width), lambda i: (i, off // width))

    def whole(a):
        return pl.BlockSpec(a.shape, lambda i: (0, 0))

    return pl.pallas_call(
        _mla_proj_kernel, grid=(rows.n_tiles,),
        in_specs=[col(MLA_Q_LORA, ATT_CQ), col(MLA_KV_LORA, ATT_CKV), col(LANES, ATT_KR),
                  whole(q_norm), whole(kv_norm), whole(wq), whole(wkv),
                  pl.BlockSpec((tm, LANES), lambda i: (i, 0)), pl.BlockSpec((tm, LANES), lambda i: (i, 0))],
        out_specs=[pl.BlockSpec((tm, hq), lambda i: (i, 0)), pl.BlockSpec((tm, hq), lambda i: (i, 0)),
                   pl.BlockSpec((tm, MLA_HEADS * MLA_V_PAD), lambda i: (i, 0))],
        out_shape=[jax.ShapeDtypeStruct((n, hq), BF16), jax.ShapeDtypeStruct((n, hq), BF16),
                   jax.ShapeDtypeStruct((n, MLA_HEADS * MLA_V_PAD), BF16)],
        compiler_params=_cparams(("arbitrary",)), name="mla_proj",
    )(p_att, p_att, p_att, q_norm, kv_norm, wq, wkv, cm, sm)


def _lane_tile_fold(x, op):
    out = x[:, 0:LANES]
    for t in range(1, x.shape[1] // LANES):
        out = op(out, x[:, t * LANES:(t + 1) * LANES])
    return out


def _mla_attn_kernel(*refs, ck, with_lat):
    if with_lat:
        q_ref, kl_ref, vl_ref, kc_ref, vc_ref, o_ref = refs
    else:
        q_ref, kc_ref, vc_ref, o_ref = refs
    chunks = [(kc_ref, vc_ref, 0, kc_ref.shape[0])]
    if with_lat:
        chunks += [(kl_ref, vl_ref, c * ck, ck) for c in range(kl_ref.shape[0] // ck)]
    m = [None] * MLA_HEADS_PER_STEP
    acc = [None] * MLA_HEADS_PER_STEP
    for k_ref, v_ref, r0, n in chunks:
        for hd in range(MLA_HEADS_PER_STEP):
            qk = slice(hd * MLA_QK_PAD, (hd + 1) * MLA_QK_PAD)
            s = _dot_nt(q_ref[:, qk], k_ref[r0:r0 + n, qk])
            m_c = jnp.max(_lane_tile_fold(s, jnp.maximum), axis=-1, keepdims=True)
            m_new = m_c if m[hd] is None else jnp.maximum(m[hd], m_c)
            pv = _dot(jnp.exp2(s - m_new).astype(BF16), v_ref[r0:r0 + n, hd * MLA_V_PAD:(hd + 1) * MLA_V_PAD])
            acc[hd] = pv if m[hd] is None else acc[hd] * jnp.exp2(m[hd] - m_new) + pv
            m[hd] = m_new
    for hd in range(MLA_HEADS_PER_STEP):
        o = acc[hd][:, 0:MLA_V] / acc[hd][:, MLA_V:MLA_V + 1]
        o_ref[:, hd * MLA_V:(hd + 1) * MLA_V] = o.astype(o_ref.dtype)


def _mla_attn(q, k, v, batch, seq, ctx_len, latent):
    ctx0 = batch * seq // ctx_len
    ck = 1024 if seq % 1024 == 0 else ctx_len
    hps = MLA_HEADS_PER_STEP
    if latent:
        tq = MLA_TQ if seq % MLA_TQ == 0 else ctx_len
        nq = seq // tq
        q_row = lambda b, h, i: b * nq + i
        o_row, n_rows = q_row, batch * seq
    else:
        tq, nq = ctx_len, 1
        q_row = lambda b, h, i: ctx0 + b
        o_row, n_rows = (lambda b, h, i: b), batch * ctx_len
    in_specs = [pl.BlockSpec((tq, hps * MLA_QK_PAD), lambda b, h, i: (q_row(b, h, i), h))]
    inputs = [q]
    if latent:
        in_specs += [pl.BlockSpec((seq, hps * MLA_QK_PAD), lambda b, h, i: (b, h)),
                     pl.BlockSpec((seq, hps * MLA_V_PAD), lambda b, h, i: (b, h))]
        inputs += [k, v]
    in_specs += [pl.BlockSpec((ctx_len, hps * MLA_QK_PAD), lambda b, h, i: (ctx0 + b, h)),
                 pl.BlockSpec((ctx_len, hps * MLA_V_PAD), lambda b, h, i: (ctx0 + b, h))]
    inputs += [k, v]
    return pl.pallas_call(
        functools.partial(_mla_attn_kernel, ck=ck, with_lat=latent),
        grid=(batch, MLA_HEADS // hps, nq), in_specs=in_specs,
        out_specs=pl.BlockSpec((tq, hps * MLA_V), lambda b, h, i: (o_row(b, h, i), h)),
        out_shape=jax.ShapeDtypeStruct((n_rows, MLA_HEADS * MLA_V), BF16),
        compiler_params=_cparams(("arbitrary", "arbitrary", "arbitrary")),
        name="mla_attn_lat" if latent else "mla_attn_ctx",
    )(*inputs)


def _swa_kernel(sink_ref, q_ref, kp_ref, kn_ref, kx_ref, kc_ref, vp_ref, vn_ref, vx_ref, vc_ref, o_ref,
                *, n_lat_steps, seq, ctx_queries):
    n = pl.program_id(1)
    g = SWA_HEADS // SWA_KV_HEADS
    blk = SWA_BLOCK
    nblk = SWA_BLOCKS_PER_STEP

    def attend(band):
        items = [(j, kh) for j in range(nblk) for kh in range(SWA_KV_HEADS)]
        ks = [slice(kh * SWA_HD, (kh + 1) * SWA_HD) for kh in range(SWA_KV_HEADS)]
        row = lax.broadcasted_iota(jnp.int32, (g * blk, 1), 0)
        if band:
            a = lax.broadcasted_iota(jnp.int32, (g * blk, 3 * blk), 0) & (blk - 1)
            jj = lax.broadcasted_iota(jnp.int32, (g * blk, 3 * blk), 1)
            in_window = jnp.abs(jj - blk - a) <= SWA_WINDOW
            k_slab = [jnp.concatenate([kp_ref[:, s], kn_ref[:, s], kx_ref[:, s]], axis=0) for s in ks]
            v_slab = [jnp.concatenate([vp_ref[:, s], vn_ref[:, s], vx_ref[:, s]], axis=0) for s in ks]
        sink = []
        for kh in range(SWA_KV_HEADS):
            sk = jnp.zeros((g * blk, 1), F32)
            for gi in range(g):
                sk = jnp.where((row >= gi * blk) & (row < (gi + 1) * blk), sink_ref[kh * g + gi], sk)
            sink.append(sk)
        s_c, s_b = {}, {}
        for it in items:
            j, kh = it
            qs = jnp.concatenate([q_ref[j * blk:(j + 1) * blk, (kh * g + gi) * SWA_HD:(kh * g + gi + 1) * SWA_HD]
                                  for gi in range(g)], axis=0)
            s_c[it] = _dot_nt(qs, kc_ref[:, ks[kh]])
            if band:
                key_pos = (n * nblk + j - 1) * blk + jj
                keep = in_window & (key_pos >= 0) & (key_pos < seq)
                s_b[it] = jnp.where(keep, _dot_nt(qs, k_slab[kh][j * blk:(j + 3) * blk]), -jnp.inf)
        p_c, p_b, den = {}, {}, {}
        for it in items:
            m = jnp.maximum(jnp.max(s_c[it], axis=-1, keepdims=True), sink[it[1]])
            if band:
                m = jnp.maximum(m, jnp.max(s_b[it], axis=-1, keepdims=True))
            pc = jnp.exp(s_c[it] - m)
            dn = jnp.sum(pc, axis=-1, keepdims=True) + jnp.exp(sink[it[1]] - m)
            if band:
                pb = jnp.exp(s_b[it] - m)
                dn = dn + jnp.sum(pb, axis=-1, keepdims=True)
                p_b[it] = pb.astype(BF16)
            p_c[it] = pc.astype(BF16)
            den[it] = dn
        for it in items:
            j, kh = it
            acc = _dot(p_c[it], vc_ref[:, ks[kh]])
            if band:
                acc = acc + _dot(p_b[it], v_slab[kh][j * blk:(j + 3) * blk])
            o = acc / den[it]
            for gi in range(g):
                c0 = (kh * g + gi) * SWA_HD
                o_ref[j * blk:(j + 1) * blk, c0:c0 + SWA_HD] = o[gi * blk:(gi + 1) * blk].astype(o_ref.dtype)

    if ctx_queries:
        pl.when(n < n_lat_steps)(lambda: attend(True))
        pl.when(n >= n_lat_steps)(lambda: attend(False))
    else:
        attend(True)


def _swa(p_att, sink, batch, seq, ctx_len, ctx_queries):
    blk = SWA_BLOCK
    nblk = SWA_BLOCKS_PER_STEP
    tq = nblk * blk
    assert seq % tq == 0 and ctx_len % tq == 0
    nb = seq // blk
    ns, ncs = seq // tq, ctx_len // tq
    ctx0 = batch * seq // ctx_len
    kcol, vcol = ATT_SK // SWA_KV_W, ATT_SV // SWA_KV_W
    q_row = lambda b, n: jnp.where(n < ns, b * ns + n, batch * ns + b * ncs + (n - ns))
    n_rows = batch * seq + (batch * ctx_len if ctx_queries else 0)

    def edge(col, first):
        def index(b, n):
            k = jnp.clip(n, 0, ns - 1) * nblk + (-1 if first else nblk)
            return (b * nb + jnp.clip(k, 0, nb - 1), col)
        return pl.BlockSpec((blk, SWA_KV_W), index)

    def own(col):
        return pl.BlockSpec((tq, SWA_KV_W), lambda b, n: (b * ns + jnp.clip(n, 0, ns - 1), col))

    ctx_k = pl.BlockSpec((ctx_len, SWA_KV_W), lambda b, n: (ctx0 + b, kcol))
    ctx_v = pl.BlockSpec((ctx_len, SWA_KV_W), lambda b, n: (ctx0 + b, vcol))
    return pl.pallas_call(
        functools.partial(_swa_kernel, n_lat_steps=ns, seq=seq, ctx_queries=ctx_queries),
        grid=(batch, ns + (ncs if ctx_queries else 0)),
        in_specs=[pl.BlockSpec(memory_space=pltpu.SMEM),
                  pl.BlockSpec((tq, SWA_W), lambda b, n: (q_row(b, n), 0)),
                  edge(kcol, True), own(kcol), edge(kcol, False), ctx_k,
                  edge(vcol, True), own(vcol), edge(vcol, False), ctx_v],
        out_specs=pl.BlockSpec((tq, SWA_W), lambda b, n: (q_row(b, n), 0)),
        out_shape=jax.ShapeDtypeStruct((n_rows, SWA_W), BF16),
        compiler_params=_cparams(("arbitrary", "arbitrary")),
        name="swa_attn",
    )(sink, *([p_att] * 9))


def _route(logits, rb):
    aff = _sigmoid(logits)
    sel = aff + rb
    tm = logits.shape[1]
    scores = []
    for g in range(N_GROUPS):
        r = [sel[g * E_PER_GROUP + j:g * E_PER_GROUP + j + 1] for j in range(E_PER_GROUP)]
        best = None
        for i in range(E_PER_GROUP):
            for j in range(i + 1, E_PER_GROUP):
                pair = r[i] + r[j]
                best = pair if best is None else jnp.maximum(best, pair)
        scores.append(best)
    gbest, gi = scores[0], jnp.zeros((1, tm), jnp.int32)
    for g in range(1, N_GROUPS):
        upd = scores[g] > gbest
        gbest = jnp.where(upd, scores[g], gbest)
        gi = jnp.where(upd, g, gi)
    s_in, a_in = [], []
    for j in range(E_PER_GROUP):
        sj, aj = sel[j:j + 1], aff[j:j + 1]
        for g in range(1, N_GROUPS):
            e = g * E_PER_GROUP + j
            sj = jnp.where(gi == g, sel[e:e + 1], sj)
            aj = jnp.where(gi == g, aff[e:e + 1], aj)
        s_in.append(sj)
        a_in.append(aj)
    chosen = []
    for j in range(E_PER_GROUP):
        rank = jnp.zeros((1, tm), jnp.int32)
        for k in range(E_PER_GROUP):
            if k == j:
                continue
            ahead = (s_in[k] >= s_in[j]) if k < j else (s_in[k] > s_in[j])
            rank = rank + ahead.astype(jnp.int32)
        chosen.append(rank < 2)
    w = [jnp.where(chosen[j], a_in[j], 0.0) for j in range(E_PER_GROUP)]
    wsum = w[0] + w[1] + w[2] + w[3]
    gate_in = [wj / wsum * ROUTE_SCALE for wj in w]
    code = sum(jnp.where(chosen[j], 1 << j, 0) for j in range(E_PER_GROUP))
    pair = jnp.zeros((1, tm), jnp.int32)
    for idx, cval in enumerate((3, 5, 9, 6, 10, 12)):
        pair = jnp.where(code == cval, idx, pair)
    bucket = gi * N_PAIRS + pair
    g_lo = jnp.zeros((1, tm), F32)
    g_hi = jnp.zeros((1, tm), F32)
    seen = jnp.zeros((1, tm), jnp.bool_)
    for j in range(E_PER_GROUP):
        g_lo = jnp.where(chosen[j] & ~seen, gate_in[j], g_lo)
        g_hi = jnp.where(chosen[j] & seen, gate_in[j], g_hi)
        seen = seen | chosen[j]
    return g_lo, g_hi, bucket


def _lat_or_ctx(lat_ref, ctx_ref, rs, lat_tiles):
    if ctx_ref is None:
        return lat_ref[rs, :]
    return jnp.where(pl.program_id(0) < lat_tiles, lat_ref[rs, :], ctx_ref[rs, :])


def _dual_specs(rows, width):
    return [pl.BlockSpec((rows.tm, width), lambda i, *_: (jnp.minimum(i, rows.lat_tiles - 1), 0)),
            pl.BlockSpec((rows.tm, width), lambda i, *_: (jnp.maximum(i - rows.lat_tiles, 0), 0),
                         pipeline_mode=pl.Buffered(1))]


def _out_kernel(*refs, alpha, dual, lat_tiles):
    if dual:
        (ohg_ref, omla_ref, omlac_ref, oswa_ref, x_ref, xc_ref, g1_ref, sh2_ref, sc2_ref, lng_ref, lnb_ref,
         w_ref, rwt_ref, rb_ref, x1_ref, h2_ref, bucket_ref) = refs
    else:
        (ohg_ref, omla_ref, oswa_ref, x_ref, g1_ref, sh2_ref, sc2_ref, lng_ref, lnb_ref,
         w_ref, rwt_ref, rb_ref, x1_ref, h2_ref, bucket_ref) = refs
        omlac_ref = xc_ref = None
    o1, o2 = HG_W, HG_W + MLA_HEADS * MLA_V
    d = x_ref.shape[1]
    tm = x_ref.shape[0]
    halves = [slice(0, tm // 2), slice(tm // 2, tm)]
    mixes = [_dot(ohg_ref[rs, :], w_ref[0:o1, :])
             + _dot(_lat_or_ctx(omla_ref, omlac_ref, rs, lat_tiles), w_ref[o1:o2, :])
             + _dot(oswa_ref[rs, :], w_ref[o2:, :]) for rs in halves]
    for rs, mix in zip(halves, mixes):
        x_in = _lat_or_ctx(x_ref, xc_ref, rs, lat_tiles)
        x1 = _ln_rows(alpha * x_in + g1_ref[...] * mix) * lng_ref[...] + lnb_ref[...]
        x1_ref[rs, :] = x1
        h2 = (_ln_rows(x1) * (1.0 + sc2_ref[...]) + sh2_ref[...]).astype(BF16)
        h2_ref[rs, 0:d] = h2.astype(h2_ref.dtype)
        g_lo, g_hi, bucket = _route(_dot_nt(rwt_ref[...], h2), rb_ref[...])
        bucket_ref[:, rs] = bucket
        gate_rows = jnp.concatenate([g_lo, g_hi, jnp.zeros((LANES - 2, tm // 2), F32)], axis=0)
        h2_ref[rs, d:d + LANES] = gate_rows.T


def _out_proj(o_hg, o_mla, o_swa, x, mod, ln_g, ln_b, w_out, rwt, rb, rows, alpha):
    tm = rows.tm
    dual = isinstance(x, tuple)
    d = (x[0] if dual else x).shape[1]
    n = rows.n_rows
    row = lambda w: pl.BlockSpec((tm, w), lambda i: (i, 0))
    whole = lambda a: pl.BlockSpec(a.shape, lambda i: (0, 0), pipeline_mode=pl.Buffered(1))
    mla_w = MLA_HEADS * MLA_V
    if dual:
        in_specs = [row(HG_W)] + _dual_specs(rows, mla_w) + [row(SWA_W)] + _dual_specs(rows, d)
        inputs = [o_hg, *o_mla, o_swa, *x]
    else:
        in_specs = [row(HG_W), row(mla_w), row(SWA_W), row(d)]
        inputs = [o_hg, o_mla, o_swa, x]
    in_specs += [_mod_spec(rows, 2, d), _mod_spec(rows, 3, d), _mod_spec(rows, 4, d),
                 whole(ln_g), whole(ln_b), whole(w_out), whole(rwt), whole(rb)]
    inputs += [mod, mod, mod, ln_g, ln_b, w_out, rwt, rb]
    return pl.pallas_call(
        functools.partial(_out_kernel, alpha=alpha, dual=dual, lat_tiles=rows.lat_tiles), grid=(rows.n_tiles,),
        in_specs=in_specs,
        out_specs=[row(d), row(d + LANES), pl.BlockSpec((1, tm), lambda i: (0, i))],
        out_shape=[jax.ShapeDtypeStruct((n, d), F32), jax.ShapeDtypeStruct((n, d + LANES), F32),
                   jax.ShapeDtypeStruct((1, n), jnp.int32)],
        compiler_params=_cparams(("arbitrary",)), name="out_proj_ln_router",
    )(*inputs)


def _row_gather_start(idx_ref, base, src_hbm, dst, sem, n, static_rows=False):
    def issue(r, carry):
        pltpu.make_async_copy(src_hbm.at[pl.ds(idx_ref[base + r], 1)], dst.at[pl.ds(r, 1)], sem).start()
        return carry
    if static_rows:
        for r in range(n):
            issue(r, 0)
    else:
        lax.fori_loop(0, n, issue, 0, unroll=8)


def _row_gather_start_next(t, n_valid, idx_ref, base, src_hbm, buf, sem, n):
    for half in range(2):
        @pl.when((t < n_valid) & (t % 2 == half))
        def _():
            _row_gather_start(idx_ref, base, src_hbm, buf.at[half], sem.at[half], n, static_rows=True)


def _row_gather_wait(src_hbm, dst, sem, n):
    pltpu.make_async_copy(src_hbm.at[pl.ds(0, n)], dst, sem).wait()


def _moe_kernel(stok_ref, off_ref, e1_ref, e2_ref, nused_ref, h_hbm, wg1_ref, wu1_ref, wd1_ref,
                wg2_ref, wu2_ref, wd2_ref, y_ref, hbuf, sem, *, tm):
    i = pl.program_id(0)
    n_used = nused_ref[0]
    d = y_ref.shape[1]

    @pl.when(i == 0)
    def _():
        _row_gather_start(stok_ref, off_ref[0], h_hbm, hbuf.at[0], sem.at[0], tm)

    nxt = jnp.minimum(i + 1, pl.num_programs(0) - 1)
    _row_gather_start_next(i + 1, n_used, stok_ref, off_ref[nxt], h_hbm, hbuf, sem, tm)

    @pl.when(i < n_used)
    def _():
        slot = i % 2
        _row_gather_wait(h_hbm, hbuf.at[slot], sem.at[slot], tm)
        h = hbuf[slot, :, 0:d].astype(BF16)
        g_lo = hbuf[slot, :, d:d + 1]
        g_hi = hbuf[slot, :, d + 1:d + 2]

        act1 = (_silu(_dot(h, wg1_ref[...])) * _dot(h, wu1_ref[...])).astype(BF16)
        act2 = (_silu(_dot(h, wg2_ref[...])) * _dot(h, wu2_ref[...])).astype(BF16)
        y_ref[...] = g_lo * _dot(act1, wd1_ref[...]) + g_hi * _dot(act2, wd2_ref[...])

    @pl.when(i >= n_used)
    def _():
        y_ref[...] = jnp.zeros_like(y_ref)


def _moe(h2, stok, off, e1, e2, n_used, wg, wu, wd, layer, tm):
    n_tiles = off.shape[0]
    dx = h2.shape[1]
    d, f = wg.shape[2], wg.shape[3]
    last = lambda i, nu: jnp.minimum(i, nu[0] - 1)
    wspec_in = lambda sel: pl.BlockSpec(
        (None, None, d, f), lambda i, s, o, a, b, nu: (layer, (a, b)[sel][last(i, nu)], 0, 0))
    wspec_dn = lambda sel: pl.BlockSpec(
        (None, None, f, d), lambda i, s, o, a, b, nu: (layer, (a, b)[sel][last(i, nu)], 0, 0))
    grid_spec = pltpu.PrefetchScalarGridSpec(
        num_scalar_prefetch=5, grid=(n_tiles,),
        in_specs=[pl.BlockSpec(memory_space=pl.ANY),
                  wspec_in(0), wspec_in(0), wspec_dn(0), wspec_in(1), wspec_in(1), wspec_dn(1)],
        out_specs=pl.BlockSpec((tm, d), lambda i, s, o, a, b, nu: (i, 0)),
        scratch_shapes=[pltpu.VMEM((2, tm, dx), F32), pltpu.SemaphoreType.DMA((2,))])
    return pl.pallas_call(
        functools.partial(_moe_kernel, tm=tm), grid_spec=grid_spec,
        out_shape=jax.ShapeDtypeStruct((n_tiles * tm, d), F32),
        compiler_params=_cparams(("arbitrary",)), name="moe_grouped",
    )(stok, off, e1, e2, n_used, h2, wg, wu, wd, wg, wu, wd)


def _moe_plan(bucket, n_tokens, tm):
    tok = jnp.arange(n_tokens, dtype=jnp.int32)
    buckets = jnp.arange(N_BUCKETS, dtype=jnp.int32)
    skey, stok = lax.sort((bucket * n_tokens + tok, tok), num_keys=1)
    counts = jnp.sum((bucket[:, None] == buckets[None, :]).astype(jnp.int32), axis=0)
    padded = (counts + tm - 1) // tm * tm
    ends = jnp.cumsum(padded)
    shift = (ends - padded) - (jnp.cumsum(counts) - counts)
    sbucket = skey // n_tokens
    slot = tok + jnp.sum(jnp.where(sbucket[:, None] == buckets[None, :], shift[None, :], 0), axis=1)
    _, pos = lax.sort((stok, slot), num_keys=1)
    n_tiles = n_tokens // tm + N_BUCKETS
    tile_start = jnp.arange(n_tiles, dtype=jnp.int32) * tm
    tile_bucket = jnp.sum((ends[None, :] <= tile_start[:, None]).astype(jnp.int32), axis=1)
    tile_bucket = jnp.minimum(tile_bucket, N_BUCKETS - 1)
    tshift = jnp.sum(jnp.where(tile_bucket[:, None] == buckets[None, :], shift[None, :], 0), axis=1)
    off = jnp.clip(tile_start - tshift, 0, n_tokens)
    grp, pair = tile_bucket // N_PAIRS, tile_bucket % N_PAIRS
    lo = jnp.where(pair < 3, 0, jnp.where(pair < 5, 1, 2))
    hi = jnp.where(pair == 0, 1, jnp.where((pair == 1) | (pair == 3), 2, 3))
    e1, e2 = grp * E_PER_GROUP + lo, grp * E_PER_GROUP + hi
    n_used = (ends[-1] // tm).astype(jnp.int32).reshape(1)
    stok = jnp.concatenate([stok, jnp.zeros((tm,), jnp.int32)])
    return stok, off.astype(jnp.int32), pos.astype(jnp.int32), e1.astype(jnp.int32), e2.astype(jnp.int32), n_used


def _ln2_kernel(pos_ref, y_hbm, x1_ref, g2_ref, lng_ref, lnb_ref, *rest, alpha, tm, emit_h):
    if emit_h:
        sh_ref, sc_ref, x2_ref, h_ref, ybuf, sem = rest
    else:
        x2_ref, ybuf, sem = rest
    i = pl.program_id(0)
    n_tiles = pl.num_programs(0)

    @pl.when(i == 0)
    def _():
        _row_gather_start(pos_ref, 0, y_hbm, ybuf.at[0], sem.at[0], tm)

    _row_gather_start_next(i + 1, n_tiles, pos_ref, (i + 1) * tm, y_hbm, ybuf, sem, tm)

    slot = i % 2
    _row_gather_wait(y_hbm, ybuf.at[slot], sem.at[slot], tm)
    x2 = _ln_rows(alpha * x1_ref[...] + g2_ref[...] * ybuf[slot]) * lng_ref[...] + lnb_ref[...]
    x2_ref[...] = x2
    if emit_h:
        h_ref[...] = (_ln_rows(x2) * (1.0 + sc_ref[...]) + sh_ref[...]).astype(h_ref.dtype)


def _ln2(pos, y_sorted, x1, mod, mod_next, ln_g, ln_b, rows, alpha):
    tm = rows.tm
    d = x1.shape[1]
    emit_h = mod_next is not None
    row = pl.BlockSpec((tm, d), lambda i, p: (i, 0))
    whole = lambda a: pl.BlockSpec(a.shape, lambda i, p: (0, 0))
    in_specs = [pl.BlockSpec(memory_space=pl.ANY), row, _mod_spec(rows, 5, d), whole(ln_g), whole(ln_b)]
    inputs = [y_sorted, x1, mod, ln_g, ln_b]
    out_specs = [row]
    out_shape = [jax.ShapeDtypeStruct((rows.n_rows, d), F32)]
    if emit_h:
        in_specs += [_mod_spec(rows, 0, d), _mod_spec(rows, 1, d)]
        inputs += [mod_next, mod_next]
        out_specs.append(row)
        out_shape.append(jax.ShapeDtypeStruct((rows.n_rows, d), BF16))
    grid_spec = pltpu.PrefetchScalarGridSpec(
        num_scalar_prefetch=1, grid=(rows.n_tiles,), in_specs=in_specs, out_specs=out_specs,
        scratch_shapes=[pltpu.VMEM((2, tm, d), F32), pltpu.SemaphoreType.DMA((2,))])
    out = pl.pallas_call(
        functools.partial(_ln2_kernel, alpha=alpha, tm=tm, emit_h=emit_h), grid_spec=grid_spec,
        out_shape=out_shape, compiler_params=_cparams(("arbitrary",)), name="unpermute_ln2",
    )(pos, *inputs)
    return out if emit_h else (out[0], None)


def _rope_tables(seq, batch, ctx_rows, dim, pad_to):
    rows = seq // GRID_W
    row = jnp.repeat(jnp.arange(rows, dtype=jnp.int32), GRID_W)
    col = jnp.tile(jnp.arange(GRID_W, dtype=jnp.int32), rows)
    nf = dim // 4
    inv_freq = ROPE_BASE ** (-jnp.arange(nf, dtype=F32) / nf)
    ang = jnp.stack([row, col], -1).astype(F32)[:, :, None] * inv_freq
    cos, sin = jnp.cos(ang), jnp.sin(ang)
    c = jnp.stack([cos, cos], axis=2).reshape(seq, dim)
    s = jnp.stack([-sin, sin], axis=2).reshape(seq, dim)
    if pad_to > dim:
        c = jnp.pad(c, ((0, 0), (0, pad_to - dim)))
        s = jnp.pad(s, ((0, 0), (0, pad_to - dim)))
    ctx_c = jnp.zeros((ctx_rows, pad_to), F32).at[:, :dim].set(1.0)
    c = jnp.concatenate([jnp.tile(c, (batch, 1)), ctx_c], axis=0)
    s = jnp.concatenate([jnp.tile(s, (batch, 1)), jnp.zeros((ctx_rows, pad_to), F32)], axis=0)
    return c, s


def _w_in_prep_kernel(w_ref, raw_ref, gate_ref, att_ref):
    o1 = 5 * HG_W
    o2 = o1 + MLA_Q_LORA + MLA_KV_LORA
    o3 = o2 + MLA_ROPE
    cast = lambda a, b: w_ref[:, a:b].astype(BF16)
    raw_ref[:, 0:HG_W] = cast(0, HG_W)
    raw_ref[:, HG_W:3 * HG_W] = cast(3 * HG_W, o1)
    gate_ref[...] = cast(HG_W, 3 * HG_W)
    att_ref[:, ATT_SQ:ATT_CQ] = cast(o3, o3 + SWA_W + 2 * SWA_KV_W)
    att_ref[:, ATT_CQ:ATT_KR] = cast(o1, o2)
    att_ref[:, ATT_KR:ATT_KR + MLA_ROPE] = cast(o2, o3)
    att_ref[:, ATT_KR + MLA_ROPE:ATT_COLS] = jnp.zeros((w_ref.shape[0], LANES - MLA_ROPE), BF16)


def _w_in_prep(w_in):
    depth, d, n = w_in.shape
    tr = 256 if d % 256 == 0 else d
    widths = (3 * HG_W, 2 * HG_W, ATT_COLS)
    return pl.pallas_call(
        _w_in_prep_kernel, grid=(depth, d // tr),
        in_specs=[pl.BlockSpec((None, tr, n), lambda l, i: (l, i, 0))],
        out_specs=[pl.BlockSpec((None, tr, w), lambda l, i: (l, i, 0)) for w in widths],
        out_shape=[jax.ShapeDtypeStruct((depth, d, w), BF16) for w in widths],
        compiler_params=_cparams(("arbitrary", "arbitrary")), name="w_in_prep",
    )(w_in)


def _mla_weights(w_uq, w_ukv):
    qk = MLA_NOPE + MLA_ROPE
    wq = w_uq.reshape(MLA_Q_LORA, MLA_HEADS, qk)
    wq = jnp.pad(wq, ((0, 0), (0, 0), (0, MLA_QK_PAD - qk))).reshape(MLA_Q_LORA, MLA_HEADS * MLA_QK_PAD)
    wkv = w_ukv.reshape(MLA_KV_LORA, MLA_HEADS, MLA_NOPE + MLA_V)
    wkv = jnp.concatenate([wkv[:, :, :MLA_NOPE].reshape(MLA_KV_LORA, -1), wkv[:, :, MLA_NOPE:].reshape(MLA_KV_LORA, -1)], axis=1)
    return wq.astype(BF16), wkv.astype(BF16)


def kernel(x, c, ctx, c_ctx, w_ada, b_ada, w_in, w_out, hg_lb_logits, hg_norm_g, mla_q_norm, mla_kv_norm,
           mla_w_uq, mla_w_ukv, swa_sink, ln1_g, ln1_b, ln2_g, ln2_b, router_w, router_b,
           moe_w_gate, moe_w_up, moe_w_down):
    batch, seq, d = x.shape
    ctx_len = ctx.shape[1]
    depth = w_ada.shape[0]
    alpha = (2.0 * depth) ** 0.25
    n_lat, n_ctx = batch * seq, batch * ctx_len
    n_all = n_lat + n_ctx
    assert batch + 1 <= 8 and ctx_len % (4 * HG_CHUNK) == 0 and seq % ctx_len == 0 and seq % GRID_W == 0

    c8 = jnp.zeros((8, d), F32).at[:batch].set(c).at[batch].set(c_ctx)
    mod_all = _ada(c8, w_ada, b_ada).reshape(depth, 8 * 6, 1, d)

    lb = jnp.cumsum(jax.nn.softmax(hg_lb_logits.astype(F32), axis=0), axis=0)
    lb = (lb - lb[0:1]).reshape(depth, 1, 2 * HG_W)
    log_lb, log_1m, one_m = jnp.log(lb), jnp.log1p(-lb), 1.0 - lb

    cs, ss = _rope_tables(seq, batch, n_ctx, SWA_HD, SWA_HD)
    cm, sm = _rope_tables(seq, batch, n_ctx, MLA_ROPE, LANES)
    rwt = router_w.T.astype(BF16)
    rb = router_b.astype(F32).reshape(N_EXPERTS, 1)

    tm_all = _pick_tm(seq, n_ctx, 512)
    rows_all = _Rows(batch, seq, ctx_len, n_all, tm_all)
    rows_lat = _Rows(batch, seq, ctx_len, n_lat, tm_all)
    moe_tm = 256

    xa = (x.reshape(n_lat, d), ctx.reshape(n_ctx, d))
    h = _lnmod(xa[0], xa[1], mod_all[0], rows_all, 0)
    w_raw_all, w_gate_all, w_att_all = _w_in_prep(w_in)
    wg_all, wu_all, wd_all = moe_w_gate.astype(BF16), moe_w_up.astype(BF16), moe_w_down.astype(BF16)

    for layer in range(depth):
        need_ctx = layer < depth - 1
        mod = mod_all[layer]
        rows = rows_all if need_ctx else rows_lat
        wq, wkv = _mla_weights(mla_w_uq[layer], mla_w_ukv[layer])

        lf, kk, p_att, p_raw = _in_proj(h, w_gate_all, w_att_all, w_raw_all, layer, log_lb[layer], log_1m[layer],
                                        one_m[layer], [cs, ss, cm, sm], rows_all)

        o_f = _hg_scan(p_raw, kk, lf, batch, seq, ctx_len, False)
        o_hg = _hg_scan(p_raw, kk, lf, batch, seq, ctx_len, True,
                        (o_f, hg_norm_g[layer].astype(F32).reshape(1, HG_DK)))

        q_mla, k_mla, v_mla = _mla_proj(p_att, mla_q_norm[layer].astype(F32).reshape(1, -1),
                                        mla_kv_norm[layer].astype(F32).reshape(1, -1), wq, wkv, cm, sm, rows_all)
        n_out = rows.n_rows
        o_mla = _mla_attn(q_mla, k_mla, v_mla, batch, seq, ctx_len, True)
        o_swa = _swa(p_att, swa_sink[layer].astype(F32), batch, seq, ctx_len, need_ctx)
        dual = isinstance(xa, tuple)
        if need_ctx:
            o_mla = (o_mla, _mla_attn(q_mla, k_mla, v_mla, batch, seq, ctx_len, False))
            if not dual:
                xa = (xa[:n_lat], xa[n_lat:])
        elif dual:
            xa = xa[0]

        x1, h2, bucket = _out_proj(
            o_hg, o_mla, o_swa, xa, mod, ln1_g[layer].astype(F32).reshape(1, d), ln1_b[layer].astype(F32).reshape(1, d),
            w_out[layer].astype(BF16), rwt, rb, rows, alpha)

        stok, off, pos, e1, e2, n_used = _moe_plan(bucket[0], n_out, moe_tm)
        y_sorted = _moe(h2, stok, off, e1, e2, n_used, wg_all, wu_all, wd_all, layer, moe_tm)
        xa, h = _ln2(pos, y_sorted, x1, mod, mod_all[layer + 1] if need_ctx else None,
                     ln2_g[layer].astype(F32).reshape(1, d), ln2_b[layer].astype(F32).reshape(1, d), rows, alpha)

    return xa[:n_lat].reshape(batch, seq, d)
```

```python
import functools
import math

import jax
import jax.numpy as jnp
from jax import lax
from jax.experimental import pallas as pl
from jax.experimental.pallas import tpu as pltpu

F32 = jnp.float32
BF16 = jnp.bfloat16

GRID_W = 64
HG_HEADS = 4
HG_DK = 128
HG_W = HG_HEADS * HG_DK
HG_CHUNK = 64
HG_SUB = 16
MLA_HEADS = 8
MLA_Q_LORA = 512
MLA_KV_LORA = 256
MLA_NOPE = 128
MLA_ROPE = 64
MLA_V = 128
MLA_QK_PAD = 256
MLA_V_PAD = 256
MLA_HEADS_PER_STEP = 4
MLA_TQ = 1024
SWA_HEADS = 4
SWA_KV_HEADS = 2
SWA_HD = 128
SWA_W = SWA_HEADS * SWA_HD
SWA_KV_W = SWA_KV_HEADS * SWA_HD
SWA_WINDOW = 128
SWA_BLOCK = 128
SWA_BLOCKS_PER_STEP = 2
N_EXPERTS = 16
N_GROUPS = 4
E_PER_GROUP = 4
N_PAIRS = 6
N_BUCKETS = N_GROUPS * N_PAIRS
ROUTE_SCALE = 2.5
ROPE_BASE = 10000.0
LN_EPS = 1e-5
RMS_EPS = 1e-6
LANES = 128
VMEM_LIMIT = 56 * 1024 * 1024

ATT_SQ, ATT_SK, ATT_SV = 0, SWA_W, SWA_W + SWA_KV_W
ATT_CQ = SWA_W + 2 * SWA_KV_W
ATT_CKV = ATT_CQ + MLA_Q_LORA
ATT_KR = ATT_CKV + MLA_KV_LORA
ATT_COLS = ATT_KR + LANES


def _cparams(sem):
    return pltpu.CompilerParams(dimension_semantics=sem, vmem_limit_bytes=VMEM_LIMIT)


def _dot(a, b):
    return jnp.dot(a, b, preferred_element_type=F32)


def _dot_nt(a, b):
    return lax.dot_general(a, b, (((1,), (1,)), ((), ())), preferred_element_type=F32)


def _dot_tn(a, b):
    return lax.dot_general(a, b, (((0,), (0,)), ((), ())), preferred_element_type=F32)


def _sigmoid(x):
    return 1.0 / (1.0 + jnp.exp(-x))


def _silu(x):
    return x * _sigmoid(x)


def _ln_rows(x):
    mu = jnp.mean(x, axis=-1, keepdims=True)
    xc = x - mu
    var = jnp.mean(xc * xc, axis=-1, keepdims=True)
    return xc * lax.rsqrt(var + LN_EPS)


def _rope_partner(x, half):
    lane = lax.broadcasted_iota(jnp.int32, x.shape, x.ndim - 1)
    first = (lane & half) == 0
    n = x.shape[-1]
    return jnp.where(first, pltpu.roll(x, n - half, x.ndim - 1), pltpu.roll(x, half, x.ndim - 1))


def _ada_kernel(c_ref, w_ref, b_ref, o_ref):
    s = _silu(c_ref[...])
    o_ref[...] = _dot(s.astype(BF16), w_ref[...].astype(BF16)) + b_ref[...]


def _ada(c8, w_ada, b_ada):
    depth, d, n = w_ada.shape
    tn = 1024 if n % 1024 == 0 else n
    return pl.pallas_call(
        _ada_kernel,
        grid=(depth, n // tn),
        in_specs=[pl.BlockSpec((8, d), lambda l, j: (0, 0)),
                  pl.BlockSpec((None, d, tn), lambda l, j: (l, 0, j)),
                  pl.BlockSpec((None, 1, tn), lambda l, j: (l, 0, j))],
        out_specs=pl.BlockSpec((None, 8, tn), lambda l, j: (l, 0, j)),
        out_shape=jax.ShapeDtypeStruct((depth, 8, n), F32),
        compiler_params=_cparams(("arbitrary", "arbitrary")),
        name="ada_mod",
    )(c8, w_ada, b_ada.reshape(depth, 1, n))


class _Rows:
    def __init__(self, batch, seq, ctx_len, n_rows, tm):
        self.batch, self.seq, self.ctx_len, self.n_rows, self.tm = batch, seq, ctx_len, n_rows, tm
        self.n_tiles = n_rows // tm
        self.lat_tiles = batch * seq // tm
        self.tiles_per_batch = seq // tm

    def mod_row(self, i):
        return jnp.where(i < self.lat_tiles, i // self.tiles_per_batch, self.batch)


def _pick_tm(seq, ctx_rows, cap):
    for tm in (1024, 512, 256, 128):
        if tm <= cap and seq % tm == 0 and ctx_rows % tm == 0:
            return tm
    raise ValueError("unsupported sequence / context lengths")


def _mod_spec(rows, chunk, d):
    return pl.BlockSpec((None, 1, d), lambda i, *_: (rows.mod_row(i) * 6 + chunk, 0, 0))


def _lnmod_kernel(x_ref, xc_ref, sh_ref, sc_ref, h_ref, *, lat_tiles):
    y = _ln_rows(_lat_or_ctx(x_ref, xc_ref, slice(None), lat_tiles))
    h_ref[...] = (y * (1.0 + sc_ref[...]) + sh_ref[...]).astype(h_ref.dtype)


def _lnmod(x_lat, x_ctx, mod, rows, chunk0):
    d = x_lat.shape[1]
    tm = rows.tm
    return pl.pallas_call(
        functools.partial(_lnmod_kernel, lat_tiles=rows.lat_tiles),
        grid=(rows.n_tiles,),
        in_specs=_dual_specs(rows, d) + [_mod_spec(rows, chunk0, d), _mod_spec(rows, chunk0 + 1, d)],
        out_specs=pl.BlockSpec((tm, d), lambda i: (i, 0)),
        out_shape=jax.ShapeDtypeStruct((rows.n_rows, d), BF16),
        compiler_params=_cparams(("arbitrary",)),
        name="ln_modulate",
    )(x_lat, x_ctx, mod, mod)


def _col_group_dots(h, w_ref, width):
    n = w_ref.shape[1]
    groups = [slice(c0, min(c0 + width, n)) for c0 in range(0, n, width)]
    return groups, [_dot(h, w_ref[:, g]) for g in groups]


def _in_proj_kernel(h_ref, wgate_ref, watt_ref, wraw_ref, loglb_ref, log1m_ref, onem_ref,
                    cs_ref, ss_ref, cm_ref, sm_ref, lf_ref, k_ref, att_ref, raw_ref):
    h = h_ref[...]
    g_groups, zs = _col_group_dots(h, wgate_ref, 2 * LANES)
    a_groups, ps = _col_group_dots(h, watt_ref, 2 * LANES)
    r_groups, rs = _col_group_dots(h, wraw_ref, 2 * LANES)
    for g, z in zip(g_groups, zs):
        e = jnp.exp(-jnp.abs(z))
        r = 1.0 / (1.0 + e)
        log_sig = jnp.minimum(z, 0.0) + jnp.log(r)
        sig_neg = jnp.where(z >= 0.0, e * r, r)
        a = loglb_ref[:, g]
        b = log1m_ref[:, g] + log_sig
        lf_ref[:, g] = jnp.maximum(a, b) + jnp.log(1.0 + jnp.exp(-jnp.abs(a - b)))
        k_ref[:, g] = onem_ref[:, g] * sig_neg
    cs, ss = cs_ref[...], ss_ref[...]
    swa_scale = SWA_HD ** -0.5
    for g, p in zip(a_groups, ps):
        for c0 in range(g.start, g.stop, LANES):
            x = p[:, c0 - g.start:c0 - g.start + LANES]
            if c0 < ATT_SV:
                x = x * cs + _rope_partner(x, SWA_HD // 4) * ss
                if c0 < ATT_SK:
                    x = x * swa_scale
            elif c0 == ATT_KR:
                x = x * cm_ref[...] + _rope_partner(x, MLA_ROPE // 4) * sm_ref[...]
            att_ref[:, c0:c0 + LANES] = x.astype(att_ref.dtype)
    for g, r in zip(r_groups, rs):
        raw_ref[:, g] = r


def _in_proj(h, w_gate_all, w_att_all, w_raw_all, layer, log_lb, log_1m, one_m, tables, rows):
    tm = rows.tm
    d = h.shape[1]
    n = rows.n_rows
    single = lambda spec: pl.BlockSpec(spec.block_shape, spec.index_map, pipeline_mode=pl.Buffered(1))
    consts = [(w_gate_all, layer), (w_att_all, layer), (w_raw_all, layer), log_lb, log_1m, one_m]
    row = lambda w: pl.BlockSpec((tm, w), lambda i: (i, 0))
    widths = (2 * HG_W, 2 * HG_W, ATT_COLS, 3 * HG_W)
    dtypes = (F32, F32, BF16, F32)
    return pl.pallas_call(
        _in_proj_kernel, grid=(rows.n_tiles,),
        in_specs=[row(d)] + [single(_const_spec(a)) for a in consts] + [row(LANES)] * 4,
        out_specs=[row(w) for w in widths],
        out_shape=[jax.ShapeDtypeStruct((n, w), dt) for w, dt in zip(widths, dtypes)],
        compiler_params=_cparams(("arbitrary",)), name="in_proj",
    )(h, w_gate_all, w_att_all, w_raw_all, log_lb, log_1m, one_m, *tables)


def _const_spec(a):
    if isinstance(a, tuple):
        stacked, layer = a
        return pl.BlockSpec((None,) + stacked.shape[1:], lambda i, *_: (layer,) + (0,) * (stacked.ndim - 1))
    return pl.BlockSpec(a.shape, lambda i, *_: (0,) * a.ndim)


def _hg_scan_kernel(*refs, reverse, n_chunks, readout):
    if readout:
        q_ref, k_ref, lf_ref, v_ref, of_ref, g_ref, ng_ref, o_ref, st_ref = refs
    else:
        q_ref, k_ref, lf_ref, v_ref, o_ref, st_ref = refs

    @pl.when(pl.program_id(1) == 0)
    def _():
        st_ref[...] = jnp.zeros_like(st_ref)

    c, s = HG_CHUNK, HG_SUB
    nsub = c // s
    chunks = [n_chunks - 1 - cc if reverse else cc for cc in range(n_chunks)]
    items = [(ch, hd) for ch in chunks for hd in range(HG_HEADS)]
    rows = lambda ch: slice(ch * c, (ch + 1) * c)
    cols = lambda hd: slice(hd * HG_DK, (hd + 1) * HG_DK)

    ri = lax.broadcasted_iota(jnp.int32, (c, c), 0)
    ci = lax.broadcasted_iota(jnp.int32, (c, c), 1)
    tri = jnp.where((ci >= ri) if reverse else (ci <= ri), 1.0, 0.0).astype(BF16)
    b_all = {}
    for ch in chunks:
        lf = lf_ref[rows(ch), :]
        hi = lf.astype(BF16)
        r1 = lf - hi.astype(F32)
        mid = r1.astype(BF16)
        lo = (r1 - mid.astype(F32)).astype(BF16)
        b_all[ch] = _dot(tri, hi) + _dot(tri, mid) + _dot(tri, lo)

    qe, k_dec, decay, v16, qt, kt = {}, {}, {}, {}, {}, {}
    for it in items:
        ch, hd = it
        b = b_all[ch][:, cols(hd)]
        q, k = q_ref[rows(ch), cols(hd)], k_ref[rows(ch), cols(hd)]
        btot = b[0:1, :] if reverse else b[c - 1:c, :]
        qe[it] = (q * jnp.exp(b)).astype(BF16)
        k_dec[it] = (k * jnp.exp(btot - b)).astype(BF16)
        decay[it] = jnp.exp(btot)
        v16[it] = v_ref[rows(ch), cols(hd)].astype(BF16)
        for i in range(nsub):
            r0 = i * s
            if reverse:
                k0, k1 = r0, c
                ref = b[r0 + s:r0 + s + 1, :] if i < nsub - 1 else jnp.zeros_like(btot)
            else:
                k0, k1 = 0, r0 + s
                ref = b[r0 - 1:r0, :] if i > 0 else jnp.zeros_like(btot)
            qt[it, i] = (q[r0:r0 + s] * jnp.exp(b[r0:r0 + s] - ref)).astype(BF16)
            kt[it, i] = (k[k0:k1] * jnp.exp(ref - b[k0:k1])).astype(BF16)

    upd = {it: _dot_tn(v16[it], k_dec[it]) for it in items}
    att = {}
    for it in items:
        for i in range(nsub):
            r0 = i * s
            k0 = r0 if reverse else 0
            a = _dot_nt(qt[it, i], kt[it, i])
            rr = lax.broadcasted_iota(jnp.int32, a.shape, 0) + r0
            cc = lax.broadcasted_iota(jnp.int32, a.shape, 1) + k0
            att[it, i] = jnp.where((cc >= rr) if reverse else (cc <= rr), a, 0.0).astype(BF16)

    states = [st_ref[hd] for hd in range(HG_HEADS)]
    o_inter = {}
    for it in items:
        ch, hd = it
        o_inter[it] = _dot_nt(qe[it], states[hd].astype(BF16))
        states[hd] = states[hd] * decay[it] + upd[it]
    for hd in range(HG_HEADS):
        st_ref[hd] = states[hd]

    for it in items:
        ch, hd = it
        outs = []
        for i in range(nsub):
            r0 = i * s
            k0, k1 = (r0, c) if reverse else (0, r0 + s)
            outs.append(o_inter[it][r0:r0 + s] + _dot(att[it, i], v16[it][k0:k1]))
        o = jnp.concatenate(outs, axis=0)
        if readout:
            o = o + of_ref[rows(ch), cols(hd)]
            o = o * lax.rsqrt(jnp.mean(o * o, axis=-1, keepdims=True) + RMS_EPS) * ng_ref[...]
            o = o * _silu(g_ref[rows(ch), cols(hd)])
        o_ref[rows(ch), cols(hd)] = o.astype(o_ref.dtype)


def _hg_scan(p_raw, kk, lf, batch, seq, ctx_len, reverse, readout_args=None):
    n_rows = p_raw.shape[0]
    tb = ctx_len
    nl = seq // tb
    ctx0 = batch * seq // tb
    direction = 1 if reverse else 0

    def row_block(b, i):
        lat = b * nl + (nl - i if reverse else i - 1)
        return jnp.where(i == 0, ctx0 + b, lat)

    def col(j):
        return pl.BlockSpec((tb, HG_W), lambda b, i: (row_block(b, i), j))

    spec = col(0)
    readout = readout_args is not None
    inputs = [p_raw, kk, lf, p_raw]
    in_specs = [col(0), col(direction), col(direction), col(1)]
    if readout:
        o_f, norm_g = readout_args
        inputs += [o_f, p_raw, norm_g]
        in_specs += [spec, col(2), pl.BlockSpec((1, HG_DK), lambda b, i: (0, 0))]
    kern = functools.partial(_hg_scan_kernel, reverse=reverse, n_chunks=tb // HG_CHUNK, readout=readout)
    return pl.pallas_call(
        kern, grid=(batch, nl + 1), in_specs=in_specs, out_specs=spec,
        out_shape=jax.ShapeDtypeStruct((n_rows, HG_W), BF16 if readout else F32),
        scratch_shapes=[pltpu.VMEM((HG_HEADS, HG_DK, HG_DK), F32)],
        compiler_params=_cparams(("arbitrary", "arbitrary")),
        name="hgrn2_scan_bwd_readout" if readout else "hgrn2_scan_fwd",
    )(*inputs)


def _mla_proj_kernel(cq_ref, ckv_ref, kr_ref, qn_ref, kvn_ref, wq_ref, wkv_ref, cm_ref, sm_ref,
                     q_ref, k_ref, v_ref):
    def rms(x, g):
        xf = x.astype(F32)
        return (xf * lax.rsqrt(jnp.mean(xf * xf, axis=-1, keepdims=True) + RMS_EPS) * g).astype(BF16)

    scale = (MLA_NOPE + MLA_ROPE) ** -0.5 * math.log2(math.e)
    _, qs = _col_group_dots(rms(cq_ref[...], qn_ref[...]), wq_ref, MLA_QK_PAD)
    _, kv2 = _col_group_dots(rms(ckv_ref[...], kvn_ref[...]), wkv_ref, 2 * LANES)
    kvs = [g[:, half * LANES:(half + 1) * LANES] for g in kv2 for half in range(2)]
    cm, sm = cm_ref[...], sm_ref[...]
    kr = kr_ref[...]
    lane = lax.broadcasted_iota(jnp.int32, (kr.shape[0], LANES), 1)
    ones_col = jnp.where(lane == 0, 1.0, 0.0).astype(v_ref.dtype)
    for hd in range(MLA_HEADS):
        c0 = hd * MLA_QK_PAD
        q_ref[:, c0:c0 + LANES] = (qs[hd][:, 0:LANES] * scale).astype(q_ref.dtype)
        x = qs[hd][:, LANES:2 * LANES]
        y = (x * cm + _rope_partner(x, MLA_ROPE // 4) * sm) * scale
        q_ref[:, c0 + LANES:c0 + 2 * LANES] = y.astype(q_ref.dtype)
        k_ref[:, c0:c0 + LANES] = kvs[hd].astype(k_ref.dtype)
        k_ref[:, c0 + LANES:c0 + 2 * LANES] = kr
        v0 = hd * MLA_V_PAD
        v_ref[:, v0:v0 + MLA_V] = kvs[MLA_HEADS + hd].astype(v_ref.dtype)
        v_ref[:, v0 + MLA_V:v0 + MLA_V_PAD] = ones_col


def _mla_proj(p_att, q_norm, kv_norm, wq, wkv, cm, sm, rows):
    tm = rows.tm
    n = rows.n_rows
    hq = MLA_HEADS * MLA_QK_PAD

    def col(width, off):
        return pl.BlockSpec((tm, width), lambda i: (i, off // width))

    def whole(a):
        return pl.BlockSpec(a.shape, lambda i: (0, 0))

    return pl.pallas_call(
        _mla_proj_kernel, grid=(rows.n_tiles,),
        in_specs=[col(MLA_Q_LORA, ATT_CQ), col(MLA_KV_LORA, ATT_CKV), col(LANES, ATT_KR),
                  whole(q_norm), whole(kv_norm), whole(wq), whole(wkv),
                  pl.BlockSpec((tm, LANES), lambda i: (i, 0)), pl.BlockSpec((tm, LANES), lambda i: (i, 0))],
        out_specs=[pl.BlockSpec((tm, hq), lambda i: (i, 0)), pl.BlockSpec((tm, hq), lambda i: (i, 0)),
                   pl.BlockSpec((tm, MLA_HEADS * MLA_V_PAD), lambda i: (i, 0))],
        out_shape=[jax.ShapeDtypeStruct((n, hq), BF16), jax.ShapeDtypeStruct((n, hq), BF16),
                   jax.ShapeDtypeStruct((n, MLA_HEADS * MLA_V_PAD), BF16)],
        compiler_params=_cparams(("arbitrary",)), name="mla_proj",
    )(p_att, p_att, p_att, q_norm, kv_norm, wq, wkv, cm, sm)


def _lane_tile_fold(x, op):
    out = x[:, 0:LANES]
    for t in range(1, x.shape[1] // LANES):
        out = op(out, x[:, t * LANES:(t + 1) * LANES])
    return out


def _mla_attn_kernel(*refs, ck, with_lat):
    if with_lat:
        q_ref, kl_ref, vl_ref, kc_ref, vc_ref, o_ref = refs
    else:
        q_ref, kc_ref, vc_ref, o_ref = refs
    chunks = [(kc_ref, vc_ref, 0, kc_ref.shape[0])]
    if with_lat:
        chunks += [(kl_ref, vl_ref, c * ck, ck) for c in range(kl_ref.shape[0] // ck)]
    m = [None] * MLA_HEADS_PER_STEP
    acc = [None] * MLA_HEADS_PER_STEP
    for k_ref, v_ref, r0, n in chunks:
        for hd in range(MLA_HEADS_PER_STEP):
            qk = slice(hd * MLA_QK_PAD, (hd + 1) * MLA_QK_PAD)
            s = _dot_nt(q_ref[:, qk], k_ref[r0:r0 + n, qk])
            m_c = jnp.max(_lane_tile_fold(s, jnp.maximum), axis=-1, keepdims=True)
            m_new = m_c if m[hd] is None else jnp.maximum(m[hd], m_c)
            pv = _dot(jnp.exp2(s - m_new).astype(BF16), v_ref[r0:r0 + n, hd * MLA_V_PAD:(hd + 1) * MLA_V_PAD])
            acc[hd] = pv if m[hd] is None else acc[hd] * jnp.exp2(m[hd] - m_new) + pv
            m[hd] = m_new
    for hd in range(MLA_HEADS_PER_STEP):
        o = acc[hd][:, 0:MLA_V] / acc[hd][:, MLA_V:MLA_V + 1]
        o_ref[:, hd * MLA_V:(hd + 1) * MLA_V] = o.astype(o_ref.dtype)


def _mla_attn(q, k, v, batch, seq, ctx_len, latent):
    ctx0 = batch * seq // ctx_len
    ck = 1024 if seq % 1024 == 0 else ctx_len
    hps = MLA_HEADS_PER_STEP
    if latent:
        tq = MLA_TQ if seq % MLA_TQ == 0 else ctx_len
        nq = seq // tq
        q_row = lambda b, h, i: b * nq + i
        o_row, n_rows = q_row, batch * seq
    else:
        tq, nq = ctx_len, 1
        q_row = lambda b, h, i: ctx0 + b
        o_row, n_rows = (lambda b, h, i: b), batch * ctx_len
    in_specs = [pl.BlockSpec((tq, hps * MLA_QK_PAD), lambda b, h, i: (q_row(b, h, i), h))]
    inputs = [q]
    if latent:
        in_specs += [pl.BlockSpec((seq, hps * MLA_QK_PAD), lambda b, h, i: (b, h)),
                     pl.BlockSpec((seq, hps * MLA_V_PAD), lambda b, h, i: (b, h))]
        inputs += [k, v]
    in_specs += [pl.BlockSpec((ctx_len, hps * MLA_QK_PAD), lambda b, h, i: (ctx0 + b, h)),
                 pl.BlockSpec((ctx_len, hps * MLA_V_PAD), lambda b, h, i: (ctx0 + b, h))]
    inputs += [k, v]
    return pl.pallas_call(
        functools.partial(_mla_attn_kernel, ck=ck, with_lat=latent),
        grid=(batch, MLA_HEADS // hps, nq), in_specs=in_specs,
        out_specs=pl.BlockSpec((tq, hps * MLA_V), lambda b, h, i: (o_row(b, h, i), h)),
        out_shape=jax.ShapeDtypeStruct((n_rows, MLA_HEADS * MLA_V), BF16),
        compiler_params=_cparams(("arbitrary", "arbitrary", "arbitrary")),
        name="mla_attn_lat" if latent else "mla_attn_ctx",
    )(*inputs)


def _swa_kernel(sink_ref, q_ref, kp_ref, kn_ref, kx_ref, kc_ref, vp_ref, vn_ref, vx_ref, vc_ref, o_ref,
                *, n_lat_steps, seq, ctx_queries):
    n = pl.program_id(1)
    g = SWA_HEADS // SWA_KV_HEADS
    blk = SWA_BLOCK
    nblk = SWA_BLOCKS_PER_STEP

    def attend(band):
        items = [(j, kh) for j in range(nblk) for kh in range(SWA_KV_HEADS)]
        ks = [slice(kh * SWA_HD, (kh + 1) * SWA_HD) for kh in range(SWA_KV_HEADS)]
        row = lax.broadcasted_iota(jnp.int32, (g * blk, 1), 0)
        if band:
            a = lax.broadcasted_iota(jnp.int32, (g * blk, 3 * blk), 0) & (blk - 1)
            jj = lax.broadcasted_iota(jnp.int32, (g * blk, 3 * blk), 1)
            in_window = jnp.abs(jj - blk - a) <= SWA_WINDOW
            k_slab = [jnp.concatenate([kp_ref[:, s], kn_ref[:, s], kx_ref[:, s]], axis=0) for s in ks]
            v_slab = [jnp.concatenate([vp_ref[:, s], vn_ref[:, s], vx_ref[:, s]], axis=0) for s in ks]
        sink = []
        for kh in range(SWA_KV_HEADS):
            sk = jnp.zeros((g * blk, 1), F32)
            for gi in range(g):
                sk = jnp.where((row >= gi * blk) & (row < (gi + 1) * blk), sink_ref[kh * g + gi], sk)
            sink.append(sk)
        s_c, s_b = {}, {}
        for it in items:
            j, kh = it
            qs = jnp.concatenate([q_ref[j * blk:(j + 1) * blk, (kh * g + gi) * SWA_HD:(kh * g + gi + 1) * SWA_HD]
                                  for gi in range(g)], axis=0)
            s_c[it] = _dot_nt(qs, kc_ref[:, ks[kh]])
            if band:
                key_pos = (n * nblk + j - 1) * blk + jj
                keep = in_window & (key_pos >= 0) & (key_pos < seq)
                s_b[it] = jnp.where(keep, _dot_nt(qs, k_slab[kh][j * blk:(j + 3) * blk]), -jnp.inf)
        p_c, p_b, den = {}, {}, {}
        for it in items:
            m = jnp.maximum(jnp.max(s_c[it], axis=-1, keepdims=True), sink[it[1]])
            if band:
                m = jnp.maximum(m, jnp.max(s_b[it], axis=-1, keepdims=True))
            pc = jnp.exp(s_c[it] - m)
            dn = jnp.sum(pc, axis=-1, keepdims=True) + jnp.exp(sink[it[1]] - m)
            if band:
                pb = jnp.exp(s_b[it] - m)
                dn = dn + jnp.sum(pb, axis=-1, keepdims=True)
                p_b[it] = pb.astype(BF16)
            p_c[it] = pc.astype(BF16)
            den[it] = dn
        for it in items:
            j, kh = it
            acc = _dot(p_c[it], vc_ref[:, ks[kh]])
            if band:
                acc = acc + _dot(p_b[it], v_slab[kh][j * blk:(j + 3) * blk])
            o = acc / den[it]
            for gi in range(g):
                c0 = (kh * g + gi) * SWA_HD
                o_ref[j * blk:(j + 1) * blk, c0:c0 + SWA_HD] = o[gi * blk:(gi + 1) * blk].astype(o_ref.dtype)

    if ctx_queries:
        pl.when(n < n_lat_steps)(lambda: attend(True))
        pl.when(n >= n_lat_steps)(lambda: attend(False))
    else:
        attend(True)


def _swa(p_att, sink, batch, seq, ctx_len, ctx_queries):
    blk = SWA_BLOCK
    nblk = SWA_BLOCKS_PER_STEP
    tq = nblk * blk
    assert seq % tq == 0 and ctx_len % tq == 0
    nb = seq // blk
    ns, ncs = seq // tq, ctx_len // tq
    ctx0 = batch * seq // ctx_len
    kcol, vcol = ATT_SK // SWA_KV_W, ATT_SV // SWA_KV_W
    q_row = lambda b, n: jnp.where(n < ns, b * ns + n, batch * ns + b * ncs + (n - ns))
    n_rows = batch * seq + (batch * ctx_len if ctx_queries else 0)

    def edge(col, first):
        def index(b, n):
            k = jnp.clip(n, 0, ns - 1) * nblk + (-1 if first else nblk)
            return (b * nb + jnp.clip(k, 0, nb - 1), col)
        return pl.BlockSpec((blk, SWA_KV_W), index)

    def own(col):
        return pl.BlockSpec((tq, SWA_KV_W), lambda b, n: (b * ns + jnp.clip(n, 0, ns - 1), col))

    ctx_k = pl.BlockSpec((ctx_len, SWA_KV_W), lambda b, n: (ctx0 + b, kcol))
    ctx_v = pl.BlockSpec((ctx_len, SWA_KV_W), lambda b, n: (ctx0 + b, vcol))
    return pl.pallas_call(
        functools.partial(_swa_kernel, n_lat_steps=ns, seq=seq, ctx_queries=ctx_queries),
        grid=(batch, ns + (ncs if ctx_queries else 0)),
        in_specs=[pl.BlockSpec(memory_space=pltpu.SMEM),
                  pl.BlockSpec((tq, SWA_W), lambda b, n: (q_row(b, n), 0)),
                  edge(kcol, True), own(kcol), edge(kcol, False), ctx_k,
                  edge(vcol, True), own(vcol), edge(vcol, False), ctx_v],
        out_specs=pl.BlockSpec((tq, SWA_W), lambda b, n: (q_row(b, n), 0)),
        out_shape=jax.ShapeDtypeStruct((n_rows, SWA_W), BF16),
        compiler_params=_cparams(("arbitrary", "arbitrary")),
        name="swa_attn",
    )(sink, *([p_att] * 9))


def _route(logits, rb):
    aff = _sigmoid(logits)
    sel = aff + rb
    tm = logits.shape[1]
    scores = []
    for g in range(N_GROUPS):
        r = [sel[g * E_PER_GROUP + j:g * E_PER_GROUP + j + 1] for j in range(E_PER_GROUP)]
        best = None
        for i in range(E_PER_GROUP):
            for j in range(i + 1, E_PER_GROUP):
                pair = r[i] + r[j]
                best = pair if best is None else jnp.maximum(best, pair)
        scores.append(best)
    gbest, gi = scores[0], jnp.zeros((1, tm), jnp.int32)
    for g in range(1, N_GROUPS):
        upd = scores[g] > gbest
        gbest = jnp.where(upd, scores[g], gbest)
        gi = jnp.where(upd, g, gi)
    s_in, a_in = [], []
    for j in range(E_PER_GROUP):
        sj, aj = sel[j:j + 1], aff[j:j + 1]
        for g in range(1, N_GROUPS):
            e = g * E_PER_GROUP + j
            sj = jnp.where(gi == g, sel[e:e + 1], sj)
            aj = jnp.where(gi == g, aff[e:e + 1], aj)
        s_in.append(sj)
        a_in.append(aj)
    chosen = []
    for j in range(E_PER_GROUP):
        rank = jnp.zeros((1, tm), jnp.int32)
        for k in range(E_PER_GROUP):
            if k == j:
                continue
            ahead = (s_in[k] >= s_in[j]) if k < j else (s_in[k] > s_in[j])
            rank = rank + ahead.astype(jnp.int32)
        chosen.append(rank < 2)
    w = [jnp.where(chosen[j], a_in[j], 0.0) for j in range(E_PER_GROUP)]
    wsum = w[0] + w[1] + w[2] + w[3]
    gate_in = [wj / wsum * ROUTE_SCALE for wj in w]
    code = sum(jnp.where(chosen[j], 1 << j, 0) for j in range(E_PER_GROUP))
    pair = jnp.zeros((1, tm), jnp.int32)
    for idx, cval in enumerate((3, 5, 9, 6, 10, 12)):
        pair = jnp.where(code == cval, idx, pair)
    bucket = gi * N_PAIRS + pair
    g_lo = jnp.zeros((1, tm), F32)
    g_hi = jnp.zeros((1, tm), F32)
    seen = jnp.zeros((1, tm), jnp.bool_)
    for j in range(E_PER_GROUP):
        g_lo = jnp.where(chosen[j] & ~seen, gate_in[j], g_lo)
        g_hi = jnp.where(chosen[j] & seen, gate_in[j], g_hi)
        seen = seen | chosen[j]
    return g_lo, g_hi, bucket


def _lat_or_ctx(lat_ref, ctx_ref, rs, lat_tiles):
    if ctx_ref is None:
        return lat_ref[rs, :]
    return jnp.where(pl.program_id(0) < lat_tiles, lat_ref[rs, :], ctx_ref[rs, :])


def _dual_specs(rows, width):
    return [pl.BlockSpec((rows.tm, width), lambda i, *_: (jnp.minimum(i, rows.lat_tiles - 1), 0)),
            pl.BlockSpec((rows.tm, width), lambda i, *_: (jnp.maximum(i - rows.lat_tiles, 0), 0),
                         pipeline_mode=pl.Buffered(1))]


def _out_kernel(*refs, alpha, dual, lat_tiles):
    if dual:
        (ohg_ref, omla_ref, omlac_ref, oswa_ref, x_ref, xc_ref, g1_ref, sh2_ref, sc2_ref, lng_ref, lnb_ref,
         w_ref, rwt_ref, rb_ref, x1_ref, h2_ref, bucket_ref) = refs
    else:
        (ohg_ref, omla_ref, oswa_ref, x_ref, g1_ref, sh2_ref, sc2_ref, lng_ref, lnb_ref,
         w_ref, rwt_ref, rb_ref, x1_ref, h2_ref, bucket_ref) = refs
        omlac_ref = xc_ref = None
    o1, o2 = HG_W, HG_W + MLA_HEADS * MLA_V
    d = x_ref.shape[1]
    tm = x_ref.shape[0]
    halves = [slice(0, tm // 2), slice(tm // 2, tm)]
    mixes = [_dot(ohg_ref[rs, :], w_ref[0:o1, :])
             + _dot(_lat_or_ctx(omla_ref, omlac_ref, rs, lat_tiles), w_ref[o1:o2, :])
             + _dot(oswa_ref[rs, :], w_ref[o2:, :]) for rs in halves]
    for rs, mix in zip(halves, mixes):
        x_in = _lat_or_ctx(x_ref, xc_ref, rs, lat_tiles)
        x1 = _ln_rows(alpha * x_in + g1_ref[...] * mix) * lng_ref[...] + lnb_ref[...]
        x1_ref[rs, :] = x1
        h2 = (_ln_rows(x1) * (1.0 + sc2_ref[...]) + sh2_ref[...]).astype(BF16)
        h2_ref[rs, 0:d] = h2.astype(h2_ref.dtype)
        g_lo, g_hi, bucket = _route(_dot_nt(rwt_ref[...], h2), rb_ref[...])
        bucket_ref[:, rs] = bucket
        gate_rows = jnp.concatenate([g_lo, g_hi, jnp.zeros((LANES - 2, tm // 2), F32)], axis=0)
        h2_ref[rs, d:d + LANES] = gate_rows.T


def _out_proj(o_hg, o_mla, o_swa, x, mod, ln_g, ln_b, w_out, rwt, rb, rows, alpha):
    tm = rows.tm
    dual = isinstance(x, tuple)
    d = (x[0] if dual else x).shape[1]
    n = rows.n_rows
    row = lambda w: pl.BlockSpec((tm, w), lambda i: (i, 0))
    whole = lambda a: pl.BlockSpec(a.shape, lambda i: (0, 0), pipeline_mode=pl.Buffered(1))
    mla_w = MLA_HEADS * MLA_V
    if dual:
        in_specs = [row(HG_W)] + _dual_specs(rows, mla_w) + [row(SWA_W)] + _dual_specs(rows, d)
        inputs = [o_hg, *o_mla, o_swa, *x]
    else:
        in_specs = [row(HG_W), row(mla_w), row(SWA_W), row(d)]
        inputs = [o_hg, o_mla, o_swa, x]
    in_specs += [_mod_spec(rows, 2, d), _mod_spec(rows, 3, d), _mod_spec(rows, 4, d),
                 whole(ln_g), whole(ln_b), whole(w_out), whole(rwt), whole(rb)]
    inputs += [mod, mod, mod, ln_g, ln_b, w_out, rwt, rb]
    return pl.pallas_call(
        functools.partial(_out_kernel, alpha=alpha, dual=dual, lat_tiles=rows.lat_tiles), grid=(rows.n_tiles,),
        in_specs=in_specs,
        out_specs=[row(d), row(d + LANES), pl.BlockSpec((1, tm), lambda i: (0, i))],
        out_shape=[jax.ShapeDtypeStruct((n, d), F32), jax.ShapeDtypeStruct((n, d + LANES), F32),
                   jax.ShapeDtypeStruct((1, n), jnp.int32)],
        compiler_params=_cparams(("arbitrary",)), name="out_proj_ln_router",
    )(*inputs)


def _row_gather_start(idx_ref, base, src_hbm, dst, sem, n, static_rows=False):
    def issue(r, carry):
        pltpu.make_async_copy(src_hbm.at[pl.ds(idx_ref[base + r], 1)], dst.at[pl.ds(r, 1)], sem).start()
        return carry
    if static_rows:
        for r in range(n):
            issue(r, 0)
    else:
        lax.fori_loop(0, n, issue, 0, unroll=8)


def _row_gather_start_next(t, n_valid, idx_ref, base, src_hbm, buf, sem, n):
    for half in range(2):
        @pl.when((t < n_valid) & (t % 2 == half))
        def _():
            _row_gather_start(idx_ref, base, src_hbm, buf.at[half], sem.at[half], n, static_rows=True)


def _row_gather_wait(src_hbm, dst, sem, n):
    pltpu.make_async_copy(src_hbm.at[pl.ds(0, n)], dst, sem).wait()


def _moe_kernel(stok_ref, off_ref, e1_ref, e2_ref, nused_ref, h_hbm, wg1_ref, wu1_ref, wd1_ref,
                wg2_ref, wu2_ref, wd2_ref, y_ref, hbuf, sem, *, tm):
    i = pl.program_id(0)
    n_used = nused_ref[0]
    d = y_ref.shape[1]

    @pl.when(i == 0)
    def _():
        _row_gather_start(stok_ref, off_ref[0], h_hbm, hbuf.at[0], sem.at[0], tm)

    nxt = jnp.minimum(i + 1, pl.num_programs(0) - 1)
    _row_gather_start_next(i + 1, n_used, stok_ref, off_ref[nxt], h_hbm, hbuf, sem, tm)

    @pl.when(i < n_used)
    def _():
        slot = i % 2
        _row_gather_wait(h_hbm, hbuf.at[slot], sem.at[slot], tm)
        h = hbuf[slot, :, 0:d].astype(BF16)
        g_lo = hbuf[slot, :, d:d + 1]
        g_hi = hbuf[slot, :, d + 1:d + 2]

        act1 = (_silu(_dot(h, wg1_ref[...])) * _dot(h, wu1_ref[...])).astype(BF16)
        act2 = (_silu(_dot(h, wg2_ref[...])) * _dot(h, wu2_ref[...])).astype(BF16)
        y_ref[...] = g_lo * _dot(act1, wd1_ref[...]) + g_hi * _dot(act2, wd2_ref[...])

    @pl.when(i >= n_used)
    def _():
        y_ref[...] = jnp.zeros_like(y_ref)


def _moe(h2, stok, off, e1, e2, n_used, wg, wu, wd, layer, tm):
    n_tiles = off.shape[0]
    dx = h2.shape[1]
    d, f = wg.shape[2], wg.shape[3]
    last = lambda i, nu: jnp.minimum(i, nu[0] - 1)
    wspec_in = lambda sel: pl.BlockSpec(
        (None, None, d, f), lambda i, s, o, a, b, nu: (layer, (a, b)[sel][last(i, nu)], 0, 0))
    wspec_dn = lambda sel: pl.BlockSpec(
        (None, None, f, d), lambda i, s, o, a, b, nu: (layer, (a, b)[sel][last(i, nu)], 0, 0))
    grid_spec = pltpu.PrefetchScalarGridSpec(
        num_scalar_prefetch=5, grid=(n_tiles,),
        in_specs=[pl.BlockSpec(memory_space=pl.ANY),
                  wspec_in(0), wspec_in(0), wspec_dn(0), wspec_in(1), wspec_in(1), wspec_dn(1)],
        out_specs=pl.BlockSpec((tm, d), lambda i, s, o, a, b, nu: (i, 0)),
        scratch_shapes=[pltpu.VMEM((2, tm, dx), F32), pltpu.SemaphoreType.DMA((2,))])
    return pl.pallas_call(
        functools.partial(_moe_kernel, tm=tm), grid_spec=grid_spec,
        out_shape=jax.ShapeDtypeStruct((n_tiles * tm, d), F32),
        compiler_params=_cparams(("arbitrary",)), name="moe_grouped",
    )(stok, off, e1, e2, n_used, h2, wg, wu, wd, wg, wu, wd)


def _moe_plan(bucket, n_tokens, tm):
    tok = jnp.arange(n_tokens, dtype=jnp.int32)
    buckets = jnp.arange(N_BUCKETS, dtype=jnp.int32)
    skey, stok = lax.sort((bucket * n_tokens + tok, tok), num_keys=1)
    counts = jnp.sum((bucket[:, None] == buckets[None, :]).astype(jnp.int32), axis=0)
    padded = (counts + tm - 1) // tm * tm
    ends = jnp.cumsum(padded)
    shift = (ends - padded) - (jnp.cumsum(counts) - counts)
    sbucket = skey // n_tokens
    slot = tok + jnp.sum(jnp.where(sbucket[:, None] == buckets[None, :], shift[None, :], 0), axis=1)
    _, pos = lax.sort((stok, slot), num_keys=1)
    n_tiles = n_tokens // tm + N_BUCKETS
    tile_start = jnp.arange(n_tiles, dtype=jnp.int32) * tm
    tile_bucket = jnp.sum((ends[None, :] <= tile_start[:, None]).astype(jnp.int32), axis=1)
    tile_bucket = jnp.minimum(tile_bucket, N_BUCKETS - 1)
    tshift = jnp.sum(jnp.where(tile_bucket[:, None] == buckets[None, :], shift[None, :], 0), axis=1)
    off = jnp.clip(tile_start - tshift, 0, n_tokens)
    grp, pair = tile_bucket // N_PAIRS, tile_bucket % N_PAIRS
    lo = jnp.where(pair < 3, 0, jnp.where(pair < 5, 1, 2))
    hi = jnp.where(pair == 0, 1, jnp.where((pair == 1) | (pair == 3), 2, 3))
    e1, e2 = grp * E_PER_GROUP + lo, grp * E_PER_GROUP + hi
    n_used = (ends[-1] // tm).astype(jnp.int32).reshape(1)
    stok = jnp.concatenate([stok, jnp.zeros((tm,), jnp.int32)])
    return stok, off.astype(jnp.int32), pos.astype(jnp.int32), e1.astype(jnp.int32), e2.astype(jnp.int32), n_used


def _ln2_kernel(pos_ref, y_hbm, x1_ref, g2_ref, lng_ref, lnb_ref, *rest, alpha, tm, emit_h):
    if emit_h:
        sh_ref, sc_ref, x2_ref, h_ref, ybuf, sem = rest
    else:
        x2_ref, ybuf, sem = rest
    i = pl.program_id(0)
    n_tiles = pl.num_programs(0)

    @pl.when(i == 0)
    def _():
        _row_gather_start(pos_ref, 0, y_hbm, ybuf.at[0], sem.at[0], tm)

    _row_gather_start_next(i + 1, n_tiles, pos_ref, (i + 1) * tm, y_hbm, ybuf, sem, tm)

    slot = i % 2
    _row_gather_wait(y_hbm, ybuf.at[slot], sem.at[slot], tm)
    x2 = _ln_rows(alpha * x1_ref[...] + g2_ref[...] * ybuf[slot]) * lng_ref[...] + lnb_ref[...]
    x2_ref[...] = x2
    if emit_h:
        h_ref[...] = (_ln_rows(x2) * (1.0 + sc_ref[...]) + sh_ref[...]).astype(h_ref.dtype)


def _ln2(pos, y_sorted, x1, mod, mod_next, ln_g, ln_b, rows, alpha):
    tm = rows.tm
    d = x1.shape[1]
    emit_h = mod_next is not None
    row = pl.BlockSpec((tm, d), lambda i, p: (i, 0))
    whole = lambda a: pl.BlockSpec(a.shape, lambda i, p: (0, 0))
    in_specs = [pl.BlockSpec(memory_space=pl.ANY), row, _mod_spec(rows, 5, d), whole(ln_g), whole(ln_b)]
    inputs = [y_sorted, x1, mod, ln_g, ln_b]
    out_specs = [row]
    out_shape = [jax.ShapeDtypeStruct((rows.n_rows, d), F32)]
    if emit_h:
        in_specs += [_mod_spec(rows, 0, d), _mod_spec(rows, 1, d)]
        inputs += [mod_next, mod_next]
        out_specs.append(row)
        out_shape.append(jax.ShapeDtypeStruct((rows.n_rows, d), BF16))
    grid_spec = pltpu.PrefetchScalarGridSpec(
        num_scalar_prefetch=1, grid=(rows.n_tiles,), in_specs=in_specs, out_specs=out_specs,
        scratch_shapes=[pltpu.VMEM((2, tm, d), F32), pltpu.SemaphoreType.DMA((2,))])
    out = pl.pallas_call(
        functools.partial(_ln2_kernel, alpha=alpha, tm=tm, emit_h=emit_h), grid_spec=grid_spec,
        out_shape=out_shape, compiler_params=_cparams(("arbitrary",)), name="unpermute_ln2",
    )(pos, *inputs)
    return out if emit_h else (out[0], None)


def _rope_tables(seq, batch, ctx_rows, dim, pad_to):
    rows = seq // GRID_W
    row = jnp.repeat(jnp.arange(rows, dtype=jnp.int32), GRID_W)
    col = jnp.tile(jnp.arange(GRID_W, dtype=jnp.int32), rows)
    nf = dim // 4
    inv_freq = ROPE_BASE ** (-jnp.arange(nf, dtype=F32) / nf)
    ang = jnp.stack([row, col], -1).astype(F32)[:, :, None] * inv_freq
    cos, sin = jnp.cos(ang), jnp.sin(ang)
    c = jnp.stack([cos, cos], axis=2).reshape(seq, dim)
    s = jnp.stack([-sin, sin], axis=2).reshape(seq, dim)
    if pad_to > dim:
        c = jnp.pad(c, ((0, 0), (0, pad_to - dim)))
        s = jnp.pad(s, ((0, 0), (0, pad_to - dim)))
    ctx_c = jnp.zeros((ctx_rows, pad_to), F32).at[:, :dim].set(1.0)
    c = jnp.concatenate([jnp.tile(c, (batch, 1)), ctx_c], axis=0)
    s = jnp.concatenate([jnp.tile(s, (batch, 1)), jnp.zeros((ctx_rows, pad_to), F32)], axis=0)
    return c, s


def _w_in_prep_kernel(w_ref, raw_ref, gate_ref, att_ref):
    o1 = 5 * HG_W
    o2 = o1 + MLA_Q_LORA + MLA_KV_LORA
    o3 = o2 + MLA_ROPE
    cast = lambda a, b: w_ref[:, a:b].astype(BF16)
    raw_ref[:, 0:HG_W] = cast(0, HG_W)
    raw_ref[:, HG_W:3 * HG_W] = cast(3 * HG_W, o1)
    gate_ref[...] = cast(HG_W, 3 * HG_W)
    att_ref[:, ATT_SQ:ATT_CQ] = cast(o3, o3 + SWA_W + 2 * SWA_KV_W)
    att_ref[:, ATT_CQ:ATT_KR] = cast(o1, o2)
    att_ref[:, ATT_KR:ATT_KR + MLA_ROPE] = cast(o2, o3)
    att_ref[:, ATT_KR + MLA_ROPE:ATT_COLS] = jnp.zeros((w_ref.shape[0], LANES - MLA_ROPE), BF16)


def _w_in_prep(w_in):
    depth, d, n = w_in.shape
    tr = 256 if d % 256 == 0 else d
    widths = (3 * HG_W, 2 * HG_W, ATT_COLS)
    return pl.pallas_call(
        _w_in_prep_kernel, grid=(depth, d // tr),
        in_specs=[pl.BlockSpec((None, tr, n), lambda l, i: (l, i, 0))],
        out_specs=[pl.BlockSpec((None, tr, w), lambda l, i: (l, i, 0)) for w in widths],
        out_shape=[jax.ShapeDtypeStruct((depth, d, w), BF16) for w in widths],
        compiler_params=_cparams(("arbitrary", "arbitrary")), name="w_in_prep",
    )(w_in)


def _mla_weights(w_uq, w_ukv):
    qk = MLA_NOPE + MLA_ROPE
    wq = w_uq.reshape(MLA_Q_LORA, MLA_HEADS, qk)
    wq = jnp.pad(wq, ((0, 0), (0, 0), (0, MLA_QK_PAD - qk))).reshape(MLA_Q_LORA, MLA_HEADS * MLA_QK_PAD)
    wkv = w_ukv.reshape(MLA_KV_LORA, MLA_HEADS, MLA_NOPE + MLA_V)
    wkv = jnp.concatenate([wkv[:, :, :MLA_NOPE].reshape(MLA_KV_LORA, -1), wkv[:, :, MLA_NOPE:].reshape(MLA_KV_LORA, -1)], axis=1)
    return wq.astype(BF16), wkv.astype(BF16)


def kernel(x, c, ctx, c_ctx, w_ada, b_ada, w_in, w_out, hg_lb_logits, hg_norm_g, mla_q_norm, mla_kv_norm,
           mla_w_uq, mla_w_ukv, swa_sink, ln1_g, ln1_b, ln2_g, ln2_b, router_w, router_b,
           moe_w_gate, moe_w_up, moe_w_down):
    batch, seq, d = x.shape
    ctx_len = ctx.shape[1]
    depth = w_ada.shape[0]
    alpha = (2.0 * depth) ** 0.25
    n_lat, n_ctx = batch * seq, batch * ctx_len
    n_all = n_lat + n_ctx
    assert batch + 1 <= 8 and ctx_len % (4 * HG_CHUNK) == 0 and seq % ctx_len == 0 and seq % GRID_W == 0

    c8 = jnp.zeros((8, d), F32).at[:batch].set(c).at[batch].set(c_ctx)
    mod_all = _ada(c8, w_ada, b_ada).reshape(depth, 8 * 6, 1, d)

    lb = jnp.cumsum(jax.nn.softmax(hg_lb_logits.astype(F32), axis=0), axis=0)
    lb = (lb - lb[0:1]).reshape(depth, 1, 2 * HG_W)
    log_lb, log_1m, one_m = jnp.log(lb), jnp.log1p(-lb), 1.0 - lb

    cs, ss = _rope_tables(seq, batch, n_ctx, SWA_HD, SWA_HD)
    cm, sm = _rope_tables(seq, batch, n_ctx, MLA_ROPE, LANES)
    rwt = router_w.T.astype(BF16)
    rb = router_b.astype(F32).reshape(N_EXPERTS, 1)

    tm_all = _pick_tm(seq, n_ctx, 512)
    rows_all = _Rows(batch, seq, ctx_len, n_all, tm_all)
    rows_lat = _Rows(batch, seq, ctx_len, n_lat, tm_all)
    moe_tm = 256

    xa = (x.reshape(n_lat, d), ctx.reshape(n_ctx, d))
    h = _lnmod(xa[0], xa[1], mod_all[0], rows_all, 0)
    w_raw_all, w_gate_all, w_att_all = _w_in_prep(w_in)
    wg_all, wu_all, wd_all = moe_w_gate.astype(BF16), moe_w_up.astype(BF16), moe_w_down.astype(BF16)

    for layer in range(depth):
        need_ctx = layer < depth - 1
        mod = mod_all[layer]
        rows = rows_all if need_ctx else rows_lat
        wq, wkv = _mla_weights(mla_w_uq[layer], mla_w_ukv[layer])

        lf, kk, p_att, p_raw = _in_proj(h, w_gate_all, w_att_all, w_raw_all, layer, log_lb[layer], log_1m[layer],
                                        one_m[layer], [cs, ss, cm, sm], rows_all)

        o_f = _hg_scan(p_raw, kk, lf, batch, seq, ctx_len, False)
        o_hg = _hg_scan(p_raw, kk, lf, batch, seq, ctx_len, True,
                        (o_f, hg_norm_g[layer].astype(F32).reshape(1, HG_DK)))

        q_mla, k_mla, v_mla = _mla_proj(p_att, mla_q_norm[layer].astype(F32).reshape(1, -1),
                                        mla_kv_norm[layer].astype(F32).reshape(1, -1), wq, wkv, cm, sm, rows_all)
        n_out = rows.n_rows
        o_mla = _mla_attn(q_mla, k_mla, v_mla, batch, seq, ctx_len, True)
        o_swa = _swa(p_att, swa_sink[layer].astype(F32), batch, seq, ctx_len, need_ctx)
        dual = isinstance(xa, tuple)
        if need_ctx:
            o_mla = (o_mla, _mla_attn(q_mla, k_mla, v_mla, batch, seq, ctx_len, False))
            if not dual:
                xa = (xa[:n_lat], xa[n_lat:])
        elif dual:
            xa = xa[0]

        x1, h2, bucket = _out_proj(
            o_hg, o_mla, o_swa, xa, mod, ln1_g[layer].astype(F32).reshape(1, d), ln1_b[layer].astype(F32).reshape(1, d),
            w_out[layer].astype(BF16), rwt, rb, rows, alpha)

        stok, off, pos, e1, e2, n_used = _moe_plan(bucket[0], n_out, moe_tm)
        y_sorted = _moe(h2, stok, off, e1, e2, n_used, wg_all, wu_all, wd_all, layer, moe_tm)
        xa, h = _ln2(pos, y_sorted, x1, mod, mod_all[layer + 1] if need_ctx else None,
                     ln2_g[layer].astype(F32).reshape(1, d), ln2_b[layer].astype(F32).reshape(1, d), rows, alpha)

    return xa[:n_lat].reshape(batch, seq, d)
```

```python
import functools
import math

import jax
import jax.numpy as jnp
from jax import lax
from jax.experimental import pallas as pl
from jax.experimental.pallas import tpu as pltpu

F32 = jnp.float32
BF16 = jnp.bfloat16

GRID_W = 64
HG_HEADS = 4
HG_DK = 128
HG_W = HG_HEADS * HG_DK
HG_CHUNK = 64
HG_SUB = 16
MLA_HEADS = 8
MLA_Q_LORA = 512
MLA_KV_LORA = 256
MLA_NOPE = 128
MLA_ROPE = 64
MLA_V = 128
MLA_QK_PAD = 256
MLA_V_PAD = 256
MLA_HEADS_PER_STEP = 4
MLA_TQ = 1024
SWA_HEADS = 4
SWA_KV_HEADS = 2
SWA_HD = 128
SWA_W = SWA_HEADS * SWA_HD
SWA_KV_W = SWA_KV_HEADS * SWA_HD
SWA_WINDOW = 128
SWA_BLOCK = 128
SWA_BLOCKS_PER_STEP = 2
N_EXPERTS = 16
N_GROUPS = 4
E_PER_GROUP = 4
N_PAIRS = 6
N_BUCKETS = N_GROUPS * N_PAIRS
ROUTE_SCALE = 2.5
ROPE_BASE = 10000.0
LN_EPS = 1e-5
RMS_EPS = 1e-6
LANES = 128
VMEM_LIMIT = 56 * 1024 * 1024

ATT_SQ, ATT_SK, ATT_SV = 0, SWA_W, SWA_W + SWA_KV_W
ATT_CQ = SWA_W + 2 * SWA_KV_W
ATT_CKV = ATT_CQ + MLA_Q_LORA
ATT_KR = ATT_CKV + MLA_KV_LORA
ATT_COLS = ATT_KR + LANES


def _cparams(sem):
    return pltpu.CompilerParams(dimension_semantics=sem, vmem_limit_bytes=VMEM_LIMIT)


def _dot(a, b):
    return jnp.dot(a, b, preferred_element_type=F32)


def _dot_nt(a, b):
    return lax.dot_general(a, b, (((1,), (1,)), ((), ())), preferred_element_type=F32)


def _dot_tn(a, b):
    return lax.dot_general(a, b, (((0,), (0,)), ((), ())), preferred_element_type=F32)


def _sigmoid(x):
    return 1.0 / (1.0 + jnp.exp(-x))


def _silu(x):
    return x * _sigmoid(x)


def _ln_rows(x):
    mu = jnp.mean(x, axis=-1, keepdims=True)
    xc = x - mu
    var = jnp.mean(xc * xc, axis=-1, keepdims=True)
    return xc * lax.rsqrt(var + LN_EPS)


def _rope_partner(x, half):
    lane = lax.broadcasted_iota(jnp.int32, x.shape, x.ndim - 1)
    first = (lane & half) == 0
    n = x.shape[-1]
    return jnp.where(first, pltpu.roll(x, n - half, x.ndim - 1), pltpu.roll(x, half, x.ndim - 1))


def _ada_kernel(c_ref, w_ref, b_ref, o_ref):
    s = _silu(c_ref[...])
    o_ref[...] = _dot(s.astype(BF16), w_ref[...].astype(BF16)) + b_ref[...]


def _ada(c8, w_ada, b_ada):
    depth, d, n = w_ada.shape
    tn = 1024 if n % 1024 == 0 else n
    return pl.pallas_call(
        _ada_kernel,
        grid=(depth, n // tn),
        in_specs=[pl.BlockSpec((8, d), lambda l, j: (0, 0)),
                  pl.BlockSpec((None, d, tn), lambda l, j: (l, 0, j)),
                  pl.BlockSpec((None, 1, tn), lambda l, j: (l, 0, j))],
        out_specs=pl.BlockSpec((None, 8, tn), lambda l, j: (l, 0, j)),
        out_shape=jax.ShapeDtypeStruct((depth, 8, n), F32),
        compiler_params=_cparams(("arbitrary", "arbitrary")),
        name="ada_mod",
    )(c8, w_ada, b_ada.reshape(depth, 1, n))


class _Rows:
    def __init__(self, batch, seq, ctx_len, n_rows, tm):
        self.batch, self.seq, self.ctx_len, self.n_rows, self.tm = batch, seq, ctx_len, n_rows, tm
        self.n_tiles = n_rows // tm
        self.lat_tiles = batch * seq // tm
        self.tiles_per_batch = seq // tm

    def mod_row(self, i):
        return jnp.where(i < self.lat_tiles, i // self.tiles_per_batch, self.batch)


def _pick_tm(seq, ctx_rows, cap):
    for tm in (1024, 512, 256, 128):
        if tm <= cap and seq % tm == 0 and ctx_rows % tm == 0:
            return tm
    raise ValueError("unsupported sequence / context lengths")


def _mod_spec(rows, chunk, d):
    return pl.BlockSpec((None, 1, d), lambda i, *_: (rows.mod_row(i) * 6 + chunk, 0, 0))


def _lnmod_kernel(x_ref, xc_ref, sh_ref, sc_ref, h_ref, *, lat_tiles):
    y = _ln_rows(_lat_or_ctx(x_ref, xc_ref, slice(None), lat_tiles))
    h_ref[...] = (y * (1.0 + sc_ref[...]) + sh_ref[...]).astype(h_ref.dtype)


def _lnmod(x_lat, x_ctx, mod, rows, chunk0):
    d = x_lat.shape[1]
    tm = rows.tm
    return pl.pallas_call(
        functools.partial(_lnmod_kernel, lat_tiles=rows.lat_tiles),
        grid=(rows.n_tiles,),
        in_specs=_dual_specs(rows, d) + [_mod_spec(rows, chunk0, d), _mod_spec(rows, chunk0 + 1, d)],
        out_specs=pl.BlockSpec((tm, d), lambda i: (i, 0)),
        out_shape=jax.ShapeDtypeStruct((rows.n_rows, d), BF16),
        compiler_params=_cparams(("arbitrary",)),
        name="ln_modulate",
    )(x_lat, x_ctx, mod, mod)


def _col_group_dots(h, w_ref, width):
    n = w_ref.shape[1]
    groups = [slice(c0, min(c0 + width, n)) for c0 in range(0, n, width)]
    return groups, [_dot(h, w_ref[:, g]) for g in groups]


def _in_proj_kernel(h_ref, wgate_ref, watt_ref, wraw_ref, loglb_ref, log1m_ref, onem_ref,
                    cs_ref, ss_ref, cm_ref, sm_ref, lf_ref, k_ref, att_ref, raw_ref):
    h = h_ref[...]
    g_groups, zs = _col_group_dots(h, wgate_ref, 2 * LANES)
    a_groups, ps = _col_group_dots(h, watt_ref, 2 * LANES)
    r_groups, rs = _col_group_dots(h, wraw_ref, 2 * LANES)
    for g, z in zip(g_groups, zs):
        e = jnp.exp(-jnp.abs(z))
        r = 1.0 / (1.0 + e)
        log_sig = jnp.minimum(z, 0.0) + jnp.log(r)
        sig_neg = jnp.where(z >= 0.0, e * r, r)
        a = loglb_ref[:, g]
        b = log1m_ref[:, g] + log_sig
        lf_ref[:, g] = jnp.maximum(a, b) + jnp.log(1.0 + jnp.exp(-jnp.abs(a - b)))
        k_ref[:, g] = onem_ref[:, g] * sig_neg
    cs, ss = cs_ref[...], ss_ref[...]
    swa_scale = SWA_HD ** -0.5
    for g, p in zip(a_groups, ps):
        for c0 in range(g.start, g.stop, LANES):
            x = p[:, c0 - g.start:c0 - g.start + LANES]
            if c0 < ATT_SV:
                x = x * cs + _rope_partner(x, SWA_HD // 4) * ss
                if c0 < ATT_SK:
                    x = x * swa_scale
            elif c0 == ATT_KR:
                x = x * cm_ref[...] + _rope_partner(x, MLA_ROPE // 4) * sm_ref[...]
            att_ref[:, c0:c0 + LANES] = x.astype(att_ref.dtype)
    for g, r in zip(r_groups, rs):
        raw_ref[:, g] = r


def _in_proj(h, w_gate_all, w_att_all, w_raw_all, layer, log_lb, log_1m, one_m, tables, rows):
    tm = rows.tm
    d = h.shape[1]
    n = rows.n_rows
    single = lambda spec: pl.BlockSpec(spec.block_shape, spec.index_map, pipeline_mode=pl.Buffered(1))
    consts = [(w_gate_all, layer), (w_att_all, layer), (w_raw_all, layer), log_lb, log_1m, one_m]
    row = lambda w: pl.BlockSpec((tm, w), lambda i: (i, 0))
    widths = (2 * HG_W, 2 * HG_W, ATT_COLS, 3 * HG_W)
    dtypes = (F32, F32, BF16, F32)
    return pl.pallas_call(
        _in_proj_kernel, grid=(rows.n_tiles,),
        in_specs=[row(d)] + [single(_const_spec(a)) for a in consts] + [row(LANES)] * 4,
        out_specs=[row(w) for w in widths],
        out_shape=[jax.ShapeDtypeStruct((n, w), dt) for w, dt in zip(widths, dtypes)],
        compiler_params=_cparams(("arbitrary",)), name="in_proj",
    )(h, w_gate_all, w_att_all, w_raw_all, log_lb, log_1m, one_m, *tables)


def _const_spec(a):
    if isinstance(a, tuple):
        stacked, layer = a
        return pl.BlockSpec((None,) + stacked.shape[1:], lambda i, *_: (layer,) + (0,) * (stacked.ndim - 1))
    return pl.BlockSpec(a.shape, lambda i, *_: (0,) * a.ndim)


def _hg_scan_kernel(*refs, reverse, n_chunks, readout):
    if readout:
        q_ref, k_ref, lf_ref, v_ref, of_ref, g_ref, ng_ref, o_ref, st_ref = refs
    else:
        q_ref, k_ref, lf_ref, v_ref, o_ref, st_ref = refs

    @pl.when(pl.program_id(1) == 0)
    def _():
        st_ref[...] = jnp.zeros_like(st_ref)

    c, s = HG_CHUNK, HG_SUB
    nsub = c // s
    chunks = [n_chunks - 1 - cc if reverse else cc for cc in range(n_chunks)]
    items = [(ch, hd) for ch in chunks for hd in range(HG_HEADS)]
    rows = lambda ch: slice(ch * c, (ch + 1) * c)
    cols = lambda hd: slice(hd * HG_DK, (hd + 1) * HG_DK)

    ri = lax.broadcasted_iota(jnp.int32, (c, c), 0)
    ci = lax.broadcasted_iota(jnp.int32, (c, c), 1)
    tri = jnp.where((ci >= ri) if reverse else (ci <= ri), 1.0, 0.0).astype(BF16)
    b_all = {}
    for ch in chunks:
        lf = lf_ref[rows(ch), :]
        hi = lf.astype(BF16)
        r1 = lf - hi.astype(F32)
        mid = r1.astype(BF16)
        lo = (r1 - mid.astype(F32)).astype(BF16)
        b_all[ch] = _dot(tri, hi) + _dot(tri, mid) + _dot(tri, lo)

    qe, k_dec, decay, v16, qt, kt = {}, {}, {}, {}, {}, {}
    for it in items:
        ch, hd = it
        b = b_all[ch][:, cols(hd)]
        q, k = q_ref[rows(ch), cols(hd)], k_ref[rows(ch), cols(hd)]
        btot = b[0:1, :] if reverse else b[c - 1:c, :]
        qe[it] = (q * jnp.exp(b)).astype(BF16)
        k_dec[it] = (k * jnp.exp(btot - b)).astype(BF16)
        decay[it] = jnp.exp(btot)
        v16[it] = v_ref[rows(ch), cols(hd)].astype(BF16)
        for i in range(nsub):
            r0 = i * s
            if reverse:
                k0, k1 = r0, c
                ref = b[r0 + s:r0 + s + 1, :] if i < nsub - 1 else jnp.zeros_like(btot)
            else:
                k0, k1 = 0, r0 + s
                ref = b[r0 - 1:r0, :] if i > 0 else jnp.zeros_like(btot)
            qt[it, i] = (q[r0:r0 + s] * jnp.exp(b[r0:r0 + s] - ref)).astype(BF16)
            kt[it, i] = (k[k0:k1] * jnp.exp(ref - b[k0:k1])).astype(BF16)

    upd = {it: _dot_tn(v16[it], k_dec[it]) for it in items}
    att = {}
    for it in items:
        for i in range(nsub):
            r0 = i * s
            k0 = r0 if reverse else 0
            a = _dot_nt(qt[it, i], kt[it, i])
            rr = lax.broadcasted_iota(jnp.int32, a.shape, 0) + r0
            cc = lax.broadcasted_iota(jnp.int32, a.shape, 1) + k0
            att[it, i] = jnp.where((cc >= rr) if reverse else (cc <= rr), a, 0.0).astype(BF16)

    states = [st_ref[hd] for hd in range(HG_HEADS)]
    o_inter = {}
    for it in items:
        ch, hd = it
        o_inter[it] = _dot_nt(qe[it], states[hd].astype(BF16))
        states[hd] = states[hd] * decay[it] + upd[it]
    for hd in range(HG_HEADS):
        st_ref[hd] = states[hd]

    for it in items:
        ch, hd = it
        outs = []
        for i in range(nsub):
            r0 = i * s
            k0, k1 = (r0, c) if reverse else (0, r0 + s)
            outs.append(o_inter[it][r0:r0 + s] + _dot(att[it, i], v16[it][k0:k1]))
        o = jnp.concatenate(outs, axis=0)
        if readout:
            o = o + of_ref[rows(ch), cols(hd)]
            o = o * lax.rsqrt(jnp.mean(o * o, axis=-1, keepdims=True) + RMS_EPS) * ng_ref[...]
            o = o * _silu(g_ref[rows(ch), cols(hd)])
        o_ref[rows(ch), cols(hd)] = o.astype(o_ref.dtype)


def _hg_scan(p_raw, kk, lf, batch, seq, ctx_len, reverse, readout_args=None):
    n_rows = p_raw.shape[0]
    tb = ctx_len
    nl = seq // tb
    ctx0 = batch * seq // tb
    direction = 1 if reverse else 0

    def row_block(b, i):
        lat = b * nl + (nl - i if reverse else i - 1)
        return jnp.where(i == 0, ctx0 + b, lat)

    def col(j):
        return pl.BlockSpec((tb, HG_W), lambda b, i: (row_block(b, i), j))

    spec = col(0)
    readout = readout_args is not None
    inputs = [p_raw, kk, lf, p_raw]
    in_specs = [col(0), col(direction), col(direction), col(1)]
    if readout:
        o_f, norm_g = readout_args
        inputs += [o_f, p_raw, norm_g]
        in_specs += [spec, col(2), pl.BlockSpec((1, HG_DK), lambda b, i: (0, 0))]
    kern = functools.partial(_hg_scan_kernel, reverse=reverse, n_chunks=tb // HG_CHUNK, readout=readout)
    return pl.pallas_call(
        kern, grid=(batch, nl + 1), in_specs=in_specs, out_specs=spec,
        out_shape=jax.ShapeDtypeStruct((n_rows, HG_W), BF16 if readout else F32),
        scratch_shapes=[pltpu.VMEM((HG_HEADS, HG_DK, HG_DK), F32)],
        compiler_params=_cparams(("arbitrary", "arbitrary")),
        name="hgrn2_scan_bwd_readout" if readout else "hgrn2_scan_fwd",
    )(*inputs)


def _mla_proj_kernel(cq_ref, ckv_ref, kr_ref, qn_ref, kvn_ref, wq_ref, wkv_ref, cm_ref, sm_ref,
                     q_ref, k_ref, v_ref):
    def rms(x, g):
        xf = x.astype(F32)
        return (xf * lax.rsqrt(jnp.mean(xf * xf, axis=-1, keepdims=True) + RMS_EPS) * g).astype(BF16)

    scale = (MLA_NOPE + MLA_ROPE) ** -0.5 * math.log2(math.e)
    _, qs = _col_group_dots(rms(cq_ref[...], qn_ref[...]), wq_ref, MLA_QK_PAD)
    _, kv2 = _col_group_dots(rms(ckv_ref[...], kvn_ref[...]), wkv_ref, 2 * LANES)
    kvs = [g[:, half * LANES:(half + 1) * LANES] for g in kv2 for half in range(2)]
    cm, sm = cm_ref[...], sm_ref[...]
    kr = kr_ref[...]
    lane = lax.broadcasted_iota(jnp.int32, (kr.shape[0], LANES), 1)
    ones_col = jnp.where(lane == 0, 1.0, 0.0).astype(v_ref.dtype)
    for hd in range(MLA_HEADS):
        c0 = hd * MLA_QK_PAD
        q_ref[:, c0:c0 + LANES] = (qs[hd][:, 0:LANES] * scale).astype(q_ref.dtype)
        x = qs[hd][:, LANES:2 * LANES]
        y = (x * cm + _rope_partner(x, MLA_ROPE // 4) * sm) * scale
        q_ref[:, c0 + LANES:c0 + 2 * LANES] = y.astype(q_ref.dtype)
        k_ref[:, c0:c0 + LANES] = kvs[hd].astype(k_ref.dtype)
        k_ref[:, c0 + LANES:c0 + 2 * LANES] = kr
        v0 = hd * MLA_V_PAD
        v_ref[:, v0:v0 + MLA_V] = kvs[MLA_HEADS + hd].astype(v_ref.dtype)
        v_ref[:, v0 + MLA_V:v0 + MLA_V_PAD] = ones_col


def _mla_proj(p_att, q_norm, kv_norm, wq, wkv, cm, sm, rows):
    tm = rows.tm
    n = rows.n_rows
    hq = MLA_HEADS * MLA_QK_PAD

    def col(width, off):
        return pl.BlockSpec((tm, width), lambda i: (i, off // width))

    def whole(a):
        return pl.BlockSpec(a.shape, lambda i: (0, 0))

    return pl.pallas_call(
        _mla_proj_kernel, grid=(rows.n_tiles,),
        in_specs=[col(MLA_Q_LORA, ATT_CQ), col(MLA_KV_LORA, ATT_CKV), col(LANES, ATT_KR),
                  whole(q_norm), whole(kv_norm), whole(wq), whole(wkv),
                  pl.BlockSpec((tm, LANES), lambda i: (i, 0)), pl.BlockSpec((tm, LANES), lambda i: (i, 0))],
        out_specs=[pl.BlockSpec((tm, hq), lambda i: (i, 0)), pl.BlockSpec((tm, hq), lambda i: (i, 0)),
                   pl.BlockSpec((tm, MLA_HEADS * MLA_V_PAD), lambda i: (i, 0))],
        out_shape=[jax.ShapeDtypeStruct((n, hq), BF16), jax.ShapeDtypeStruct((n, hq), BF16),
                   jax.ShapeDtypeStruct((n, MLA_HEADS * MLA_V_PAD), BF16)],
        compiler_params=_cparams(("arbitrary",)), name="mla_proj",
    )(p_att, p_att, p_att, q_norm, kv_norm, wq, wkv, cm, sm)


def _lane_tile_fold(x, op):
    out = x[:, 0:LANES]
    for t in range(1, x.shape[1] // LANES):
        out = op(out, x[:, t * LANES:(t + 1) * LANES])
    return out


def _mla_attn_kernel(*refs, ck, with_lat):
    if with_lat:
        q_ref, kl_ref, vl_ref, kc_ref, vc_ref, o_ref = refs
    else:
        q_ref, kc_ref, vc_ref, o_ref = refs
    chunks = [(kc_ref, vc_ref, 0, kc_ref.shape[0])]
    if with_lat:
        chunks += [(kl_ref, vl_ref, c * ck, ck) for c in range(kl_ref.shape[0] // ck)]
    m = [None] * MLA_HEADS_PER_STEP
    acc = [None] * MLA_HEADS_PER_STEP
    for k_ref, v_ref, r0, n in chunks:
        for hd in range(MLA_HEADS_PER_STEP):
            qk = slice(hd * MLA_QK_PAD, (hd + 1) * MLA_QK_PAD)
            s = _dot_nt(q_ref[:, qk], k_ref[r0:r0 + n, qk])
            m_c = jnp.max(_lane_tile_fold(s, jnp.maximum), axis=-1, keepdims=True)
            m_new = m_c if m[hd] is None else jnp.maximum(m[hd], m_c)
            pv = _dot(jnp.exp2(s - m_new).astype(BF16), v_ref[r0:r0 + n, hd * MLA_V_PAD:(hd + 1) * MLA_V_PAD])
            acc[hd] = pv if m[hd] is None else acc[hd] * jnp.exp2(m[hd] - m_new) + pv
            m[hd] = m_new
    for hd in range(MLA_HEADS_PER_STEP):
        o = acc[hd][:, 0:MLA_V] / acc[hd][:, MLA_V:MLA_V + 1]
        o_ref[:, hd * MLA_V:(hd + 1) * MLA_V] = o.astype(o_ref.dtype)


def _mla_attn(q, k, v, batch, seq, ctx_len, latent):
    ctx0 = batch * seq // ctx_len
    ck = 1024 if seq % 1024 == 0 else ctx_len
    hps = MLA_HEADS_PER_STEP
    if latent:
        tq = MLA_TQ if seq % MLA_TQ == 0 else ctx_len
        nq = seq // tq
        q_row = lambda b, h, i: b * nq + i
        o_row, n_rows = q_row, batch * seq
    else:
        tq, nq = ctx_len, 1
        q_row = lambda b, h, i: ctx0 + b
        o_row, n_rows = (lambda b, h, i: b), batch * ctx_len
    in_specs = [pl.BlockSpec((tq, hps * MLA_QK_PAD), lambda b, h, i: (q_row(b, h, i), h))]
    inputs = [q]
    if latent:
        in_specs += [pl.BlockSpec((seq, hps * MLA_QK_PAD), lambda b, h, i: (b, h)),
                     pl.BlockSpec((seq, hps * MLA_V_PAD), lambda b, h, i: (b, h))]
        inputs += [k, v]
    in_specs += [pl.BlockSpec((ctx_len, hps * MLA_QK_PAD), lambda b, h, i: (ctx0 + b, h)),
                 pl.BlockSpec((ctx_len, hps * MLA_V_PAD), lambda b, h, i: (ctx0 + b, h))]
    inputs += [k, v]
    return pl.pallas_call(
        functools.partial(_mla_attn_kernel, ck=ck, with_lat=latent),
        grid=(batch, MLA_HEADS // hps, nq), in_specs=in_specs,
        out_specs=pl.BlockSpec((tq, hps * MLA_V), lambda b, h, i: (o_row(b, h, i), h)),
        out_shape=jax.ShapeDtypeStruct((n_rows, MLA_HEADS * MLA_V), BF16),
        compiler_params=_cparams(("arbitrary", "arbitrary", "arbitrary")),
        name="mla_attn_lat" if latent else "mla_attn_ctx",
    )(*inputs)


def _swa_kernel(sink_ref, q_ref, kp_ref, kn_ref, kx_ref, kc_ref, vp_ref, vn_ref, vx_ref, vc_ref, o_ref,
                *, n_lat_steps, seq, ctx_queries):
    n = pl.program_id(1)
    g = SWA_HEADS // SWA_KV_HEADS
    blk = SWA_BLOCK
    nblk = SWA_BLOCKS_PER_STEP

    def attend(band):
        items = [(j, kh) for j in range(nblk) for kh in range(SWA_KV_HEADS)]
        ks = [slice(kh * SWA_HD, (kh + 1) * SWA_HD) for kh in range(SWA_KV_HEADS)]
        row = lax.broadcasted_iota(jnp.int32, (g * blk, 1), 0)
        if band:
            a = lax.broadcasted_iota(jnp.int32, (g * blk, 3 * blk), 0) & (blk - 1)
            jj = lax.broadcasted_iota(jnp.int32, (g * blk, 3 * blk), 1)
            in_window = jnp.abs(jj - blk - a) <= SWA_WINDOW
            k_slab = [jnp.concatenate([kp_ref[:, s], kn_ref[:, s], kx_ref[:, s]], axis=0) for s in ks]
            v_slab = [jnp.concatenate([vp_ref[:, s], vn_ref[:, s], vx_ref[:, s]], axis=0) for s in ks]
        sink = []
        for kh in range(SWA_KV_HEADS):
            sk = jnp.zeros((g * blk, 1), F32)
            for gi in range(g):
                sk = jnp.where((row >= gi * blk) & (row < (gi + 1) * blk), sink_ref[kh * g + gi], sk)
            sink.append(sk)
        s_c, s_b = {}, {}
        for it in items:
            j, kh = it
            qs = jnp.concatenate([q_ref[j * blk:(j + 1) * blk, (kh * g + gi) * SWA_HD:(kh * g + gi + 1) * SWA_HD]
                                  for gi in range(g)], axis=0)
            s_c[it] = _dot_nt(qs, kc_ref[:, ks[kh]])
            if band:
                key_pos = (n * nblk + j - 1) * blk + jj
                keep = in_window & (key_pos >= 0) & (key_pos < seq)
                s_b[it] = jnp.where(keep, _dot_nt(qs, k_slab[kh][j * blk:(j + 3) * blk]), -jnp.inf)
        p_c, p_b, den = {}, {}, {}
        for it in items:
            m = jnp.maximum(jnp.max(s_c[it], axis=-1, keepdims=True), sink[it[1]])
            if band:
                m = jnp.maximum(m, jnp.max(s_b[it], axis=-1, keepdims=True))
            pc = jnp.exp(s_c[it] - m)
            dn = jnp.sum(pc, axis=-1, keepdims=True) + jnp.exp(sink[it[1]] - m)
            if band:
                pb = jnp.exp(s_b[it] - m)
                dn = dn + jnp.sum(pb, axis=-1, keepdims=True)
                p_b[it] = pb.astype(BF16)
            p_c[it] = pc.astype(BF16)
            den[it] = dn
        for it in items:
            j, kh = it
            acc = _dot(p_c[it], vc_ref[:, ks[kh]])
            if band:
                acc = acc + _dot(p_b[it], v_slab[kh][j * blk:(j + 3) * blk])
            o = acc / den[it]
            for gi in range(g):
                c0 = (kh * g + gi) * SWA_HD
                o_ref[j * blk:(j + 1) * blk, c0:c0 + SWA_HD] = o[gi * blk:(gi + 1) * blk].astype(o_ref.dtype)

    if ctx_queries:
        pl.when(n < n_lat_steps)(lambda: attend(True))
        pl.when(n >= n_lat_steps)(lambda: attend(False))
    else:
        attend(True)


def _swa(p_att, sink, batch, seq, ctx_len, ctx_queries):
    blk = SWA_BLOCK
    nblk = SWA_BLOCKS_PER_STEP
    tq = nblk * blk
    assert seq % tq == 0 and ctx_len % tq == 0
    nb = seq // blk
    ns, ncs = seq // tq, ctx_len // tq
    ctx0 = batch * seq // ctx_len
    kcol, vcol = ATT_SK // SWA_KV_W, ATT_SV // SWA_KV_W
    q_row = lambda b, n: jnp.where(n < ns, b * ns + n, batch * ns + b * ncs + (n - ns))
    n_rows = batch * seq + (batch * ctx_len if ctx_queries else 0)

    def edge(col, first):
        def index(b, n):
            k = jnp.clip(n, 0, ns - 1) * nblk + (-1 if first else nblk)
            return (b * nb + jnp.clip(k, 0, nb - 1), col)
        return pl.BlockSpec((blk, SWA_KV_W), index)

    def own(col):
        return pl.BlockSpec((tq, SWA_KV_W), lambda b, n: (b * ns + jnp.clip(n, 0, ns - 1), col))

    ctx_k = pl.BlockSpec((ctx_len, SWA_KV_W), lambda b, n: (ctx0 + b, kcol))
    ctx_v = pl.BlockSpec((ctx_len, SWA_KV_W), lambda b, n: (ctx0 + b, vcol))
    return pl.pallas_call(
        functools.partial(_swa_kernel, n_lat_steps=ns, seq=seq, ctx_queries=ctx_queries),
        grid=(batch, ns + (ncs if ctx_queries else 0)),
        in_specs=[pl.BlockSpec(memory_space=pltpu.SMEM),
                  pl.BlockSpec((tq, SWA_W), lambda b, n: (q_row(b, n), 0)),
                  edge(kcol, True), own(kcol), edge(kcol, False), ctx_k,
                  edge(vcol, True), own(vcol), edge(vcol, False), ctx_v],
        out_specs=pl.BlockSpec((tq, SWA_W), lambda b, n: (q_row(b, n), 0)),
        out_shape=jax.ShapeDtypeStruct((n_rows, SWA_W), BF16),
        compiler_params=_cparams(("arbitrary", "arbitrary")),
        name="swa_attn",
    )(sink, *([p_att] * 9))


def _route(logits, rb):
    aff = _sigmoid(logits)
    sel = aff + rb
    tm = logits.shape[1]
    scores = []
    for g in range(N_GROUPS):
        r = [sel[g * E_PER_GROUP + j:g * E_PER_GROUP + j + 1] for j in range(E_PER_GROUP)]
        best = None
        for i in range(E_PER_GROUP):
            for j in range(i + 1, E_PER_GROUP):
                pair = r[i] + r[j]
                best = pair if best is None else jnp.maximum(best, pair)
        scores.append(best)
    gbest, gi = scores[0], jnp.zeros((1, tm), jnp.int32)
    for g in range(1, N_GROUPS):
        upd = scores[g] > gbest
        gbest = jnp.where(upd, scores[g], gbest)
        gi = jnp.where(upd, g, gi)
    s_in, a_in = [], []
    for j in range(E_PER_GROUP):
        sj, aj = sel[j:j + 1], aff[j:j + 1]
        for g in range(1, N_GROUPS):
            e = g * E_PER_GROUP + j
            sj = jnp.where(gi == g, sel[e:e + 1], sj)
            aj = jnp.where(gi == g, aff[e:e + 1], aj)
        s_in.append(sj)
        a_in.append(aj)
    chosen = []
    for j in range(E_PER_GROUP):
        rank = jnp.zeros((1, tm), jnp.int32)
        for k in range(E_PER_GROUP):
            if k == j:
                continue
            ahead = (s_in[k] >= s_in[j]) if k < j else (s_in[k] > s_in[j])
            rank = rank + ahead.astype(jnp.int32)
        chosen.append(rank < 2)
    w = [jnp.where(chosen[j], a_in[j], 0.0) for j in range(E_PER_GROUP)]
    wsum = w[0] + w[1] + w[2] + w[3]
    gate_in = [wj / wsum * ROUTE_SCALE for wj in w]
    code = sum(jnp.where(chosen[j], 1 << j, 0) for j in range(E_PER_GROUP))
    pair = jnp.zeros((1, tm), jnp.int32)
    for idx, cval in enumerate((3, 5, 9, 6, 10, 12)):
        pair = jnp.where(code == cval, idx, pair)
    bucket = gi * N_PAIRS + pair
    g_lo = jnp.zeros((1, tm), F32)
    g_hi = jnp.zeros((1, tm), F32)
    seen = jnp.zeros((1, tm), jnp.bool_)
    for j in range(E_PER_GROUP):
        g_lo = jnp.where(chosen[j] & ~seen, gate_in[j], g_lo)
        g_hi = jnp.where(chosen[j] & seen, gate_in[j], g_hi)
        seen = seen | chosen[j]
    return g_lo, g_hi, bucket


def _lat_or_ctx(lat_ref, ctx_ref, rs, lat_tiles):
    if ctx_ref is None:
        return lat_ref[rs, :]
    return jnp.where(pl.program_id(0) < lat_tiles, lat_ref[rs, :], ctx_ref[rs, :])


def _dual_specs(rows, width):
    return [pl.BlockSpec((rows.tm, width), lambda i, *_: (jnp.minimum(i, rows.lat_tiles - 1), 0)),
            pl.BlockSpec((rows.tm, width), lambda i, *_: (jnp.maximum(i - rows.lat_tiles, 0), 0),
                         pipeline_mode=pl.Buffered(1))]


def _out_kernel(*refs, alpha, dual, lat_tiles):
    if dual:
        (ohg_ref, omla_ref, omlac_ref, oswa_ref, x_ref, xc_ref, g1_ref, sh2_ref, sc2_ref, lng_ref, lnb_ref,
         w_ref, rwt_ref, rb_ref, x1_ref, h2_ref, bucket_ref) = refs
    else:
        (ohg_ref, omla_ref, oswa_ref, x_ref, g1_ref, sh2_ref, sc2_ref, lng_ref, lnb_ref,
         w_ref, rwt_ref, rb_ref, x1_ref, h2_ref, bucket_ref) = refs
        omlac_ref = xc_ref = None
    o1, o2 = HG_W, HG_W + MLA_HEADS * MLA_V
    d = x_ref.shape[1]
    tm = x_ref.shape[0]
    halves = [slice(0, tm // 2), slice(tm // 2, tm)]
    mixes = [_dot(ohg_ref[rs, :], w_ref[0:o1, :])
             + _dot(_lat_or_ctx(omla_ref, omlac_ref, rs, lat_tiles), w_ref[o1:o2, :])
             + _dot(oswa_ref[rs, :], w_ref[o2:, :]) for rs in halves]
    for rs, mix in zip(halves, mixes):
        x_in = _lat_or_ctx(x_ref, xc_ref, rs, lat_tiles)
        x1 = _ln_rows(alpha * x_in + g1_ref[...] * mix) * lng_ref[...] + lnb_ref[...]
        x1_ref[rs, :] = x1
        h2 = (_ln_rows(x1) * (1.0 + sc2_ref[...]) + sh2_ref[...]).astype(BF16)
        h2_ref[rs, 0:d] = h2.astype(h2_ref.dtype)
        g_lo, g_hi, bucket = _route(_dot_nt(rwt_ref[...], h2), rb_ref[...])
        bucket_ref[:, rs] = bucket
        gate_rows = jnp.concatenate([g_lo, g_hi, jnp.zeros((LANES - 2, tm // 2), F32)], axis=0)
        h2_ref[rs, d:d + LANES] = gate_rows.T


def _out_proj(o_hg, o_mla, o_swa, x, mod, ln_g, ln_b, w_out, rwt, rb, rows, alpha):
    tm = rows.tm
    dual = isinstance(x, tuple)
    d = (x[0] if dual else x).shape[1]
    n = rows.n_rows
    row = lambda w: pl.BlockSpec((tm, w), lambda i: (i, 0))
    whole = lambda a: pl.BlockSpec(a.shape, lambda i: (0, 0), pipeline_mode=pl.Buffered(1))
    mla_w = MLA_HEADS * MLA_V
    if dual:
        in_specs = [row(HG_W)] + _dual_specs(rows, mla_w) + [row(SWA_W)] + _dual_specs(rows, d)
        inputs = [o_hg, *o_mla, o_swa, *x]
    else:
        in_specs = [row(HG_W), row(mla_w), row(SWA_W), row(d)]
        inputs = [o_hg, o_mla, o_swa, x]
    in_specs += [_mod_spec(rows, 2, d), _mod_spec(rows, 3, d), _mod_spec(rows, 4, d),
                 whole(ln_g), whole(ln_b), whole(w_out), whole(rwt), whole(rb)]
    inputs += [mod, mod, mod, ln_g, ln_b, w_out, rwt, rb]
    return pl.pallas_call(
        functools.partial(_out_kernel, alpha=alpha, dual=dual, lat_tiles=rows.lat_tiles), grid=(rows.n_tiles,),
        in_specs=in_specs,
        out_specs=[row(d), row(d + LANES), pl.BlockSpec((1, tm), lambda i: (0, i))],
        out_shape=[jax.ShapeDtypeStruct((n, d), F32), jax.ShapeDtypeStruct((n, d + LANES), F32),
                   jax.ShapeDtypeStruct((1, n), jnp.int32)],
        compiler_params=_cparams(("arbitrary",)), name="out_proj_ln_router",
    )(*inputs)


def _row_gather_start(idx_ref, base, src_hbm, dst, sem, n, static_rows=False):
    def issue(r, carry):
        pltpu.make_async_copy(src_hbm.at[pl.ds(idx_ref[base + r], 1)], dst.at[pl.ds(r, 1)], sem).start()
        return carry
    if static_rows:
        for r in range(n):
            issue(r, 0)
    else:
        lax.fori_loop(0, n, issue, 0, unroll=8)


def _row_gather_start_next(t, n_valid, idx_ref, base, src_hbm, buf, sem, n):
    for half in range(2):
        @pl.when((t < n_valid) & (t % 2 == half))
        def _():
            _row_gather_start(idx_ref, base, src_hbm, buf.at[half], sem.at[half], n, static_rows=True)


def _row_gather_wait(src_hbm, dst, sem, n):
    pltpu.make_async_copy(src_hbm.at[pl.ds(0, n)], dst, sem).wait()


def _moe_kernel(stok_ref, off_ref, e1_ref, e2_ref, nused_ref, h_hbm, wg1_ref, wu1_ref, wd1_ref,
                wg2_ref, wu2_ref, wd2_ref, y_ref, hbuf, sem, *, tm):
    i = pl.program_id(0)
    n_used = nused_ref[0]
    d = y_ref.shape[1]

    @pl.when(i == 0)
    def _():
        _row_gather_start(stok_ref, off_ref[0], h_hbm, hbuf.at[0], sem.at[0], tm)

    @pl.when(i < n_used)
    def _():
        slot = i % 2
        _row_gather_wait(h_hbm, hbuf.at[slot], sem.at[slot], tm)
        h = hbuf[slot, :, 0:d].astype(BF16)
        g_lo = hbuf[slot, :, d:d + 1]
        g_hi = hbuf[slot, :, d + 1:d + 2]

        act1 = (_silu(_dot(h, wg1_ref[...])) * _dot(h, wu1_ref[...])).astype(BF16)
        act2 = (_silu(_dot(h, wg2_ref[...])) * _dot(h, wu2_ref[...])).astype(BF16)
        nxt = jnp.minimum(i + 1, n_used - 1)
        _row_gather_start(stok_ref, off_ref[nxt], h_hbm, hbuf.at[1 - slot], sem.at[1 - slot], tm, static_rows=True)
        y_ref[...] = g_lo * _dot(act1, wd1_ref[...]) + g_hi * _dot(act2, wd2_ref[...])

    @pl.when(i == n_used - 1)
    def _():
        other = 1 - i % 2
        _row_gather_wait(h_hbm, hbuf.at[other], sem.at[other], tm)

    @pl.when(i >= n_used)
    def _():
        y_ref[...] = jnp.zeros_like(y_ref)


def _moe(h2, stok, off, e1, e2, n_used, wg, wu, wd, layer, tm):
    n_tiles = off.shape[0]
    dx = h2.shape[1]
    d, f = wg.shape[2], wg.shape[3]
    last = lambda i, nu: jnp.minimum(i, nu[0] - 1)
    wspec_in = lambda sel: pl.BlockSpec(
        (None, None, d, f), lambda i, s, o, a, b, nu: (layer, (a, b)[sel][last(i, nu)], 0, 0))
    wspec_dn = lambda sel: pl.BlockSpec(
        (None, None, f, d), lambda i, s, o, a, b, nu: (layer, (a, b)[sel][last(i, nu)], 0, 0))
    grid_spec = pltpu.PrefetchScalarGridSpec(
        num_scalar_prefetch=5, grid=(n_tiles,),
        in_specs=[pl.BlockSpec(memory_space=pl.ANY),
                  wspec_in(0), wspec_in(0), wspec_dn(0), wspec_in(1), wspec_in(1), wspec_dn(1)],
        out_specs=pl.BlockSpec((tm, d), lambda i, s, o, a, b, nu: (i, 0)),
        scratch_shapes=[pltpu.VMEM((2, tm, dx), F32), pltpu.SemaphoreType.DMA((2,))])
    return pl.pallas_call(
        functools.partial(_moe_kernel, tm=tm), grid_spec=grid_spec,
        out_shape=jax.ShapeDtypeStruct((n_tiles * tm, d), F32),
        compiler_params=_cparams(("arbitrary",)), name="moe_grouped",
    )(stok, off, e1, e2, n_used, h2, wg, wu, wd, wg, wu, wd)


def _moe_plan(bucket, n_tokens, tm):
    tok = jnp.arange(n_tokens, dtype=jnp.int32)
    buckets = jnp.arange(N_BUCKETS, dtype=jnp.int32)
    skey, stok = lax.sort((bucket * n_tokens + tok, tok), num_keys=1)
    counts = jnp.sum((bucket[:, None] == buckets[None, :]).astype(jnp.int32), axis=0)
    padded = (counts + tm - 1) // tm * tm
    ends = jnp.cumsum(padded)
    shift = (ends - padded) - (jnp.cumsum(counts) - counts)
    sbucket = skey // n_tokens
    slot = tok + jnp.sum(jnp.where(sbucket[:, None] == buckets[None, :], shift[None, :], 0), axis=1)
    _, pos = lax.sort((stok, slot), num_keys=1)
    n_tiles = n_tokens // tm + N_BUCKETS
    tile_start = jnp.arange(n_tiles, dtype=jnp.int32) * tm
    tile_bucket = jnp.sum((ends[None, :] <= tile_start[:, None]).astype(jnp.int32), axis=1)
    tile_bucket = jnp.minimum(tile_bucket, N_BUCKETS - 1)
    tshift = jnp.sum(jnp.where(tile_bucket[:, None] == buckets[None, :], shift[None, :], 0), axis=1)
    off = jnp.clip(tile_start - tshift, 0, n_tokens)
    grp, pair = tile_bucket // N_PAIRS, tile_bucket % N_PAIRS
    lo = jnp.where(pair < 3, 0, jnp.where(pair < 5, 1, 2))
    hi = jnp.where(pair == 0, 1, jnp.where((pair == 1) | (pair == 3), 2, 3))
    e1, e2 = grp * E_PER_GROUP + lo, grp * E_PER_GROUP + hi
    n_used = (ends[-1] // tm).astype(jnp.int32).reshape(1)
    stok = jnp.concatenate([stok, jnp.zeros((tm,), jnp.int32)])
    return stok, off.astype(jnp.int32), pos.astype(jnp.int32), e1.astype(jnp.int32), e2.astype(jnp.int32), n_used


def _ln2_kernel(pos_ref, y_hbm, x1_ref, g2_ref, lng_ref, lnb_ref, *rest, alpha, tm, emit_h):
    if emit_h:
        sh_ref, sc_ref, x2_ref, h_ref, ybuf, sem = rest
    else:
        x2_ref, ybuf, sem = rest
    i = pl.program_id(0)
    n_tiles = pl.num_programs(0)

    @pl.when(i == 0)
    def _():
        _row_gather_start(pos_ref, 0, y_hbm, ybuf.at[0], sem.at[0], tm)

    _row_gather_start_next(i + 1, n_tiles, pos_ref, (i + 1) * tm, y_hbm, ybuf, sem, tm)

    slot = i % 2
    _row_gather_wait(y_hbm, ybuf.at[slot], sem.at[slot], tm)
    x2 = _ln_rows(alpha * x1_ref[...] + g2_ref[...] * ybuf[slot]) * lng_ref[...] + lnb_ref[...]
    x2_ref[...] = x2
    if emit_h:
        h_ref[...] = (_ln_rows(x2) * (1.0 + sc_ref[...]) + sh_ref[...]).astype(h_ref.dtype)


def _ln2(pos, y_sorted, x1, mod, mod_next, ln_g, ln_b, rows, alpha):
    tm = rows.tm
    d = x1.shape[1]
    emit_h = mod_next is not None
    row = pl.BlockSpec((tm, d), lambda i, p: (i, 0))
    whole = lambda a: pl.BlockSpec(a.shape, lambda i, p: (0, 0))
    in_specs = [pl.BlockSpec(memory_space=pl.ANY), row, _mod_spec(rows, 5, d), whole(ln_g), whole(ln_b)]
    inputs = [y_sorted, x1, mod, ln_g, ln_b]
    out_specs = [row]
    out_shape = [jax.ShapeDtypeStruct((rows.n_rows, d), F32)]
    if emit_h:
        in_specs += [_mod_spec(rows, 0, d), _mod_spec(rows, 1, d)]
        inputs += [mod_next, mod_next]
        out_specs.append(row)
        out_shape.append(jax.ShapeDtypeStruct((rows.n_rows, d), BF16))
    grid_spec = pltpu.PrefetchScalarGridSpec(
        num_scalar_prefetch=1, grid=(rows.n_tiles,), in_specs=in_specs, out_specs=out_specs,
        scratch_shapes=[pltpu.VMEM((2, tm, d), F32), pltpu.SemaphoreType.DMA((2,))])
    out = pl.pallas_call(
        functools.partial(_ln2_kernel, alpha=alpha, tm=tm, emit_h=emit_h), grid_spec=grid_spec,
        out_shape=out_shape, compiler_params=_cparams(("arbitrary",)), name="unpermute_ln2",
    )(pos, *inputs)
    return out if emit_h else (out[0], None)


def _rope_tables(seq, batch, ctx_rows, dim, pad_to):
    rows = seq // GRID_W
    row = jnp.repeat(jnp.arange(rows, dtype=jnp.int32), GRID_W)
    col = jnp.tile(jnp.arange(GRID_W, dtype=jnp.int32), rows)
    nf = dim // 4
    inv_freq = ROPE_BASE ** (-jnp.arange(nf, dtype=F32) / nf)
    ang = jnp.stack([row, col], -1).astype(F32)[:, :, None] * inv_freq
    cos, sin = jnp.cos(ang), jnp.sin(ang)
    c = jnp.stack([cos, cos], axis=2).reshape(seq, dim)
    s = jnp.stack([-sin, sin], axis=2).reshape(seq, dim)
    if pad_to > dim:
        c = jnp.pad(c, ((0, 0), (0, pad_to - dim)))
        s = jnp.pad(s, ((0, 0), (0, pad_to - dim)))
    ctx_c = jnp.zeros((ctx_rows, pad_to), F32).at[:, :dim].set(1.0)
    c = jnp.concatenate([jnp.tile(c, (batch, 1)), ctx_c], axis=0)
    s = jnp.concatenate([jnp.tile(s, (batch, 1)), jnp.zeros((ctx_rows, pad_to), F32)], axis=0)
    return c, s


def _w_in_prep_kernel(w_ref, raw_ref, gate_ref, att_ref):
    o1 = 5 * HG_W
    o2 = o1 + MLA_Q_LORA + MLA_KV_LORA
    o3 = o2 + MLA_ROPE
    cast = lambda a, b: w_ref[:, a:b].astype(BF16)
    raw_ref[:, 0:HG_W] = cast(0, HG_W)
    raw_ref[:, HG_W:3 * HG_W] = cast(3 * HG_W, o1)
    gate_ref[...] = cast(HG_W, 3 * HG_W)
    att_ref[:, ATT_SQ:ATT_CQ] = cast(o3, o3 + SWA_W + 2 * SWA_KV_W)
    att_ref[:, ATT_CQ:ATT_KR] = cast(o1, o2)
    att_ref[:, ATT_KR:ATT_KR + MLA_ROPE] = cast(o2, o3)
    att_ref[:, ATT_KR + MLA_ROPE:ATT_COLS] = jnp.zeros((w_ref.shape[0], LANES - MLA_ROPE), BF16)


def _w_in_prep(w_in):
    depth, d, n = w_in.shape
    tr = 256 if d % 256 == 0 else d
    widths = (3 * HG_W, 2 * HG_W, ATT_COLS)
    return pl.pallas_call(
        _w_in_prep_kernel, grid=(depth, d // tr),
        in_specs=[pl.BlockSpec((None, tr, n), lambda l, i: (l, i, 0))],
        out_specs=[pl.BlockSpec((None, tr, w), lambda l, i: (l, i, 0)) for w in widths],
        out_shape=[jax.ShapeDtypeStruct((depth, d, w), BF16) for w in widths],
        compiler_params=_cparams(("arbitrary", "arbitrary")), name="w_in_prep",
    )(w_in)


def _mla_weights(w_uq, w_ukv):
    qk = MLA_NOPE + MLA_ROPE
    wq = w_uq.reshape(MLA_Q_LORA, MLA_HEADS, qk)
    wq = jnp.pad(wq, ((0, 0), (0, 0), (0, MLA_QK_PAD - qk))).reshape(MLA_Q_LORA, MLA_HEADS * MLA_QK_PAD)
    wkv = w_ukv.reshape(MLA_KV_LORA, MLA_HEADS, MLA_NOPE + MLA_V)
    wkv = jnp.concatenate([wkv[:, :, :MLA_NOPE].reshape(MLA_KV_LORA, -1), wkv[:, :, MLA_NOPE:].reshape(MLA_KV_LORA, -1)], axis=1)
    return wq.astype(BF16), wkv.astype(BF16)


def kernel(x, c, ctx, c_ctx, w_ada, b_ada, w_in, w_out, hg_lb_logits, hg_norm_g, mla_q_norm, mla_kv_norm,
           mla_w_uq, mla_w_ukv, swa_sink, ln1_g, ln1_b, ln2_g, ln2_b, router_w, router_b,
           moe_w_gate, moe_w_up, moe_w_down):
    batch, seq, d = x.shape
    ctx_len = ctx.shape[1]
    depth = w_ada.shape[0]
    alpha = (2.0 * depth) ** 0.25
    n_lat, n_ctx = batch * seq, batch * ctx_len
    n_all = n_lat + n_ctx
    assert batch + 1 <= 8 and ctx_len % (4 * HG_CHUNK) == 0 and seq % ctx_len == 0 and seq % GRID_W == 0

    c8 = jnp.zeros((8, d), F32).at[:batch].set(c).at[batch].set(c_ctx)
    mod_all = _ada(c8, w_ada, b_ada).reshape(depth, 8 * 6, 1, d)

    lb = jnp.cumsum(jax.nn.softmax(hg_lb_logits.astype(F32), axis=0), axis=0)
    lb = (lb - lb[0:1]).reshape(depth, 1, 2 * HG_W)
    log_lb, log_1m, one_m = jnp.log(lb), jnp.log1p(-lb), 1.0 - lb

    cs, ss = _rope_tables(seq, batch, n_ctx, SWA_HD, SWA_HD)
    cm, sm = _rope_tables(seq, batch, n_ctx, MLA_ROPE, LANES)
    rwt = router_w.T.astype(BF16)
    rb = router_b.astype(F32).reshape(N_EXPERTS, 1)

    tm_all = _pick_tm(seq, n_ctx, 512)
    rows_all = _Rows(batch, seq, ctx_len, n_all, tm_all)
    rows_lat = _Rows(batch, seq, ctx_len, n_lat, tm_all)
    moe_tm = 256

    xa = (x.reshape(n_lat, d), ctx.reshape(n_ctx, d))
    h = _lnmod(xa[0], xa[1], mod_all[0], rows_all, 0)
    w_raw_all, w_gate_all, w_att_all = _w_in_prep(w_in)
    wg_all, wu_all, wd_all = moe_w_gate.astype(BF16), moe_w_up.astype(BF16), moe_w_down.astype(BF16)

    for layer in range(depth):
        need_ctx = layer < depth - 1
        mod = mod_all[layer]
        rows = rows_all if need_ctx else rows_lat
        wq, wkv = _mla_weights(mla_w_uq[layer], mla_w_ukv[layer])

        lf, kk, p_att, p_raw = _in_proj(h, w_gate_all, w_att_all, w_raw_all, layer, log_lb[layer], log_1m[layer],
                                        one_m[layer], [cs, ss, cm, sm], rows_all)

        o_f = _hg_scan(p_raw, kk, lf, batch, seq, ctx_len, False)
        o_hg = _hg_scan(p_raw, kk, lf, batch, seq, ctx_len, True,
                        (o_f, hg_norm_g[layer].astype(F32).reshape(1, HG_DK)))

        q_mla, k_mla, v_mla = _mla_proj(p_att, mla_q_norm[layer].astype(F32).reshape(1, -1),
                                        mla_kv_norm[layer].astype(F32).reshape(1, -1), wq, wkv, cm, sm, rows_all)
        n_out = rows.n_rows
        o_mla = _mla_attn(q_mla, k_mla, v_mla, batch, seq, ctx_len, True)
        o_swa = _swa(p_att, swa_sink[layer].astype(F32), batch, seq, ctx_len, need_ctx)
        dual = isinstance(xa, tuple)
        if need_ctx:
            o_mla = (o_mla, _mla_attn(q_mla, k_mla, v_mla, batch, seq, ctx_len, False))
            if not dual:
                xa = (xa[:n_lat], xa[n_lat:])
        elif dual:
            xa = xa[0]

        x1, h2, bucket = _out_proj(
            o_hg, o_mla, o_swa, xa, mod, ln1_g[layer].astype(F32).reshape(1, d), ln1_b[layer].astype(F32).reshape(1, d),
            w_out[layer].astype(BF16), rwt, rb, rows, alpha)

        stok, off, pos, e1, e2, n_used = _moe_plan(bucket[0], n_out, moe_tm)
        y_sorted = _moe(h2, stok, off, e1, e2, n_used, wg_all, wu_all, wd_all, layer, moe_tm)
        xa, h = _ln2(pos, y_sorted, x1, mod, mod_all[layer + 1] if need_ctx else None,
                     ln2_g[layer].astype(F32).reshape(1, d), ln2_b[layer].astype(F32).reshape(1, d), rows, alpha)

    return xa[:n_lat].reshape(batch, seq, d)
```

```python
import functools
import math

import jax
import jax.numpy as jnp
from jax import lax
from jax.experimental import pallas as pl
from jax.experimental.pallas import tpu as pltpu

F32 = jnp.float32
BF16 = jnp.bfloat16

GRID_W = 64
HG_HEADS = 4
HG_DK = 128
HG_W = HG_HEADS * HG_DK
HG_CHUNK = 64
HG_SUB = 16
MLA_HEADS = 8
MLA_Q_LORA = 512
MLA_KV_LORA = 256
MLA_NOPE = 128
MLA_ROPE = 64
MLA_V = 128
MLA_QK_PAD = 256
MLA_V_PAD = 256
MLA_HEADS_PER_STEP = 4
MLA_TQ = 1024
SWA_HEADS = 4
SWA_KV_HEADS = 2
SWA_HD = 128
SWA_W = SWA_HEADS * SWA_HD
SWA_KV_W = SWA_KV_HEADS * SWA_HD
SWA_WINDOW = 128
SWA_BLOCK = 128
SWA_BLOCKS_PER_STEP = 2
N_EXPERTS = 16
N_GROUPS = 4
E_PER_GROUP = 4
N_PAIRS = 6
N_BUCKETS = N_GROUPS * N_PAIRS
ROUTE_SCALE = 2.5
ROPE_BASE = 10000.0
LN_EPS = 1e-5
RMS_EPS = 1e-6
LANES = 128
VMEM_LIMIT = 56 * 1024 * 1024

ATT_SQ, ATT_SK, ATT_SV = 0, SWA_W, SWA_W + SWA_KV_W
ATT_CQ = SWA_W + 2 * SWA_KV_W
ATT_CKV = ATT_CQ + MLA_Q_LORA
ATT_KR = ATT_CKV + MLA_KV_LORA
ATT_COLS = ATT_KR + LANES


def _cparams(sem):
    return pltpu.CompilerParams(dimension_semantics=sem, vmem_limit_bytes=VMEM_LIMIT)


def _dot(a, b):
    return jnp.dot(a, b, preferred_element_type=F32)


def _dot_nt(a, b):
    return lax.dot_general(a, b, (((1,), (1,)), ((), ())), preferred_element_type=F32)


def _dot_tn(a, b):
    return lax.dot_general(a, b, (((0,), (0,)), ((), ())), preferred_element_type=F32)


def _sigmoid(x):
    return 1.0 / (1.0 + jnp.exp(-x))


def _silu(x):
    return x * _sigmoid(x)


def _ln_rows(x):
    mu = jnp.mean(x, axis=-1, keepdims=True)
    xc = x - mu
    var = jnp.mean(xc * xc, axis=-1, keepdims=True)
    return xc * lax.rsqrt(var + LN_EPS)


def _rope_partner(x, half):
    lane = lax.broadcasted_iota(jnp.int32, x.shape, x.ndim - 1)
    first = (lane & half) == 0
    n = x.shape[-1]
    return jnp.where(first, pltpu.roll(x, n - half, x.ndim - 1), pltpu.roll(x, half, x.ndim - 1))


def _ada_kernel(c_ref, w_ref, b_ref, o_ref):
    s = _silu(c_ref[...])
    o_ref[...] = _dot(s.astype(BF16), w_ref[...].astype(BF16)) + b_ref[...]


def _ada(c8, w_ada, b_ada):
    depth, d, n = w_ada.shape
    tn = 1024 if n % 1024 == 0 else n
    return pl.pallas_call(
        _ada_kernel,
        grid=(depth, n // tn),
        in_specs=[pl.BlockSpec((8, d), lambda l, j: (0, 0)),
                  pl.BlockSpec((None, d, tn), lambda l, j: (l, 0, j)),
                  pl.BlockSpec((None, 1, tn), lambda l, j: (l, 0, j))],
        out_specs=pl.BlockSpec((None, 8, tn), lambda l, j: (l, 0, j)),
        out_shape=jax.ShapeDtypeStruct((depth, 8, n), F32),
        compiler_params=_cparams(("arbitrary", "arbitrary")),
        name="ada_mod",
    )(c8, w_ada, b_ada.reshape(depth, 1, n))


class _Rows:
    def __init__(self, batch, seq, ctx_len, n_rows, tm):
        self.batch, self.seq, self.ctx_len, self.n_rows, self.tm = batch, seq, ctx_len, n_rows, tm
        self.n_tiles = n_rows // tm
        self.lat_tiles = batch * seq // tm
        self.tiles_per_batch = seq // tm

    def mod_row(self, i):
        return jnp.where(i < self.lat_tiles, i // self.tiles_per_batch, self.batch)


def _pick_tm(seq, ctx_rows, cap):
    for tm in (1024, 512, 256, 128):
        if tm <= cap and seq % tm == 0 and ctx_rows % tm == 0:
            return tm
    raise ValueError("unsupported sequence / context lengths")


def _mod_spec(rows, chunk, d):
    return pl.BlockSpec((None, 1, d), lambda i, *_: (rows.mod_row(i) * 6 + chunk, 0, 0))


def _lnmod_kernel(x_ref, xc_ref, sh_ref, sc_ref, h_ref, *, lat_tiles):
    y = _ln_rows(_lat_or_ctx(x_ref, xc_ref, slice(None), lat_tiles))
    h_ref[...] = (y * (1.0 + sc_ref[...]) + sh_ref[...]).astype(h_ref.dtype)


def _lnmod(x_lat, x_ctx, mod, rows, chunk0):
    d = x_lat.shape[1]
    tm = rows.tm
    return pl.pallas_call(
        functools.partial(_lnmod_kernel, lat_tiles=rows.lat_tiles),
        grid=(rows.n_tiles,),
        in_specs=_dual_specs(rows, d) + [_mod_spec(rows, chunk0, d), _mod_spec(rows, chunk0 + 1, d)],
        out_specs=pl.BlockSpec((tm, d), lambda i: (i, 0)),
        out_shape=jax.ShapeDtypeStruct((rows.n_rows, d), BF16),
        compiler_params=_cparams(("arbitrary",)),
        name="ln_modulate",
    )(x_lat, x_ctx, mod, mod)


def _col_group_dots(h, w_ref, width):
    n = w_ref.shape[1]
    groups = [slice(c0, min(c0 + width, n)) for c0 in range(0, n, width)]
    return groups, [_dot(h, w_ref[:, g]) for g in groups]


def _in_proj_kernel(h_ref, wgate_ref, watt_ref, wraw_ref, loglb_ref, log1m_ref, onem_ref,
                    cs_ref, ss_ref, cm_ref, sm_ref, lf_ref, k_ref, att_ref, raw_ref):
    h = h_ref[...]
    g_groups, zs = _col_group_dots(h, wgate_ref, 2 * LANES)
    a_groups, ps = _col_group_dots(h, watt_ref, 2 * LANES)
    r_groups, rs = _col_group_dots(h, wraw_ref, 2 * LANES)
    for g, z in zip(g_groups, zs):
        e = jnp.exp(-jnp.abs(z))
        r = 1.0 / (1.0 + e)
        log_sig = jnp.minimum(z, 0.0) + jnp.log(r)
        sig_neg = jnp.where(z >= 0.0, e * r, r)
        a = loglb_ref[:, g]
        b = log1m_ref[:, g] + log_sig
        lf_ref[:, g] = jnp.maximum(a, b) + jnp.log(1.0 + jnp.exp(-jnp.abs(a - b)))
        k_ref[:, g] = onem_ref[:, g] * sig_neg
    cs, ss = cs_ref[...], ss_ref[...]
    swa_scale = SWA_HD ** -0.5
    for g, p in zip(a_groups, ps):
        for c0 in range(g.start, g.stop, LANES):
            x = p[:, c0 - g.start:c0 - g.start + LANES]
            if c0 < ATT_SV:
                x = x * cs + _rope_partner(x, SWA_HD // 4) * ss
                if c0 < ATT_SK:
                    x = x * swa_scale
            elif c0 == ATT_KR:
                x = x * cm_ref[...] + _rope_partner(x, MLA_ROPE // 4) * sm_ref[...]
            att_ref[:, c0:c0 + LANES] = x.astype(att_ref.dtype)
    for g, r in zip(r_groups, rs):
        raw_ref[:, g] = r


def _in_proj(h, w_gate_all, w_att_all, w_raw_all, layer, log_lb, log_1m, one_m, tables, rows):
    tm = rows.tm
    d = h.shape[1]
    n = rows.n_rows
    single = lambda spec: pl.BlockSpec(spec.block_shape, spec.index_map, pipeline_mode=pl.Buffered(1))
    consts = [(w_gate_all, layer), (w_att_all, layer), (w_raw_all, layer), log_lb, log_1m, one_m]
    row = lambda w: pl.BlockSpec((tm, w), lambda i: (i, 0))
    widths = (2 * HG_W, 2 * HG_W, ATT_COLS, 3 * HG_W)
    dtypes = (F32, F32, BF16, F32)
    return pl.pallas_call(
        _in_proj_kernel, grid=(rows.n_tiles,),
        in_specs=[row(d)] + [single(_const_spec(a)) for a in consts] + [row(LANES)] * 4,
        out_specs=[row(w) for w in widths],
        out_shape=[jax.ShapeDtypeStruct((n, w), dt) for w, dt in zip(widths, dtypes)],
        compiler_params=_cparams(("arbitrary",)), name="in_proj",
    )(h, w_gate_all, w_att_all, w_raw_all, log_lb, log_1m, one_m, *tables)


def _const_spec(a):
    if isinstance(a, tuple):
        stacked, layer = a
        return pl.BlockSpec((None,) + stacked.shape[1:], lambda i, *_: (layer,) + (0,) * (stacked.ndim - 1))
    return pl.BlockSpec(a.shape, lambda i, *_: (0,) * a.ndim)


def _hg_scan_kernel(*refs, reverse, n_chunks, readout):
    if readout:
        q_ref, k_ref, lf_ref, v_ref, of_ref, g_ref, ng_ref, o_ref, st_ref = refs
    else:
        q_ref, k_ref, lf_ref, v_ref, o_ref, st_ref = refs

    @pl.when(pl.program_id(1) == 0)
    def _():
        st_ref[...] = jnp.zeros_like(st_ref)

    c, s = HG_CHUNK, HG_SUB
    nsub = c // s
    chunks = [n_chunks - 1 - cc if reverse else cc for cc in range(n_chunks)]
    items = [(ch, hd) for ch in chunks for hd in range(HG_HEADS)]
    rows = lambda ch: slice(ch * c, (ch + 1) * c)
    cols = lambda hd: slice(hd * HG_DK, (hd + 1) * HG_DK)

    ri = lax.broadcasted_iota(jnp.int32, (c, c), 0)
    ci = lax.broadcasted_iota(jnp.int32, (c, c), 1)
    tri = jnp.where((ci >= ri) if reverse else (ci <= ri), 1.0, 0.0).astype(BF16)
    b_all = {}
    for ch in chunks:
        lf = lf_ref[rows(ch), :]
        hi = lf.astype(BF16)
        r1 = lf - hi.astype(F32)
        mid = r1.astype(BF16)
        lo = (r1 - mid.astype(F32)).astype(BF16)
        b_all[ch] = _dot(tri, hi) + _dot(tri, mid) + _dot(tri, lo)

    qe, k_dec, decay, v16, qt, kt = {}, {}, {}, {}, {}, {}
    for it in items:
        ch, hd = it
        b = b_all[ch][:, cols(hd)]
        q, k = q_ref[rows(ch), cols(hd)], k_ref[rows(ch), cols(hd)]
        btot = b[0:1, :] if reverse else b[c - 1:c, :]
        qe[it] = (q * jnp.exp(b)).astype(BF16)
        k_dec[it] = (k * jnp.exp(btot - b)).astype(BF16)
        decay[it] = jnp.exp(btot)
        v16[it] = v_ref[rows(ch), cols(hd)].astype(BF16)
        for i in range(nsub):
            r0 = i * s
            if reverse:
                k0, k1 = r0, c
                ref = b[r0 + s:r0 + s + 1, :] if i < nsub - 1 else jnp.zeros_like(btot)
            else:
                k0, k1 = 0, r0 + s
                ref = b[r0 - 1:r0, :] if i > 0 else jnp.zeros_like(btot)
            qt[it, i] = (q[r0:r0 + s] * jnp.exp(b[r0:r0 + s] - ref)).astype(BF16)
            kt[it, i] = (k[k0:k1] * jnp.exp(ref - b[k0:k1])).astype(BF16)

    upd = {it: _dot_tn(v16[it], k_dec[it]) for it in items}
    att = {}
    for it in items:
        for i in range(nsub):
            r0 = i * s
            k0 = r0 if reverse else 0
            a = _dot_nt(qt[it, i], kt[it, i])
            rr = lax.broadcasted_iota(jnp.int32, a.shape, 0) + r0
            cc = lax.broadcasted_iota(jnp.int32, a.shape, 1) + k0
            att[it, i] = jnp.where((cc >= rr) if reverse else (cc <= rr), a, 0.0).astype(BF16)

    states = [st_ref[hd] for hd in range(HG_HEADS)]
    o_inter = {}
    for it in items:
        ch, hd = it
        o_inter[it] = _dot_nt(qe[it], states[hd].astype(BF16))
        states[hd] = states[hd] * decay[it] + upd[it]
    for hd in range(HG_HEADS):
        st_ref[hd] = states[hd]

    for it in items:
        ch, hd = it
        outs = []
        for i in range(nsub):
            r0 = i * s
            k0, k1 = (r0, c) if reverse else (0, r0 + s)
            outs.append(o_inter[it][r0:r0 + s] + _dot(att[it, i], v16[it][k0:k1]))
        o = jnp.concatenate(outs, axis=0)
        if readout:
            o = o + of_ref[rows(ch), cols(hd)]
            o = o * lax.rsqrt(jnp.mean(o * o, axis=-1, keepdims=True) + RMS_EPS) * ng_ref[...]
            o = o * _silu(g_ref[rows(ch), cols(hd)])
        o_ref[rows(ch), cols(hd)] = o.astype(o_ref.dtype)


def _hg_scan(p_raw, kk, lf, batch, seq, ctx_len, reverse, readout_args=None):
    n_rows = p_raw.shape[0]
    tb = ctx_len
    nl = seq // tb
    ctx0 = batch * seq // tb
    direction = 1 if reverse else 0

    def row_block(b, i):
        lat = b * nl + (nl - i if reverse else i - 1)
        return jnp.where(i == 0, ctx0 + b, lat)

    def col(j):
        return pl.BlockSpec((tb, HG_W), lambda b, i: (row_block(b, i), j))

    spec = col(0)
    readout = readout_args is not None
    inputs = [p_raw, kk, lf, p_raw]
    in_specs = [col(0), col(direction), col(direction), col(1)]
    if readout:
        o_f, norm_g = readout_args
        inputs += [o_f, p_raw, norm_g]
        in_specs += [spec, col(2), pl.BlockSpec((1, HG_DK), lambda b, i: (0, 0))]
    kern = functools.partial(_hg_scan_kernel, reverse=reverse, n_chunks=tb // HG_CHUNK, readout=readout)
    return pl.pallas_call(
        kern, grid=(batch, nl + 1), in_specs=in_specs, out_specs=spec,
        out_shape=jax.ShapeDtypeStruct((n_rows, HG_W), BF16 if readout else F32),
        scratch_shapes=[pltpu.VMEM((HG_HEADS, HG_DK, HG_DK), F32)],
        compiler_params=_cparams(("arbitrary", "arbitrary")),
        name="hgrn2_scan_bwd_readout" if readout else "hgrn2_scan_fwd",
    )(*inputs)


def _mla_proj_kernel(cq_ref, ckv_ref, kr_ref, qn_ref, kvn_ref, wq_ref, wkv_ref, cm_ref, sm_ref,
                     q_ref, k_ref, v_ref):
    def rms(x, g):
        xf = x.astype(F32)
        return (xf * lax.rsqrt(jnp.mean(xf * xf, axis=-1, keepdims=True) + RMS_EPS) * g).astype(BF16)

    scale = (MLA_NOPE + MLA_ROPE) ** -0.5 * math.log2(math.e)
    _, qs = _col_group_dots(rms(cq_ref[...], qn_ref[...]), wq_ref, MLA_QK_PAD)
    _, kv2 = _col_group_dots(rms(ckv_ref[...], kvn_ref[...]), wkv_ref, 2 * LANES)
    kvs = [g[:, half * LANES:(half + 1) * LANES] for g in kv2 for half in range(2)]
    cm, sm = cm_ref[...], sm_ref[...]
    kr = kr_ref[...]
    lane = lax.broadcasted_iota(jnp.int32, (kr.shape[0], LANES), 1)
    ones_col = jnp.where(lane == 0, 1.0, 0.0).astype(v_ref.dtype)
    for hd in range(MLA_HEADS):
        c0 = hd * MLA_QK_PAD
        q_ref[:, c0:c0 + LANES] = (qs[hd][:, 0:LANES] * scale).astype(q_ref.dtype)
        x = qs[hd][:, LANES:2 * LANES]
        y = (x * cm + _rope_partner(x, MLA_ROPE // 4) * sm) * scale
        q_ref[:, c0 + LANES:c0 + 2 * LANES] = y.astype(q_ref.dtype)
        k_ref[:, c0:c0 + LANES] = kvs[hd].astype(k_ref.dtype)
        k_ref[:, c0 + LANES:c0 + 2 * LANES] = kr
        v0 = hd * MLA_V_PAD
        v_ref[:, v0:v0 + MLA_V] = kvs[MLA_HEADS + hd].astype(v_ref.dtype)
        v_ref[:, v0 + MLA_V:v0 + MLA_V_PAD] = ones_col


def _mla_proj(p_att, q_norm, kv_norm, wq, wkv, cm, sm, rows):
    tm = rows.tm
    n = rows.n_rows
    hq = MLA_HEADS * MLA_QK_PAD

    def col(width, off):
        return pl.BlockSpec((tm, width), lambda i: (i, off // width))

    def whole(a):
        return pl.BlockSpec(a.shape, lambda i: (0, 0))

    return pl.pallas_call(
        _mla_proj_kernel, grid=(rows.n_tiles,),
        in_specs=[col(MLA_Q_LORA, ATT_CQ), col(MLA_KV_LORA, ATT_CKV), col(LANES, ATT_KR),
                  whole(q_norm), whole(kv_norm), whole(wq), whole(wkv),
                  pl.BlockSpec((tm, LANES), lambda i: (i, 0)), pl.BlockSpec((tm, LANES), lambda i: (i, 0))],
        out_specs=[pl.BlockSpec((tm, hq), lambda i: (i, 0)), pl.BlockSpec((tm, hq), lambda i: (i, 0)),
                   pl.BlockSpec((tm, MLA_HEADS * MLA_V_PAD), lambda i: (i, 0))],
        out_shape=[jax.ShapeDtypeStruct((n, hq), BF16), jax.ShapeDtypeStruct((n, hq), BF16),
                   jax.ShapeDtypeStruct((n, MLA_HEADS * MLA_V_PAD), BF16)],
        compiler_params=_cparams(("arbitrary",)), name="mla_proj",
    )(p_att, p_att, p_att, q_norm, kv_norm, wq, wkv, cm, sm)


def _lane_tile_fold(x, op):
    out = x[:, 0:LANES]
    for t in range(1, x.shape[1] // LANES):
        out = op(out, x[:, t * LANES:(t + 1) * LANES])
    return out


def _mla_attn_kernel(*refs, ck, with_lat):
    if with_lat:
        q_ref, kl_ref, vl_ref, kc_ref, vc_ref, o_ref = refs
    else:
        q_ref, kc_ref, vc_ref, o_ref = refs
    chunks = [(kc_ref, vc_ref, 0, kc_ref.shape[0])]
    if with_lat:
        chunks += [(kl_ref, vl_ref, c * ck, ck) for c in range(kl_ref.shape[0] // ck)]
    m = [None] * MLA_HEADS_PER_STEP
    acc = [None] * MLA_HEADS_PER_STEP
    for k_ref, v_ref, r0, n in chunks:
        for hd in range(MLA_HEADS_PER_STEP):
            qk = slice(hd * MLA_QK_PAD, (hd + 1) * MLA_QK_PAD)
            s = _dot_nt(q_ref[:, qk], k_ref[r0:r0 + n, qk])
            m_c = jnp.max(_lane_tile_fold(s, jnp.maximum), axis=-1, keepdims=True)
            m_new = m_c if m[hd] is None else jnp.maximum(m[hd], m_c)
            pv = _dot(jnp.exp2(s - m_new).astype(BF16), v_ref[r0:r0 + n, hd * MLA_V_PAD:(hd + 1) * MLA_V_PAD])
            acc[hd] = pv if m[hd] is None else acc[hd] * jnp.exp2(m[hd] - m_new) + pv
            m[hd] = m_new
    for hd in range(MLA_HEADS_PER_STEP):
        o = acc[hd][:, 0:MLA_V] / acc[hd][:, MLA_V:MLA_V + 1]
        o_ref[:, hd * MLA_V:(hd + 1) * MLA_V] = o.astype(o_ref.dtype)


def _mla_attn(q, k, v, batch, seq, ctx_len, latent):
    ctx0 = batch * seq // ctx_len
    ck = 1024 if seq % 1024 == 0 else ctx_len
    hps = MLA_HEADS_PER_STEP
    if latent:
        tq = MLA_TQ if seq % MLA_TQ == 0 else ctx_len
        nq = seq // tq
        q_row = lambda b, h, i: b * nq + i
        o_row, n_rows = q_row, batch * seq
    else:
        tq, nq = ctx_len, 1
        q_row = lambda b, h, i: ctx0 + b
        o_row, n_rows = (lambda b, h, i: b), batch * ctx_len
    in_specs = [pl.BlockSpec((tq, hps * MLA_QK_PAD), lambda b, h, i: (q_row(b, h, i), h))]
    inputs = [q]
    if latent:
        in_specs += [pl.BlockSpec((seq, hps * MLA_QK_PAD), lambda b, h, i: (b, h)),
                     pl.BlockSpec((seq, hps * MLA_V_PAD), lambda b, h, i: (b, h))]
        inputs += [k, v]
    in_specs += [pl.BlockSpec((ctx_len, hps * MLA_QK_PAD), lambda b, h, i: (ctx0 + b, h)),
                 pl.BlockSpec((ctx_len, hps * MLA_V_PAD), lambda b, h, i: (ctx0 + b, h))]
    inputs += [k, v]
    return pl.pallas_call(
        functools.partial(_mla_attn_kernel, ck=ck, with_lat=latent),
        grid=(batch, MLA_HEADS // hps, nq), in_specs=in_specs,
        out_specs=pl.BlockSpec((tq, hps * MLA_V), lambda b, h, i: (o_row(b, h, i), h)),
        out_shape=jax.ShapeDtypeStruct((n_rows, MLA_HEADS * MLA_V), BF16),
        compiler_params=_cparams(("arbitrary", "arbitrary", "arbitrary")),
        name="mla_attn_lat" if latent else "mla_attn_ctx",
    )(*inputs)


def _swa_kernel(sink_ref, q_ref, kp_ref, kn_ref, kx_ref, kc_ref, vp_ref, vn_ref, vx_ref, vc_ref, o_ref,
                *, n_lat_steps, seq, ctx_queries):
    n = pl.program_id(1)
    g = SWA_HEADS // SWA_KV_HEADS
    blk = SWA_BLOCK
    nblk = SWA_BLOCKS_PER_STEP

    def attend(band):
        items = [(j, kh) for j in range(nblk) for kh in range(SWA_KV_HEADS)]
        ks = [slice(kh * SWA_HD, (kh + 1) * SWA_HD) for kh in range(SWA_KV_HEADS)]
        row = lax.broadcasted_iota(jnp.int32, (g * blk, 1), 0)
        if band:
            a = lax.broadcasted_iota(jnp.int32, (g * blk, 3 * blk), 0) & (blk - 1)
            jj = lax.broadcasted_iota(jnp.int32, (g * blk, 3 * blk), 1)
            in_window = jnp.abs(jj - blk - a) <= SWA_WINDOW
            k_slab = [jnp.concatenate([kp_ref[:, s], kn_ref[:, s], kx_ref[:, s]], axis=0) for s in ks]
            v_slab = [jnp.concatenate([vp_ref[:, s], vn_ref[:, s], vx_ref[:, s]], axis=0) for s in ks]
        sink = []
        for kh in range(SWA_KV_HEADS):
            sk = jnp.zeros((g * blk, 1), F32)
            for gi in range(g):
                sk = jnp.where((row >= gi * blk) & (row < (gi + 1) * blk), sink_ref[kh * g + gi], sk)
            sink.append(sk)
        s_c, s_b = {}, {}
        for it in items:
            j, kh = it
            qs = jnp.concatenate([q_ref[j * blk:(j + 1) * blk, (kh * g + gi) * SWA_HD:(kh * g + gi + 1) * SWA_HD]
                                  for gi in range(g)], axis=0)
            s_c[it] = _dot_nt(qs, kc_ref[:, ks[kh]])
            if band:
                key_pos = (n * nblk + j - 1) * blk + jj
                keep = in_window & (key_pos >= 0) & (key_pos < seq)
                s_b[it] = jnp.where(keep, _dot_nt(qs, k_slab[kh][j * blk:(j + 3) * blk]), -jnp.inf)
        p_c, p_b, den = {}, {}, {}
        for it in items:
            m = jnp.maximum(jnp.max(s_c[it], axis=-1, keepdims=True), sink[it[1]])
            if band:
                m = jnp.maximum(m, jnp.max(s_b[it], axis=-1, keepdims=True))
            pc = jnp.exp(s_c[it] - m)
            dn = jnp.sum(pc, axis=-1, keepdims=True) + jnp.exp(sink[it[1]] - m)
            if band:
                pb = jnp.exp(s_b[it] - m)
                dn = dn + jnp.sum(pb, axis=-1, keepdims=True)
                p_b[it] = pb.astype(BF16)
            p_c[it] = pc.astype(BF16)
            den[it] = dn
        for it in items:
            j, kh = it
            acc = _dot(p_c[it], vc_ref[:, ks[kh]])
            if band:
                acc = acc + _dot(p_b[it], v_slab[kh][j * blk:(j + 3) * blk])
            o = acc / den[it]
            for gi in range(g):
                c0 = (kh * g + gi) * SWA_HD
                o_ref[j * blk:(j + 1) * blk, c0:c0 + SWA_HD] = o[gi * blk:(gi + 1) * blk].astype(o_ref.dtype)

    if ctx_queries:
        pl.when(n < n_lat_steps)(lambda: attend(True))
        pl.when(n >= n_lat_steps)(lambda: attend(False))
    else:
        attend(True)


def _swa(p_att, sink, batch, seq, ctx_len, ctx_queries):
    blk = SWA_BLOCK
    nblk = SWA_BLOCKS_PER_STEP
    tq = nblk * blk
    assert seq % tq == 0 and ctx_len % tq == 0
    nb = seq // blk
    ns, ncs = seq // tq, ctx_len // tq
    ctx0 = batch * seq // ctx_len
    kcol, vcol = ATT_SK // SWA_KV_W, ATT_SV // SWA_KV_W
    q_row = lambda b, n: jnp.where(n < ns, b * ns + n, batch * ns + b * ncs + (n - ns))
    n_rows = batch * seq + (batch * ctx_len if ctx_queries else 0)

    def edge(col, first):
        def index(b, n):
            k = jnp.clip(n, 0, ns - 1) * nblk + (-1 if first else nblk)
            return (b * nb + jnp.clip(k, 0, nb - 1), col)
        return pl.BlockSpec((blk, SWA_KV_W), index)

    def own(col):
        return pl.BlockSpec((tq, SWA_KV_W), lambda b, n: (b * ns + jnp.clip(n, 0, ns - 1), col))

    ctx_k = pl.BlockSpec((ctx_len, SWA_KV_W), lambda b, n: (ctx0 + b, kcol))
    ctx_v = pl.BlockSpec((ctx_len, SWA_KV_W), lambda b, n: (ctx0 + b, vcol))
    return pl.pallas_call(
        functools.partial(_swa_kernel, n_lat_steps=ns, seq=seq, ctx_queries=ctx_queries),
        grid=(batch, ns + (ncs if ctx_queries else 0)),
        in_specs=[pl.BlockSpec(memory_space=pltpu.SMEM),
                  pl.BlockSpec((tq, SWA_W), lambda b, n: (q_row(b, n), 0)),
                  edge(kcol, True), own(kcol), edge(kcol, False), ctx_k,
                  edge(vcol, True), own(vcol), edge(vcol, False), ctx_v],
        out_specs=pl.BlockSpec((tq, SWA_W), lambda b, n: (q_row(b, n), 0)),
        out_shape=jax.ShapeDtypeStruct((n_rows, SWA_W), BF16),
        compiler_params=_cparams(("arbitrary", "arbitrary")),
        name="swa_attn",
    )(sink, *([p_att] * 9))


def _route(logits, rb):
    aff = _sigmoid(logits)
    sel = aff + rb
    tm = logits.shape[1]
    scores = []
    for g in range(N_GROUPS):
        r = [sel[g * E_PER_GROUP + j:g * E_PER_GROUP + j + 1] for j in range(E_PER_GROUP)]
        best = None
        for i in range(E_PER_GROUP):
            for j in range(i + 1, E_PER_GROUP):
                pair = r[i] + r[j]
                best = pair if best is None else jnp.maximum(best, pair)
        scores.append(best)
    gbest, gi = scores[0], jnp.zeros((1, tm), jnp.int32)
    for g in range(1, N_GROUPS):
        upd = scores[g] > gbest
        gbest = jnp.where(upd, scores[g], gbest)
        gi = jnp.where(upd, g, gi)
    s_in, a_in = [], []
    for j in range(E_PER_GROUP):
        sj, aj = sel[j:j + 1], aff[j:j + 1]
        for g in range(1, N_GROUPS):
            e = g * E_PER_GROUP + j
            sj = jnp.where(gi == g, sel[e:e + 1], sj)
            aj = jnp.where(gi == g, aff[e:e + 1], aj)
        s_in.append(sj)
        a_in.append(aj)
    chosen = []
    for j in range(E_PER_GROUP):
        rank = jnp.zeros((1, tm), jnp.int32)
        for k in range(E_PER_GROUP):
            if k == j:
                continue
            ahead = (s_in[k] >= s_in[j]) if k < j else (s_in[k] > s_in[j])
            rank = rank + ahead.astype(jnp.int32)
        chosen.append(rank < 2)
    w = [jnp.where(chosen[j], a_in[j], 0.0) for j in range(E_PER_GROUP)]
    wsum = w[0] + w[1] + w[2] + w[3]
    gate_in = [wj / wsum * ROUTE_SCALE for wj in w]
    code = sum(jnp.where(chosen[j], 1 << j, 0) for j in range(E_PER_GROUP))
    pair = jnp.zeros((1, tm), jnp.int32)
    for idx, cval in enumerate((3, 5, 9, 6, 10, 12)):
        pair = jnp.where(code == cval, idx, pair)
    bucket = gi * N_PAIRS + pair
    g_lo = jnp.zeros((1, tm), F32)
    g_hi = jnp.zeros((1, tm), F32)
    seen = jnp.zeros((1, tm), jnp.bool_)
    for j in range(E_PER_GROUP):
        g_lo = jnp.where(chosen[j] & ~seen, gate_in[j], g_lo)
        g_hi = jnp.where(chosen[j] & seen, gate_in[j], g_hi)
        seen = seen | chosen[j]
    return g_lo, g_hi, bucket


def _lat_or_ctx(lat_ref, ctx_ref, rs, lat_tiles):
    if ctx_ref is None:
        return lat_ref[rs, :]
    return jnp.where(pl.program_id(0) < lat_tiles, lat_ref[rs, :], ctx_ref[rs, :])


def _dual_specs(rows, width):
    return [pl.BlockSpec((rows.tm, width), lambda i, *_: (jnp.minimum(i, rows.lat_tiles - 1), 0)),
            pl.BlockSpec((rows.tm, width), lambda i, *_: (jnp.maximum(i - rows.lat_tiles, 0), 0),
                         pipeline_mode=pl.Buffered(1))]


def _out_kernel(*refs, alpha, dual, lat_tiles):
    if dual:
        (ohg_ref, omla_ref, omlac_ref, oswa_ref, x_ref, xc_ref, g1_ref, sh2_ref, sc2_ref, lng_ref, lnb_ref,
         w_ref, rwt_ref, rb_ref, x1_ref, h2_ref, bucket_ref) = refs
    else:
        (ohg_ref, omla_ref, oswa_ref, x_ref, g1_ref, sh2_ref, sc2_ref, lng_ref, lnb_ref,
         w_ref, rwt_ref, rb_ref, x1_ref, h2_ref, bucket_ref) = refs
        omlac_ref = xc_ref = None
    o1, o2 = HG_W, HG_W + MLA_HEADS * MLA_V
    d = x_ref.shape[1]
    tm = x_ref.shape[0]
    halves = [slice(0, tm // 2), slice(tm // 2, tm)]
    mixes = [_dot(ohg_ref[rs, :], w_ref[0:o1, :])
             + _dot(_lat_or_ctx(omla_ref, omlac_ref, rs, lat_tiles), w_ref[o1:o2, :])
             + _dot(oswa_ref[rs, :], w_ref[o2:, :]) for rs in halves]
    for rs, mix in zip(halves, mixes):
        x_in = _lat_or_ctx(x_ref, xc_ref, rs, lat_tiles)
        x1 = _ln_rows(alpha * x_in + g1_ref[...] * mix) * lng_ref[...] + lnb_ref[...]
        x1_ref[rs, :] = x1
        h2 = (_ln_rows(x1) * (1.0 + sc2_ref[...]) + sh2_ref[...]).astype(BF16)
        h2_ref[rs, 0:d] = h2.astype(h2_ref.dtype)
        g_lo, g_hi, bucket = _route(_dot_nt(rwt_ref[...], h2), rb_ref[...])
        bucket_ref[:, rs] = bucket
        gate_rows = jnp.concatenate([g_lo, g_hi, jnp.zeros((LANES - 2, tm // 2), F32)], axis=0)
        h2_ref[rs, d:d + LANES] = gate_rows.T


def _out_proj(o_hg, o_mla, o_swa, x, mod, ln_g, ln_b, w_out, rwt, rb, rows, alpha):
    tm = rows.tm
    dual = isinstance(x, tuple)
    d = (x[0] if dual else x).shape[1]
    n = rows.n_rows
    row = lambda w: pl.BlockSpec((tm, w), lambda i: (i, 0))
    whole = lambda a: pl.BlockSpec(a.shape, lambda i: (0, 0), pipeline_mode=pl.Buffered(1))
    mla_w = MLA_HEADS * MLA_V
    if dual:
        in_specs = [row(HG_W)] + _dual_specs(rows, mla_w) + [row(SWA_W)] + _dual_specs(rows, d)
        inputs = [o_hg, *o_mla, o_swa, *x]
    else:
        in_specs = [row(HG_W), row(mla_w), row(SWA_W), row(d)]
        inputs = [o_hg, o_mla, o_swa, x]
    in_specs += [_mod_spec(rows, 2, d), _mod_spec(rows, 3, d), _mod_spec(rows, 4, d),
                 whole(ln_g), whole(ln_b), whole(w_out), whole(rwt), whole(rb)]
    inputs += [mod, mod, mod, ln_g, ln_b, w_out, rwt, rb]
    return pl.pallas_call(
        functools.partial(_out_kernel, alpha=alpha, dual=dual, lat_tiles=rows.lat_tiles), grid=(rows.n_tiles,),
        in_specs=in_specs,
        out_specs=[row(d), row(d + LANES), pl.BlockSpec((1, tm), lambda i: (0, i))],
        out_shape=[jax.ShapeDtypeStruct((n, d), F32), jax.ShapeDtypeStruct((n, d + LANES), F32),
                   jax.ShapeDtypeStruct((1, n), jnp.int32)],
        compiler_params=_cparams(("arbitrary",)), name="out_proj_ln_router",
    )(*inputs)


def _row_gather_start(idx_ref, base, src_hbm, dst, sem, n, static_rows=False):
    def issue(r, carry):
        pltpu.make_async_copy(src_hbm.at[pl.ds(idx_ref[base + r], 1)], dst.at[pl.ds(r, 1)], sem).start()
        return carry
    if static_rows:
        for r in range(n):
            issue(r, 0)
    else:
        lax.fori_loop(0, n, issue, 0, unroll=8)


def _row_gather_start_next(t, n_valid, idx_ref, base, src_hbm, buf, sem, n):
    for half in range(2):
        @pl.when((t < n_valid) & (t % 2 == half))
        def _():
            _row_gather_start(idx_ref, base, src_hbm, buf.at[half], sem.at[half], n, static_rows=True)


def _row_gather_wait(src_hbm, dst, sem, n):
    pltpu.make_async_copy(src_hbm.at[pl.ds(0, n)], dst, sem).wait()


def _moe_kernel(stok_ref, off_ref, e1_ref, e2_ref, nused_ref, h_hbm, wg1_ref, wu1_ref, wd1_ref,
                wg2_ref, wu2_ref, wd2_ref, y_ref, hbuf, sem, *, tm):
    i = pl.program_id(0)
    n_used = nused_ref[0]
    d = y_ref.shape[1]

    @pl.when(i == 0)
    def _():
        _row_gather_start(stok_ref, off_ref[0], h_hbm, hbuf.at[0], sem.at[0], tm)

    @pl.when((i == 0) & (n_used > 1))
    def _():
        _row_gather_start(stok_ref, off_ref[1], h_hbm, hbuf.at[1], sem.at[1], tm)

    @pl.when(i < n_used)
    def _():
        slot = i % 3
        _row_gather_wait(h_hbm, hbuf.at[slot], sem.at[slot], tm)
        h = hbuf[slot, :, 0:d].astype(BF16)
        g_lo = hbuf[slot, :, d:d + 1]
        g_hi = hbuf[slot, :, d + 1:d + 2]

        act1 = (_silu(_dot(h, wg1_ref[...])) * _dot(h, wu1_ref[...])).astype(BF16)
        act2 = (_silu(_dot(h, wg2_ref[...])) * _dot(h, wu2_ref[...])).astype(BF16)
        ahead = (i + 2) % 3
        nxt = jnp.minimum(i + 2, n_used - 1)
        _row_gather_start(stok_ref, off_ref[nxt], h_hbm, hbuf.at[ahead], sem.at[ahead], tm, static_rows=True)
        y_ref[...] = g_lo * _dot(act1, wd1_ref[...]) + g_hi * _dot(act2, wd2_ref[...])

    @pl.when(i == n_used - 1)
    def _():
        ahead = (i + 2) % 3
        _row_gather_wait(h_hbm, hbuf.at[ahead], sem.at[ahead], tm)

    @pl.when((i == n_used - 1) & (i >= 1))
    def _():
        prev_ahead = (i + 1) % 3
        _row_gather_wait(h_hbm, hbuf.at[prev_ahead], sem.at[prev_ahead], tm)

    @pl.when(i >= n_used)
    def _():
        y_ref[...] = jnp.zeros_like(y_ref)


def _moe(h2, stok, off, e1, e2, n_used, wg, wu, wd, layer, tm):
    n_tiles = off.shape[0]
    dx = h2.shape[1]
    d, f = wg.shape[2], wg.shape[3]
    last = lambda i, nu: jnp.minimum(i, nu[0] - 1)
    wspec_in = lambda sel: pl.BlockSpec(
        (None, None, d, f), lambda i, s, o, a, b, nu: (layer, (a, b)[sel][last(i, nu)], 0, 0))
    wspec_dn = lambda sel: pl.BlockSpec(
        (None, None, f, d), lambda i, s, o, a, b, nu: (layer, (a, b)[sel][last(i, nu)], 0, 0))
    grid_spec = pltpu.PrefetchScalarGridSpec(
        num_scalar_prefetch=5, grid=(n_tiles,),
        in_specs=[pl.BlockSpec(memory_space=pl.ANY),
                  wspec_in(0), wspec_in(0), wspec_dn(0), wspec_in(1), wspec_in(1), wspec_dn(1)],
        out_specs=pl.BlockSpec((tm, d), lambda i, s, o, a, b, nu: (i, 0)),
        scratch_shapes=[pltpu.VMEM((3, tm, dx), F32), pltpu.SemaphoreType.DMA((3,))])
    return pl.pallas_call(
        functools.partial(_moe_kernel, tm=tm), grid_spec=grid_spec,
        out_shape=jax.ShapeDtypeStruct((n_tiles * tm, d), F32),
        compiler_params=_cparams(("arbitrary",)), name="moe_grouped",
    )(stok, off, e1, e2, n_used, h2, wg, wu, wd, wg, wu, wd)


def _moe_plan(bucket, n_tokens, tm):
    tok = jnp.arange(n_tokens, dtype=jnp.int32)
    buckets = jnp.arange(N_BUCKETS, dtype=jnp.int32)
    skey, stok = lax.sort((bucket * n_tokens + tok, tok), num_keys=1)
    counts = jnp.sum((bucket[:, None] == buckets[None, :]).astype(jnp.int32), axis=0)
    padded = (counts + tm - 1) // tm * tm
    ends = jnp.cumsum(padded)
    shift = (ends - padded) - (jnp.cumsum(counts) - counts)
    sbucket = skey // n_tokens
    slot = tok + jnp.sum(jnp.where(sbucket[:, None] == buckets[None, :], shift[None, :], 0), axis=1)
    _, pos = lax.sort((stok, slot), num_keys=1)
    n_tiles = n_tokens // tm + N_BUCKETS
    tile_start = jnp.arange(n_tiles, dtype=jnp.int32) * tm
    tile_bucket = jnp.sum((ends[None, :] <= tile_start[:, None]).astype(jnp.int32), axis=1)
    tile_bucket = jnp.minimum(tile_bucket, N_BUCKETS - 1)
    tshift = jnp.sum(jnp.where(tile_bucket[:, None] == buckets[None, :], shift[None, :], 0), axis=1)
    off = jnp.clip(tile_start - tshift, 0, n_tokens)
    grp, pair = tile_bucket // N_PAIRS, tile_bucket % N_PAIRS
    lo = jnp.where(pair < 3, 0, jnp.where(pair < 5, 1, 2))
    hi = jnp.where(pair == 0, 1, jnp.where((pair == 1) | (pair == 3), 2, 3))
    e1, e2 = grp * E_PER_GROUP + lo, grp * E_PER_GROUP + hi
    n_used = (ends[-1] // tm).astype(jnp.int32).reshape(1)
    stok = jnp.concatenate([stok, jnp.zeros((tm,), jnp.int32)])
    return stok, off.astype(jnp.int32), pos.astype(jnp.int32), e1.astype(jnp.int32), e2.astype(jnp.int32), n_used


def _ln2_kernel(pos_ref, y_hbm, x1_ref, g2_ref, lng_ref, lnb_ref, *rest, alpha, tm, emit_h):
    if emit_h:
        sh_ref, sc_ref, x2_ref, h_ref, ybuf, sem = rest
    else:
        x2_ref, ybuf, sem = rest
    i = pl.program_id(0)
    n_tiles = pl.num_programs(0)

    @pl.when(i == 0)
    def _():
        _row_gather_start(pos_ref, 0, y_hbm, ybuf.at[0], sem.at[0], tm)

    _row_gather_start_next(i + 1, n_tiles, pos_ref, (i + 1) * tm, y_hbm, ybuf, sem, tm)

    slot = i % 2
    _row_gather_wait(y_hbm, ybuf.at[slot], sem.at[slot], tm)
    x2 = _ln_rows(alpha * x1_ref[...] + g2_ref[...] * ybuf[slot]) * lng_ref[...] + lnb_ref[...]
    x2_ref[...] = x2
    if emit_h:
        h_ref[...] = (_ln_rows(x2) * (1.0 + sc_ref[...]) + sh_ref[...]).astype(h_ref.dtype)


def _ln2(pos, y_sorted, x1, mod, mod_next, ln_g, ln_b, rows, alpha):
    tm = rows.tm
    d = x1.shape[1]
    emit_h = mod_next is not None
    row = pl.BlockSpec((tm, d), lambda i, p: (i, 0))
    whole = lambda a: pl.BlockSpec(a.shape, lambda i, p: (0, 0))
    in_specs = [pl.BlockSpec(memory_space=pl.ANY), row, _mod_spec(rows, 5, d), whole(ln_g), whole(ln_b)]
    inputs = [y_sorted, x1, mod, ln_g, ln_b]
    out_specs = [row]
    out_shape = [jax.ShapeDtypeStruct((rows.n_rows, d), F32)]
    if emit_h:
        in_specs += [_mod_spec(rows, 0, d), _mod_spec(rows, 1, d)]
        inputs += [mod_next, mod_next]
        out_specs.append(row)
        out_shape.append(jax.ShapeDtypeStruct((rows.n_rows, d), BF16))
    grid_spec = pltpu.PrefetchScalarGridSpec(
        num_scalar_prefetch=1, grid=(rows.n_tiles,), in_specs=in_specs, out_specs=out_specs,
        scratch_shapes=[pltpu.VMEM((2, tm, d), F32), pltpu.SemaphoreType.DMA((2,))])
    out = pl.pallas_call(
        functools.partial(_ln2_kernel, alpha=alpha, tm=tm, emit_h=emit_h), grid_spec=grid_spec,
        out_shape=out_shape, compiler_params=_cparams(("arbitrary",)), name="unpermute_ln2",
    )(pos, *inputs)
    return out if emit_h else (out[0], None)


def _rope_tables(seq, batch, ctx_rows, dim, pad_to):
    rows = seq // GRID_W
    row = jnp.repeat(jnp.arange(rows, dtype=jnp.int32), GRID_W)
    col = jnp.tile(jnp.arange(GRID_W, dtype=jnp.int32), rows)
    nf = dim // 4
    inv_freq = ROPE_BASE ** (-jnp.arange(nf, dtype=F32) / nf)
    ang = jnp.stack([row, col], -1).astype(F32)[:, :, None] * inv_freq
    cos, sin = jnp.cos(ang), jnp.sin(ang)
    c = jnp.stack([cos, cos], axis=2).reshape(seq, dim)
    s = jnp.stack([-sin, sin], axis=2).reshape(seq, dim)
    if pad_to > dim:
        c = jnp.pad(c, ((0, 0), (0, pad_to - dim)))
        s = jnp.pad(s, ((0, 0), (0, pad_to - dim)))
    ctx_c = jnp.zeros((ctx_rows, pad_to), F32).at[:, :dim].set(1.0)
    c = jnp.concatenate([jnp.tile(c, (batch, 1)), ctx_c], axis=0)
    s = jnp.concatenate([jnp.tile(s, (batch, 1)), jnp.zeros((ctx_rows, pad_to), F32)], axis=0)
    return c, s


def _w_in_prep_kernel(w_ref, raw_ref, gate_ref, att_ref):
    o1 = 5 * HG_W
    o2 = o1 + MLA_Q_LORA + MLA_KV_LORA
    o3 = o2 + MLA_ROPE
    cast = lambda a, b: w_ref[:, a:b].astype(BF16)
    raw_ref[:, 0:HG_W] = cast(0, HG_W)
    raw_ref[:, HG_W:3 * HG_W] = cast(3 * HG_W, o1)
    gate_ref[...] = cast(HG_W, 3 * HG_W)
    att_ref[:, ATT_SQ:ATT_CQ] = cast(o3, o3 + SWA_W + 2 * SWA_KV_W)
    att_ref[:, ATT_CQ:ATT_KR] = cast(o1, o2)
    att_ref[:, ATT_KR:ATT_KR + MLA_ROPE] = cast(o2, o3)
    att_ref[:, ATT_KR + MLA_ROPE:ATT_COLS] = jnp.zeros((w_ref.shape[0], LANES - MLA_ROPE), BF16)


def _w_in_prep(w_in):
    depth, d, n = w_in.shape
    tr = 256 if d % 256 == 0 else d
    widths = (3 * HG_W, 2 * HG_W, ATT_COLS)
    return pl.pallas_call(
        _w_in_prep_kernel, grid=(depth, d // tr),
        in_specs=[pl.BlockSpec((None, tr, n), lambda l, i: (l, i, 0))],
        out_specs=[pl.BlockSpec((None, tr, w), lambda l, i: (l, i, 0)) for w in widths],
        out_shape=[jax.ShapeDtypeStruct((depth, d, w), BF16) for w in widths],
        compiler_params=_cparams(("arbitrary", "arbitrary")), name="w_in_prep",
    )(w_in)


def _mla_weights(w_uq, w_ukv):
    qk = MLA_NOPE + MLA_ROPE
    wq = w_uq.reshape(MLA_Q_LORA, MLA_HEADS, qk)
    wq = jnp.pad(wq, ((0, 0), (0, 0), (0, MLA_QK_PAD - qk))).reshape(MLA_Q_LORA, MLA_HEADS * MLA_QK_PAD)
    wkv = w_ukv.reshape(MLA_KV_LORA, MLA_HEADS, MLA_NOPE + MLA_V)
    wkv = jnp.concatenate([wkv[:, :, :MLA_NOPE].reshape(MLA_KV_LORA, -1), wkv[:, :, MLA_NOPE:].reshape(MLA_KV_LORA, -1)], axis=1)
    return wq.astype(BF16), wkv.astype(BF16)


def kernel(x, c, ctx, c_ctx, w_ada, b_ada, w_in, w_out, hg_lb_logits, hg_norm_g, mla_q_norm, mla_kv_norm,
           mla_w_uq, mla_w_ukv, swa_sink, ln1_g, ln1_b, ln2_g, ln2_b, router_w, router_b,
           moe_w_gate, moe_w_up, moe_w_down):
    batch, seq, d = x.shape
    ctx_len = ctx.shape[1]
    depth = w_ada.shape[0]
    alpha = (2.0 * depth) ** 0.25
    n_lat, n_ctx = batch * seq, batch * ctx_len
    n_all = n_lat + n_ctx
    assert batch + 1 <= 8 and ctx_len % (4 * HG_CHUNK) == 0 and seq % ctx_len == 0 and seq % GRID_W == 0

    c8 = jnp.zeros((8, d), F32).at[:batch].set(c).at[batch].set(c_ctx)
    mod_all = _ada(c8, w_ada, b_ada).reshape(depth, 8 * 6, 1, d)

    lb = jnp.cumsum(jax.nn.softmax(hg_lb_logits.astype(F32), axis=0), axis=0)
    lb = (lb - lb[0:1]).reshape(depth, 1, 2 * HG_W)
    log_lb, log_1m, one_m = jnp.log(lb), jnp.log1p(-lb), 1.0 - lb

    cs, ss = _rope_tables(seq, batch, n_ctx, SWA_HD, SWA_HD)
    cm, sm = _rope_tables(seq, batch, n_ctx, MLA_ROPE, LANES)
    rwt = router_w.T.astype(BF16)
    rb = router_b.astype(F32).reshape(N_EXPERTS, 1)

    tm_all = _pick_tm(seq, n_ctx, 512)
    rows_all = _Rows(batch, seq, ctx_len, n_all, tm_all)
    rows_lat = _Rows(batch, seq, ctx_len, n_lat, tm_all)
    moe_tm = 256

    xa = (x.reshape(n_lat, d), ctx.reshape(n_ctx, d))
    h = _lnmod(xa[0], xa[1], mod_all[0], rows_all, 0)
    w_raw_all, w_gate_all, w_att_all = _w_in_prep(w_in)
    wg_all, wu_all, wd_all = moe_w_gate.astype(BF16), moe_w_up.astype(BF16), moe_w_down.astype(BF16)

    for layer in range(depth):
        need_ctx = layer < depth - 1
        mod = mod_all[layer]
        rows = rows_all if need_ctx else rows_lat
        wq, wkv = _mla_weights(mla_w_uq[layer], mla_w_ukv[layer])

        lf, kk, p_att, p_raw = _in_proj(h, w_gate_all, w_att_all, w_raw_all, layer, log_lb[layer], log_1m[layer],
                                        one_m[layer], [cs, ss, cm, sm], rows_all)

        o_f = _hg_scan(p_raw, kk, lf, batch, seq, ctx_len, False)
        o_hg = _hg_scan(p_raw, kk, lf, batch, seq, ctx_len, True,
                        (o_f, hg_norm_g[layer].astype(F32).reshape(1, HG_DK)))

        q_mla, k_mla, v_mla = _mla_proj(p_att, mla_q_norm[layer].astype(F32).reshape(1, -1),
                                        mla_kv_norm[layer].astype(F32).reshape(1, -1), wq, wkv, cm, sm, rows_all)
        n_out = rows.n_rows
        o_mla = _mla_attn(q_mla, k_mla, v_mla, batch, seq, ctx_len, True)
        o_swa = _swa(p_att, swa_sink[layer].astype(F32), batch, seq, ctx_len, need_ctx)
        dual = isinstance(xa, tuple)
        if need_ctx:
            o_mla = (o_mla, _mla_attn(q_mla, k_mla, v_mla, batch, seq, ctx_len, False))
            if not dual:
                xa = (xa[:n_lat], xa[n_lat:])
        elif dual:
            xa = xa[0]

        x1, h2, bucket = _out_proj(
            o_hg, o_mla, o_swa, xa, mod, ln1_g[layer].astype(F32).reshape(1, d), ln1_b[layer].astype(F32).reshape(1, d),
            w_out[layer].astype(BF16), rwt, rb, rows, alpha)

        stok, off, pos, e1, e2, n_used = _moe_plan(bucket[0], n_out, moe_tm)
        y_sorted = _moe(h2, stok, off, e1, e2, n_used, wg_all, wu_all, wd_all, layer, moe_tm)
        xa, h = _ln2(pos, y_sorted, x1, mod, mod_all[layer + 1] if need_ctx else None,
                     ln2_g[layer].astype(F32).reshape(1, d), ln2_b[layer].astype(F32).reshape(1, d), rows, alpha)

    return xa[:n_lat].reshape(batch, seq, d)
```

```python
import functools
import math

import jax
import jax.numpy as jnp
from jax import lax
from jax.experimental import pallas as pl
from jax.experimental.pallas import tpu as pltpu

F32 = jnp.float32
BF16 = jnp.bfloat16

GRID_W = 64
HG_HEADS = 4
HG_DK = 128
HG_W = HG_HEADS * HG_DK
HG_CHUNK = 64
HG_SUB = 16
MLA_HEADS = 8
MLA_Q_LORA = 512
MLA_KV_LORA = 256
MLA_NOPE = 128
MLA_ROPE = 64
MLA_V = 128
MLA_QK_PAD = 256
MLA_V_PAD = 256
MLA_HEADS_PER_STEP = 4
MLA_TQ = 1024
SWA_HEADS = 4
SWA_KV_HEADS = 2
SWA_HD = 128
SWA_W = SWA_HEADS * SWA_HD
SWA_KV_W = SWA_KV_HEADS * SWA_HD
SWA_WINDOW = 128
SWA_BLOCK = 128
SWA_BLOCKS_PER_STEP = 2
N_EXPERTS = 16
N_GROUPS = 4
E_PER_GROUP = 4
N_PAIRS = 6
N_BUCKETS = N_GROUPS * N_PAIRS
ROUTE_SCALE = 2.5
ROPE_BASE = 10000.0
LN_EPS = 1e-5
RMS_EPS = 1e-6
LANES = 128
VMEM_LIMIT = 56 * 1024 * 1024

ATT_SQ, ATT_SK, ATT_SV = 0, SWA_W, SWA_W + SWA_KV_W
ATT_CQ = SWA_W + 2 * SWA_KV_W
ATT_CKV = ATT_CQ + MLA_Q_LORA
ATT_KR = ATT_CKV + MLA_KV_LORA
ATT_COLS = ATT_KR + LANES


def _cparams(sem):
    return pltpu.CompilerParams(dimension_semantics=sem, vmem_limit_bytes=VMEM_LIMIT)


def _dot(a, b):
    return jnp.dot(a, b, preferred_element_type=F32)


def _dot_nt(a, b):
    return lax.dot_general(a, b, (((1,), (1,)), ((), ())), preferred_element_type=F32)


def _dot_tn(a, b):
    return lax.dot_general(a, b, (((0,), (0,)), ((), ())), preferred_element_type=F32)


def _sigmoid(x):
    return 1.0 / (1.0 + jnp.exp(-x))


def _silu(x):
    return x * _sigmoid(x)


def _ln_rows(x):
    mu = jnp.mean(x, axis=-1, keepdims=True)
    xc = x - mu
    var = jnp.mean(xc * xc, axis=-1, keepdims=True)
    return xc * lax.rsqrt(var + LN_EPS)


def _rope_partner(x, half):
    lane = lax.broadcasted_iota(jnp.int32, x.shape, x.ndim - 1)
    first = (lane & half) == 0
    n = x.shape[-1]
    return jnp.where(first, pltpu.roll(x, n - half, x.ndim - 1), pltpu.roll(x, half, x.ndim - 1))


def _ada_kernel(c_ref, w_ref, b_ref, o_ref):
    s = _silu(c_ref[...])
    o_ref[...] = _dot(s.astype(BF16), w_ref[...].astype(BF16)) + b_ref[...]


def _ada(c8, w_ada, b_ada):
    depth, d, n = w_ada.shape
    tn = 1024 if n % 1024 == 0 else n
    return pl.pallas_call(
        _ada_kernel,
        grid=(depth, n // tn),
        in_specs=[pl.BlockSpec((8, d), lambda l, j: (0, 0)),
                  pl.BlockSpec((None, d, tn), lambda l, j: (l, 0, j)),
                  pl.BlockSpec((None, 1, tn), lambda l, j: (l, 0, j))],
        out_specs=pl.BlockSpec((None, 8, tn), lambda l, j: (l, 0, j)),
        out_shape=jax.ShapeDtypeStruct((depth, 8, n), F32),
        compiler_params=_cparams(("arbitrary", "arbitrary")),
        name="ada_mod",
    )(c8, w_ada, b_ada.reshape(depth, 1, n))


class _Rows:
    def __init__(self, batch, seq, ctx_len, n_rows, tm):
        self.batch, self.seq, self.ctx_len, self.n_rows, self.tm = batch, seq, ctx_len, n_rows, tm
        self.n_tiles = n_rows // tm
        self.lat_tiles = batch * seq // tm
        self.tiles_per_batch = seq // tm

    def mod_row(self, i):
        return jnp.where(i < self.lat_tiles, i // self.tiles_per_batch, self.batch)


def _pick_tm(seq, ctx_rows, cap):
    for tm in (1024, 512, 256, 128):
        if tm <= cap and seq % tm == 0 and ctx_rows % tm == 0:
            return tm
    raise ValueError("unsupported sequence / context lengths")


def _mod_spec(rows, chunk, d):
    return pl.BlockSpec((None, 1, d), lambda i, *_: (rows.mod_row(i) * 6 + chunk, 0, 0))


def _lnmod_kernel(x_ref, xc_ref, sh_ref, sc_ref, h_ref, *, lat_tiles):
    y = _ln_rows(_lat_or_ctx(x_ref, xc_ref, slice(None), lat_tiles))
    h_ref[...] = (y * (1.0 + sc_ref[...]) + sh_ref[...]).astype(h_ref.dtype)


def _lnmod(x_lat, x_ctx, mod, rows, chunk0):
    d = x_lat.shape[1]
    tm = rows.tm
    return pl.pallas_call(
        functools.partial(_lnmod_kernel, lat_tiles=rows.lat_tiles),
        grid=(rows.n_tiles,),
        in_specs=_dual_specs(rows, d) + [_mod_spec(rows, chunk0, d), _mod_spec(rows, chunk0 + 1, d)],
        out_specs=pl.BlockSpec((tm, d), lambda i: (i, 0)),
        out_shape=jax.ShapeDtypeStruct((rows.n_rows, d), BF16),
        compiler_params=_cparams(("arbitrary",)),
        name="ln_modulate",
    )(x_lat, x_ctx, mod, mod)


def _col_group_dots(h, w_ref, width):
    n = w_ref.shape[1]
    groups = [slice(c0, min(c0 + width, n)) for c0 in range(0, n, width)]
    return groups, [_dot(h, w_ref[:, g]) for g in groups]


def _in_proj_kernel(h_ref, wgate_ref, watt_ref, wraw_ref, loglb_ref, log1m_ref, onem_ref,
                    cs_ref, ss_ref, cm_ref, sm_ref, lf_ref, k_ref, att_ref, raw_ref):
    h = h_ref[...]
    g_groups, zs = _col_group_dots(h, wgate_ref, 2 * LANES)
    a_groups, ps = _col_group_dots(h, watt_ref, 2 * LANES)
    r_groups, rs = _col_group_dots(h, wraw_ref, 2 * LANES)
    for g, z in zip(g_groups, zs):
        e = jnp.exp(-jnp.abs(z))
        r = 1.0 / (1.0 + e)
        log_sig = jnp.minimum(z, 0.0) + jnp.log(r)
        sig_neg = jnp.where(z >= 0.0, e * r, r)
        a = loglb_ref[:, g]
        b = log1m_ref[:, g] + log_sig
        lf_ref[:, g] = jnp.maximum(a, b) + jnp.log(1.0 + jnp.exp(-jnp.abs(a - b)))
        k_ref[:, g] = onem_ref[:, g] * sig_neg
    cs, ss = cs_ref[...], ss_ref[...]
    swa_scale = SWA_HD ** -0.5
    for g, p in zip(a_groups, ps):
        for c0 in range(g.start, g.stop, LANES):
            x = p[:, c0 - g.start:c0 - g.start + LANES]
            if c0 < ATT_SV:
                x = x * cs + _rope_partner(x, SWA_HD // 4) * ss
                if c0 < ATT_SK:
                    x = x * swa_scale
            elif c0 == ATT_KR:
                x = x * cm_ref[...] + _rope_partner(x, MLA_ROPE // 4) * sm_ref[...]
            att_ref[:, c0:c0 + LANES] = x.astype(att_ref.dtype)
    for g, r in zip(r_groups, rs):
        raw_ref[:, g] = r


def _in_proj(h, w_gate_all, w_att_all, w_raw_all, layer, log_lb, log_1m, one_m, tables, rows):
    tm = rows.tm
    d = h.shape[1]
    n = rows.n_rows
    single = lambda spec: pl.BlockSpec(spec.block_shape, spec.index_map, pipeline_mode=pl.Buffered(1))
    consts = [(w_gate_all, layer), (w_att_all, layer), (w_raw_all, layer), log_lb, log_1m, one_m]
    row = lambda w: pl.BlockSpec((tm, w), lambda i: (i, 0))
    widths = (2 * HG_W, 2 * HG_W, ATT_COLS, 3 * HG_W)
    dtypes = (F32, F32, BF16, F32)
    return pl.pallas_call(
        _in_proj_kernel, grid=(rows.n_tiles,),
        in_specs=[row(d)] + [single(_const_spec(a)) for a in consts] + [row(LANES)] * 4,
        out_specs=[row(w) for w in widths],
        out_shape=[jax.ShapeDtypeStruct((n, w), dt) for w, dt in zip(widths, dtypes)],
        compiler_params=_cparams(("arbitrary",)), name="in_proj",
    )(h, w_gate_all, w_att_all, w_raw_all, log_lb, log_1m, one_m, *tables)


def _const_spec(a):
    if isinstance(a, tuple):
        stacked, layer = a
        return pl.BlockSpec((None,) + stacked.shape[1:], lambda i, *_: (layer,) + (0,) * (stacked.ndim - 1))
    return pl.BlockSpec(a.shape, lambda i, *_: (0,) * a.ndim)


def _hg_scan_kernel(*refs, reverse, n_chunks, readout):
    if readout:
        q_ref, k_ref, lf_ref, v_ref, of_ref, g_ref, ng_ref, o_ref, st_ref = refs
    else:
        q_ref, k_ref, lf_ref, v_ref, o_ref, st_ref = refs

    @pl.when(pl.program_id(1) == 0)
    def _():
        st_ref[...] = jnp.zeros_like(st_ref)

    c, s = HG_CHUNK, HG_SUB
    nsub = c // s
    chunks = [n_chunks - 1 - cc if reverse else cc for cc in range(n_chunks)]
    items = [(ch, hd) for ch in chunks for hd in range(HG_HEADS)]
    rows = lambda ch: slice(ch * c, (ch + 1) * c)
    cols = lambda hd: slice(hd * HG_DK, (hd + 1) * HG_DK)

    ri = lax.broadcasted_iota(jnp.int32, (c, c), 0)
    ci = lax.broadcasted_iota(jnp.int32, (c, c), 1)
    tri = jnp.where((ci >= ri) if reverse else (ci <= ri), 1.0, 0.0).astype(BF16)
    b_all = {}
    for ch in chunks:
        lf = lf_ref[rows(ch), :]
        hi = lf.astype(BF16)
        r1 = lf - hi.astype(F32)
        mid = r1.astype(BF16)
        lo = (r1 - mid.astype(F32)).astype(BF16)
        b_all[ch] = _dot(tri, hi) + _dot(tri, mid) + _dot(tri, lo)

    qe, k_dec, decay, v16, qt, kt = {}, {}, {}, {}, {}, {}
    for it in items:
        ch, hd = it
        b = b_all[ch][:, cols(hd)]
        q, k = q_ref[rows(ch), cols(hd)], k_ref[rows(ch), cols(hd)]
        btot = b[0:1, :] if reverse else b[c - 1:c, :]
        qe[it] = (q * jnp.exp(b)).astype(BF16)
        k_dec[it] = (k * jnp.exp(btot - b)).astype(BF16)
        decay[it] = jnp.exp(btot)
        v16[it] = v_ref[rows(ch), cols(hd)].astype(BF16)
        for i in range(nsub):
            r0 = i * s
            if reverse:
                k0, k1 = r0, c
                ref = b[r0 + s:r0 + s + 1, :] if i < nsub - 1 else jnp.zeros_like(btot)
            else:
                k0, k1 = 0, r0 + s
                ref = b[r0 - 1:r0, :] if i > 0 else jnp.zeros_like(btot)
            qt[it, i] = (q[r0:r0 + s] * jnp.exp(b[r0:r0 + s] - ref)).astype(BF16)
            kt[it, i] = (k[k0:k1] * jnp.exp(ref - b[k0:k1])).astype(BF16)

    upd = {it: _dot_tn(v16[it], k_dec[it]) for it in items}
    att = {}
    for it in items:
        for i in range(nsub):
            r0 = i * s
            k0 = r0 if reverse else 0
            a = _dot_nt(qt[it, i], kt[it, i])
            rr = lax.broadcasted_iota(jnp.int32, a.shape, 0) + r0
            cc = lax.broadcasted_iota(jnp.int32, a.shape, 1) + k0
            att[it, i] = jnp.where((cc >= rr) if reverse else (cc <= rr), a, 0.0).astype(BF16)

    states = [st_ref[hd] for hd in range(HG_HEADS)]
    o_inter = {}
    for it in items:
        ch, hd = it
        o_inter[it] = _dot_nt(qe[it], states[hd].astype(BF16))
        states[hd] = states[hd] * decay[it] + upd[it]
    for hd in range(HG_HEADS):
        st_ref[hd] = states[hd]

    for it in items:
        ch, hd = it
        outs = []
        for i in range(nsub):
            r0 = i * s
            k0, k1 = (r0, c) if reverse else (0, r0 + s)
            outs.append(o_inter[it][r0:r0 + s] + _dot(att[it, i], v16[it][k0:k1]))
        o = jnp.concatenate(outs, axis=0)
        if readout:
            o = o + of_ref[rows(ch), cols(hd)]
            o = o * lax.rsqrt(jnp.mean(o * o, axis=-1, keepdims=True) + RMS_EPS) * ng_ref[...]
            o = o * _silu(g_ref[rows(ch), cols(hd)])
        o_ref[rows(ch), cols(hd)] = o.astype(o_ref.dtype)


def _hg_scan(p_raw, kk, lf, batch, seq, ctx_len, reverse, readout_args=None):
    n_rows = p_raw.shape[0]
    tb = ctx_len
    nl = seq // tb
    ctx0 = batch * seq // tb
    direction = 1 if reverse else 0

    def row_block(b, i):
        lat = b * nl + (nl - i if reverse else i - 1)
        return jnp.where(i == 0, ctx0 + b, lat)

    def col(j):
        return pl.BlockSpec((tb, HG_W), lambda b, i: (row_block(b, i), j))

    spec = col(0)
    readout = readout_args is not None
    inputs = [p_raw, kk, lf, p_raw]
    in_specs = [col(0), col(direction), col(direction), col(1)]
    if readout:
        o_f, norm_g = readout_args
        inputs += [o_f, p_raw, norm_g]
        in_specs += [spec, col(2), pl.BlockSpec((1, HG_DK), lambda b, i: (0, 0))]
    kern = functools.partial(_hg_scan_kernel, reverse=reverse, n_chunks=tb // HG_CHUNK, readout=readout)
    return pl.pallas_call(
        kern, grid=(batch, nl + 1), in_specs=in_specs, out_specs=spec,
        out_shape=jax.ShapeDtypeStruct((n_rows, HG_W), BF16 if readout else F32),
        scratch_shapes=[pltpu.VMEM((HG_HEADS, HG_DK, HG_DK), F32)],
        compiler_params=_cparams(("arbitrary", "arbitrary")),
        name="hgrn2_scan_bwd_readout" if readout else "hgrn2_scan_fwd",
    )(*inputs)


def _mla_proj_kernel(cq_ref, ckv_ref, kr_ref, qn_ref, kvn_ref, wq_ref, wkv_ref, cm_ref, sm_ref,
                     q_ref, k_ref, v_ref):
    def rms(x, g):
        xf = x.astype(F32)
        return (xf * lax.rsqrt(jnp.mean(xf * xf, axis=-1, keepdims=True) + RMS_EPS) * g).astype(BF16)

    scale = (MLA_NOPE + MLA_ROPE) ** -0.5 * math.log2(math.e)
    _, qs = _col_group_dots(rms(cq_ref[...], qn_ref[...]), wq_ref, MLA_QK_PAD)
    _, kv2 = _col_group_dots(rms(ckv_ref[...], kvn_ref[...]), wkv_ref, 2 * LANES)
    kvs = [g[:, half * LANES:(half + 1) * LANES] for g in kv2 for half in range(2)]
    cm, sm = cm_ref[...], sm_ref[...]
    kr = kr_ref[...]
    lane = lax.broadcasted_iota(jnp.int32, (kr.shape[0], LANES), 1)
    ones_col = jnp.where(lane == 0, 1.0, 0.0).astype(v_ref.dtype)
    for hd in range(MLA_HEADS):
        c0 = hd * MLA_QK_PAD
        q_ref[:, c0:c0 + LANES] = (qs[hd][:, 0:LANES] * scale).astype(q_ref.dtype)
        x = qs[hd][:, LANES:2 * LANES]
        y = (x * cm + _rope_partner(x, MLA_ROPE // 4) * sm) * scale
        q_ref[:, c0 + LANES:c0 + 2 * LANES] = y.astype(q_ref.dtype)
        k_ref[:, c0:c0 + LANES] = kvs[hd].astype(k_ref.dtype)
        k_ref[:, c0 + LANES:c0 + 2 * LANES] = kr
        v0 = hd * MLA_V_PAD
        v_ref[:, v0:v0 + MLA_V] = kvs[MLA_HEADS + hd].astype(v_ref.dtype)
        v_ref[:, v0 + MLA_V:v0 + MLA_V_PAD] = ones_col


def _mla_proj(p_att, q_norm, kv_norm, wq, wkv, cm, sm, rows):
    tm = rows.tm
    n = rows.n_rows
    hq = MLA_HEADS * MLA_QK_PAD

    def col(width, off):
        return pl.BlockSpec((tm, width), lambda i: (i, off // width))

    def whole(a):
        return pl.BlockSpec(a.shape, lambda i: (0, 0))

    return pl.pallas_call(
        _mla_proj_kernel, grid=(rows.n_tiles,),
        in_specs=[col(MLA_Q_LORA, ATT_CQ), col(MLA_KV_LORA, ATT_CKV), col(LANES, ATT_KR),
                  whole(q_norm), whole(kv_norm), whole(wq), whole(wkv),
                  pl.BlockSpec((tm, LANES), lambda i: (i, 0)), pl.BlockSpec((tm, LANES), lambda i: (i, 0))],
        out_specs=[pl.BlockSpec((tm, hq), lambda i: (i, 0)), pl.BlockSpec((tm, hq), lambda i: (i, 0)),
                   pl.BlockSpec((tm, MLA_HEADS * MLA_V_PAD), lambda i: (i, 0))],
        out_shape=[jax.ShapeDtypeStruct((n, hq), BF16), jax.ShapeDtypeStruct((n, hq), BF16),
                   jax.ShapeDtypeStruct((n, MLA_HEADS * MLA_V_PAD), BF16)],
        compiler_params=_cparams(("arbitrary",)), name="mla_proj",
    )(p_att, p_att, p_att, q_norm, kv_norm, wq, wkv, cm, sm)


def _lane_tile_fold(x, op):
    out = x[:, 0:LANES]
    for t in range(1, x.shape[1] // LANES):
        out = op(out, x[:, t * LANES:(t + 1) * LANES])
    return out


def _mla_attn_kernel(*refs, ck, with_lat):
    if with_lat:
        q_ref, kl_ref, vl_ref, kc_ref, vc_ref, o_ref = refs
    else:
        q_ref, kc_ref, vc_ref, o_ref = refs
    chunks = [(kc_ref, vc_ref, 0, kc_ref.shape[0])]
    if with_lat:
        chunks += [(kl_ref, vl_ref, c * ck, ck) for c in range(kl_ref.shape[0] // ck)]
    m = [None] * MLA_HEADS_PER_STEP
    acc = [None] * MLA_HEADS_PER_STEP
    for k_ref, v_ref, r0, n in chunks:
        for hd in range(MLA_HEADS_PER_STEP):
            qk = slice(hd * MLA_QK_PAD, (hd + 1) * MLA_QK_PAD)
            s = _dot_nt(q_ref[:, qk], k_ref[r0:r0 + n, qk])
            m_c = jnp.max(_lane_tile_fold(s, jnp.maximum), axis=-1, keepdims=True)
            m_new = m_c if m[hd] is None else jnp.maximum(m[hd], m_c)
            pv = _dot(jnp.exp2(s - m_new).astype(BF16), v_ref[r0:r0 + n, hd * MLA_V_PAD:(hd + 1) * MLA_V_PAD])
            acc[hd] = pv if m[hd] is None else acc[hd] * jnp.exp2(m[hd] - m_new) + pv
            m[hd] = m_new
    for hd in range(MLA_HEADS_PER_STEP):
        o = acc[hd][:, 0:MLA_V] / acc[hd][:, MLA_V:MLA_V + 1]
        o_ref[:, hd * MLA_V:(hd + 1) * MLA_V] = o.astype(o_ref.dtype)


def _mla_attn(q, k, v, batch, seq, ctx_len, latent):
    ctx0 = batch * seq // ctx_len
    ck = 1024 if seq % 1024 == 0 else ctx_len
    hps = MLA_HEADS_PER_STEP
    if latent:
        tq = MLA_TQ if seq % MLA_TQ == 0 else ctx_len
        nq = seq // tq
        q_row = lambda b, h, i: b * nq + i
        o_row, n_rows = q_row, batch * seq
    else:
        tq, nq = ctx_len, 1
        q_row = lambda b, h, i: ctx0 + b
        o_row, n_rows = (lambda b, h, i: b), batch * ctx_len
    in_specs = [pl.BlockSpec((tq, hps * MLA_QK_PAD), lambda b, h, i: (q_row(b, h, i), h))]
    inputs = [q]
    if latent:
        in_specs += [pl.BlockSpec((seq, hps * MLA_QK_PAD), lambda b, h, i: (b, h)),
                     pl.BlockSpec((seq, hps * MLA_V_PAD), lambda b, h, i: (b, h))]
        inputs += [k, v]
    in_specs += [pl.BlockSpec((ctx_len, hps * MLA_QK_PAD), lambda b, h, i: (ctx0 + b, h)),
                 pl.BlockSpec((ctx_len, hps * MLA_V_PAD), lambda b, h, i: (ctx0 + b, h))]
    inputs += [k, v]
    return pl.pallas_call(
        functools.partial(_mla_attn_kernel, ck=ck, with_lat=latent),
        grid=(batch, MLA_HEADS // hps, nq), in_specs=in_specs,
        out_specs=pl.BlockSpec((tq, hps * MLA_V), lambda b, h, i: (o_row(b, h, i), h)),
        out_shape=jax.ShapeDtypeStruct((n_rows, MLA_HEADS * MLA_V), BF16),
        compiler_params=_cparams(("arbitrary", "arbitrary", "arbitrary")),
        name="mla_attn_lat" if latent else "mla_attn_ctx",
    )(*inputs)


def _swa_kernel(sink_ref, q_ref, kp_ref, kn_ref, kx_ref, kc_ref, vp_ref, vn_ref, vx_ref, vc_ref, o_ref,
                *, n_lat_steps, seq, ctx_queries):
    n = pl.program_id(1)
    g = SWA_HEADS // SWA_KV_HEADS
    blk = SWA_BLOCK
    nblk = SWA_BLOCKS_PER_STEP

    def attend(band):
        items = [(j, kh) for j in range(nblk) for kh in range(SWA_KV_HEADS)]
        ks = [slice(kh * SWA_HD, (kh + 1) * SWA_HD) for kh in range(SWA_KV_HEADS)]
        row = lax.broadcasted_iota(jnp.int32, (g * blk, 1), 0)
        if band:
            a = lax.broadcasted_iota(jnp.int32, (g * blk, 3 * blk), 0) & (blk - 1)
            jj = lax.broadcasted_iota(jnp.int32, (g * blk, 3 * blk), 1)
            in_window = jnp.abs(jj - blk - a) <= SWA_WINDOW
            k_slab = [jnp.concatenate([kp_ref[:, s], kn_ref[:, s], kx_ref[:, s]], axis=0) for s in ks]
            v_slab = [jnp.concatenate([vp_ref[:, s], vn_ref[:, s], vx_ref[:, s]], axis=0) for s in ks]
        sink = []
        for kh in range(SWA_KV_HEADS):
            sk = jnp.zeros((g * blk, 1), F32)
            for gi in range(g):
                sk = jnp.where((row >= gi * blk) & (row < (gi + 1) * blk), sink_ref[kh * g + gi], sk)
            sink.append(sk)
        s_c, s_b = {}, {}
        for it in items:
            j, kh = it
            qs = jnp.concatenate([q_ref[j * blk:(j + 1) * blk, (kh * g + gi) * SWA_HD:(kh * g + gi + 1) * SWA_HD]
                                  for gi in range(g)], axis=0)
            s_c[it] = _dot_nt(qs, kc_ref[:, ks[kh]])
            if band:
                key_pos = (n * nblk + j - 1) * blk + jj
                keep = in_window & (key_pos >= 0) & (key_pos < seq)
                s_b[it] = jnp.where(keep, _dot_nt(qs, k_slab[kh][j * blk:(j + 3) * blk]), -jnp.inf)
        p_c, p_b, den = {}, {}, {}
        for it in items:
            m = jnp.maximum(jnp.max(s_c[it], axis=-1, keepdims=True), sink[it[1]])
            if band:
                m = jnp.maximum(m, jnp.max(s_b[it], axis=-1, keepdims=True))
            pc = jnp.exp(s_c[it] - m)
            dn = jnp.sum(pc, axis=-1, keepdims=True) + jnp.exp(sink[it[1]] - m)
            if band:
                pb = jnp.exp(s_b[it] - m)
                dn = dn + jnp.sum(pb, axis=-1, keepdims=True)
                p_b[it] = pb.astype(BF16)
            p_c[it] = pc.astype(BF16)
            den[it] = dn
        for it in items:
            j, kh = it
            acc = _dot(p_c[it], vc_ref[:, ks[kh]])
            if band:
                acc = acc + _dot(p_b[it], v_slab[kh][j * blk:(j + 3) * blk])
            o = acc / den[it]
            for gi in range(g):
                c0 = (kh * g + gi) * SWA_HD
                o_ref[j * blk:(j + 1) * blk, c0:c0 + SWA_HD] = o[gi * blk:(gi + 1) * blk].astype(o_ref.dtype)

    if ctx_queries:
        pl.when(n < n_lat_steps)(lambda: attend(True))
        pl.when(n >= n_lat_steps)(lambda: attend(False))
    else:
        attend(True)


def _swa(p_att, sink, batch, seq, ctx_len, ctx_queries):
    blk = SWA_BLOCK
    nblk = SWA_BLOCKS_PER_STEP
    tq = nblk * blk
    assert seq % tq == 0 and ctx_len % tq == 0
    nb = seq // blk
    ns, ncs = seq // tq, ctx_len // tq
    ctx0 = batch * seq // ctx_len
    kcol, vcol = ATT_SK // SWA_KV_W, ATT_SV // SWA_KV_W
    q_row = lambda b, n: jnp.where(n < ns, b * ns + n, batch * ns + b * ncs + (n - ns))
    n_rows = batch * seq + (batch * ctx_len if ctx_queries else 0)

    def edge(col, first):
        def index(b, n):
            k = jnp.clip(n, 0, ns - 1) * nblk + (-1 if first else nblk)
            return (b * nb + jnp.clip(k, 0, nb - 1), col)
        return pl.BlockSpec((blk, SWA_KV_W), index)

    def own(col):
        return pl.BlockSpec((tq, SWA_KV_W), lambda b, n: (b * ns + jnp.clip(n, 0, ns - 1), col))

    ctx_k = pl.BlockSpec((ctx_len, SWA_KV_W), lambda b, n: (ctx0 + b, kcol))
    ctx_v = pl.BlockSpec((ctx_len, SWA_KV_W), lambda b, n: (ctx0 + b, vcol))
    return pl.pallas_call(
        functools.partial(_swa_kernel, n_lat_steps=ns, seq=seq, ctx_queries=ctx_queries),
        grid=(batch, ns + (ncs if ctx_queries else 0)),
        in_specs=[pl.BlockSpec(memory_space=pltpu.SMEM),
                  pl.BlockSpec((tq, SWA_W), lambda b, n: (q_row(b, n), 0)),
                  edge(kcol, True), own(kcol), edge(kcol, False), ctx_k,
                  edge(vcol, True), own(vcol), edge(vcol, False), ctx_v],
        out_specs=pl.BlockSpec((tq, SWA_W), lambda b, n: (q_row(b, n), 0)),
        out_shape=jax.ShapeDtypeStruct((n_rows, SWA_W), BF16),
        compiler_params=_cparams(("arbitrary", "arbitrary")),
        name="swa_attn",
    )(sink, *([p_att] * 9))


def _route(logits, rb):
    aff = _sigmoid(logits)
    sel = aff + rb
    tm = logits.shape[1]
    scores = []
    for g in range(N_GROUPS):
        r = [sel[g * E_PER_GROUP + j:g * E_PER_GROUP + j + 1] for j in range(E_PER_GROUP)]
        best = None
        for i in range(E_PER_GROUP):
            for j in range(i + 1, E_PER_GROUP):
                pair = r[i] + r[j]
                best = pair if best is None else jnp.maximum(best, pair)
        scores.append(best)
    gbest, gi = scores[0], jnp.zeros((1, tm), jnp.int32)
    for g in range(1, N_GROUPS):
        upd = scores[g] > gbest
        gbest = jnp.where(upd, scores[g], gbest)
        gi = jnp.where(upd, g, gi)
    s_in, a_in = [], []
    for j in range(E_PER_GROUP):
        sj, aj = sel[j:j + 1], aff[j:j + 1]
        for g in range(1, N_GROUPS):
            e = g * E_PER_GROUP + j
            sj = jnp.where(gi == g, sel[e:e + 1], sj)
            aj = jnp.where(gi == g, aff[e:e + 1], aj)
        s_in.append(sj)
        a_in.append(aj)
    chosen = []
    for j in range(E_PER_GROUP):
        rank = jnp.zeros((1, tm), jnp.int32)
        for k in range(E_PER_GROUP):
            if k == j:
                continue
            ahead = (s_in[k] >= s_in[j]) if k < j else (s_in[k] > s_in[j])
            rank = rank + ahead.astype(jnp.int32)
        chosen.append(rank < 2)
    w = [jnp.where(chosen[j], a_in[j], 0.0) for j in range(E_PER_GROUP)]
    wsum = w[0] + w[1] + w[2] + w[3]
    gate_in = [wj / wsum * ROUTE_SCALE for wj in w]
    code = sum(jnp.where(chosen[j], 1 << j, 0) for j in range(E_PER_GROUP))
    pair = jnp.zeros((1, tm), jnp.int32)
    for idx, cval in enumerate((3, 5, 9, 6, 10, 12)):
        pair = jnp.where(code == cval, idx, pair)
    bucket = gi * N_PAIRS + pair
    g_lo = jnp.zeros((1, tm), F32)
    g_hi = jnp.zeros((1, tm), F32)
    seen = jnp.zeros((1, tm), jnp.bool_)
    for j in range(E_PER_GROUP):
        g_lo = jnp.where(chosen[j] & ~seen, gate_in[j], g_lo)
        g_hi = jnp.where(chosen[j] & seen, gate_in[j], g_hi)
        seen = seen | chosen[j]
    return g_lo, g_hi, bucket


def _lat_or_ctx(lat_ref, ctx_ref, rs, lat_tiles):
    if ctx_ref is None:
        return lat_ref[rs, :]
    return jnp.where(pl.program_id(0) < lat_tiles, lat_ref[rs, :], ctx_ref[rs, :])


def _dual_specs(rows, width):
    return [pl.BlockSpec((rows.tm, width), lambda i, *_: (jnp.minimum(i, rows.lat_tiles - 1), 0)),
            pl.BlockSpec((rows.tm, width), lambda i, *_: (jnp.maximum(i - rows.lat_tiles, 0), 0),
                         pipeline_mode=pl.Buffered(1))]


def _out_kernel(*refs, alpha, dual, lat_tiles):
    if dual:
        (ohg_ref, omla_ref, omlac_ref, oswa_ref, x_ref, xc_ref, g1_ref, sh2_ref, sc2_ref, lng_ref, lnb_ref,
         w_ref, rwt_ref, rb_ref, x1_ref, h2_ref, bucket_ref) = refs
    else:
        (ohg_ref, omla_ref, oswa_ref, x_ref, g1_ref, sh2_ref, sc2_ref, lng_ref, lnb_ref,
         w_ref, rwt_ref, rb_ref, x1_ref, h2_ref, bucket_ref) = refs
        omlac_ref = xc_ref = None
    o1, o2 = HG_W, HG_W + MLA_HEADS * MLA_V
    d = x_ref.shape[1]
    tm = x_ref.shape[0]
    halves = [slice(0, tm // 2), slice(tm // 2, tm)]
    mixes = [_dot(ohg_ref[rs, :], w_ref[0:o1, :])
             + _dot(_lat_or_ctx(omla_ref, omlac_ref, rs, lat_tiles), w_ref[o1:o2, :])
             + _dot(oswa_ref[rs, :], w_ref[o2:, :]) for rs in halves]
    for rs, mix in zip(halves, mixes):
        x_in = _lat_or_ctx(x_ref, xc_ref, rs, lat_tiles)
        x1 = _ln_rows(alpha * x_in + g1_ref[...] * mix) * lng_ref[...] + lnb_ref[...]
        x1_ref[rs, :] = x1
        h2 = (_ln_rows(x1) * (1.0 + sc2_ref[...]) + sh2_ref[...]).astype(BF16)
        h2_ref[rs, 0:d] = h2.astype(h2_ref.dtype)
        g_lo, g_hi, bucket = _route(_dot_nt(rwt_ref[...], h2), rb_ref[...])
        bucket_ref[:, rs] = bucket
        gate_rows = jnp.concatenate([g_lo, g_hi, jnp.zeros((LANES - 2, tm // 2), F32)], axis=0)
        h2_ref[rs, d:d + LANES] = gate_rows.T


def _out_proj(o_hg, o_mla, o_swa, x, mod, ln_g, ln_b, w_out, rwt, rb, rows, alpha):
    tm = rows.tm
    dual = isinstance(x, tuple)
    d = (x[0] if dual else x).shape[1]
    n = rows.n_rows
    row = lambda w: pl.BlockSpec((tm, w), lambda i: (i, 0))
    whole = lambda a: pl.BlockSpec(a.shape, lambda i: (0, 0), pipeline_mode=pl.Buffered(1))
    mla_w = MLA_HEADS * MLA_V
    if dual:
        in_specs = [row(HG_W)] + _dual_specs(rows, mla_w) + [row(SWA_W)] + _dual_specs(rows, d)
        inputs = [o_hg, *o_mla, o_swa, *x]
    else:
        in_specs = [row(HG_W), row(mla_w), row(SWA_W), row(d)]
        inputs = [o_hg, o_mla, o_swa, x]
    in_specs += [_mod_spec(rows, 2, d), _mod_spec(rows, 3, d), _mod_spec(rows, 4, d),
                 whole(ln_g), whole(ln_b), whole(w_out), whole(rwt), whole(rb)]
    inputs += [mod, mod, mod, ln_g, ln_b, w_out, rwt, rb]
    return pl.pallas_call(
        functools.partial(_out_kernel, alpha=alpha, dual=dual, lat_tiles=rows.lat_tiles), grid=(rows.n_tiles,),
        in_specs=in_specs,
        out_specs=[row(d), row(d + LANES), pl.BlockSpec((1, tm), lambda i: (0, i))],
        out_shape=[jax.ShapeDtypeStruct((n, d), F32), jax.ShapeDtypeStruct((n, d + LANES), F32),
                   jax.ShapeDtypeStruct((1, n), jnp.int32)],
        compiler_params=_cparams(("arbitrary",)), name="out_proj_ln_router",
    )(*inputs)


def _row_gather_start(idx_ref, base, src_hbm, dst, sem, n, static_rows=False):
    def issue(r, carry, priority=0):
        pltpu.make_async_copy(src_hbm.at[pl.ds(idx_ref[base + r], 1)], dst.at[pl.ds(r, 1)], sem).start(priority)
        return carry
    if static_rows:
        for r in range(n):
            issue(r, 0, priority=r % 2)
    else:
        lax.fori_loop(0, n, issue, 0, unroll=8)


def _row_gather_start_next(t, n_valid, idx_ref, base, src_hbm, buf, sem, n):
    for half in range(2):
        @pl.when((t < n_valid) & (t % 2 == half))
        def _():
            _row_gather_start(idx_ref, base, src_hbm, buf.at[half], sem.at[half], n, static_rows=True)


def _row_gather_wait(src_hbm, dst, sem, n):
    pltpu.make_async_copy(src_hbm.at[pl.ds(0, n)], dst, sem).wait()


def _moe_kernel(stok_ref, off_ref, e1_ref, e2_ref, nused_ref, h_hbm, wg1_ref, wu1_ref, wd1_ref,
                wg2_ref, wu2_ref, wd2_ref, y_ref, hbuf, sem, *, tm):
    i = pl.program_id(0)
    n_used = nused_ref[0]
    d = y_ref.shape[1]

    @pl.when(i == 0)
    def _():
        _row_gather_start(stok_ref, off_ref[0], h_hbm, hbuf.at[0], sem.at[0], tm)

    nxt = jnp.minimum(i + 1, pl.num_programs(0) - 1)
    _row_gather_start_next(i + 1, n_used, stok_ref, off_ref[nxt], h_hbm, hbuf, sem, tm)

    @pl.when(i < n_used)
    def _():
        slot = i % 2
        _row_gather_wait(h_hbm, hbuf.at[slot], sem.at[slot], tm)
        h = hbuf[slot, :, 0:d].astype(BF16)
        g_lo = hbuf[slot, :, d:d + 1]
        g_hi = hbuf[slot, :, d + 1:d + 2]

        act1 = (_silu(_dot(h, wg1_ref[...])) * _dot(h, wu1_ref[...])).astype(BF16)
        act2 = (_silu(_dot(h, wg2_ref[...])) * _dot(h, wu2_ref[...])).astype(BF16)
        y_ref[...] = g_lo * _dot(act1, wd1_ref[...]) + g_hi * _dot(act2, wd2_ref[...])

    @pl.when(i >= n_used)
    def _():
        y_ref[...] = jnp.zeros_like(y_ref)


def _moe(h2, stok, off, e1, e2, n_used, wg, wu, wd, layer, tm):
    n_tiles = off.shape[0]
    dx = h2.shape[1]
    d, f = wg.shape[2], wg.shape[3]
    last = lambda i, nu: jnp.minimum(i, nu[0] - 1)
    wspec_in = lambda sel: pl.BlockSpec(
        (None, None, d, f), lambda i, s, o, a, b, nu: (layer, (a, b)[sel][last(i, nu)], 0, 0))
    wspec_dn = lambda sel: pl.BlockSpec(
        (None, None, f, d), lambda i, s, o, a, b, nu: (layer, (a, b)[sel][last(i, nu)], 0, 0))
    grid_spec = pltpu.PrefetchScalarGridSpec(
        num_scalar_prefetch=5, grid=(n_tiles,),
        in_specs=[pl.BlockSpec(memory_space=pl.ANY),
                  wspec_in(0), wspec_in(0), wspec_dn(0), wspec_in(1), wspec_in(1), wspec_dn(1)],
        out_specs=pl.BlockSpec((tm, d), lambda i, s, o, a, b, nu: (i, 0)),
        scratch_shapes=[pltpu.VMEM((2, tm, dx), F32), pltpu.SemaphoreType.DMA((2,))])
    return pl.pallas_call(
        functools.partial(_moe_kernel, tm=tm), grid_spec=grid_spec,
        out_shape=jax.ShapeDtypeStruct((n_tiles * tm, d), F32),
        compiler_params=_cparams(("arbitrary",)), name="moe_grouped",
    )(stok, off, e1, e2, n_used, h2, wg, wu, wd, wg, wu, wd)


def _moe_plan(bucket, n_tokens, tm):
    tok = jnp.arange(n_tokens, dtype=jnp.int32)
    buckets = jnp.arange(N_BUCKETS, dtype=jnp.int32)
    skey, stok = lax.sort((bucket * n_tokens + tok, tok), num_keys=1)
    counts = jnp.sum((bucket[:, None] == buckets[None, :]).astype(jnp.int32), axis=0)
    padded = (counts + tm - 1) // tm * tm
    ends = jnp.cumsum(padded)
    shift = (ends - padded) - (jnp.cumsum(counts) - counts)
    sbucket = skey // n_tokens
    slot = tok + jnp.sum(jnp.where(sbucket[:, None] == buckets[None, :], shift[None, :], 0), axis=1)
    _, pos = lax.sort((stok, slot), num_keys=1)
    n_tiles = n_tokens // tm + N_BUCKETS
    tile_start = jnp.arange(n_tiles, dtype=jnp.int32) * tm
    tile_bucket = jnp.sum((ends[None, :] <= tile_start[:, None]).astype(jnp.int32), axis=1)
    tile_bucket = jnp.minimum(tile_bucket, N_BUCKETS - 1)
    tshift = jnp.sum(jnp.where(tile_bucket[:, None] == buckets[None, :], shift[None, :], 0), axis=1)
    off = jnp.clip(tile_start - tshift, 0, n_tokens)
    grp, pair = tile_bucket // N_PAIRS, tile_bucket % N_PAIRS
    lo = jnp.where(pair < 3, 0, jnp.where(pair < 5, 1, 2))
    hi = jnp.where(pair == 0, 1, jnp.where((pair == 1) | (pair == 3), 2, 3))
    e1, e2 = grp * E_PER_GROUP + lo, grp * E_PER_GROUP + hi
    n_used = (ends[-1] // tm).astype(jnp.int32).reshape(1)
    stok = jnp.concatenate([stok, jnp.zeros((tm,), jnp.int32)])
    return stok, off.astype(jnp.int32), pos.astype(jnp.int32), e1.astype(jnp.int32), e2.astype(jnp.int32), n_used


def _ln2_kernel(pos_ref, y_hbm, x1_ref, g2_ref, lng_ref, lnb_ref, *rest, alpha, tm, emit_h):
    if emit_h:
        sh_ref, sc_ref, x2_ref, h_ref, ybuf, sem = rest
    else:
        x2_ref, ybuf, sem = rest
    i = pl.program_id(0)
    n_tiles = pl.num_programs(0)

    @pl.when(i == 0)
    def _():
        _row_gather_start(pos_ref, 0, y_hbm, ybuf.at[0], sem.at[0], tm)

    _row_gather_start_next(i + 1, n_tiles, pos_ref, (i + 1) * tm, y_hbm, ybuf, sem, tm)

    slot = i % 2
    _row_gather_wait(y_hbm, ybuf.at[slot], sem.at[slot], tm)
    x2 = _ln_rows(alpha * x1_ref[...] + g2_ref[...] * ybuf[slot]) * lng_ref[...] + lnb_ref[...]
    x2_ref[...] = x2
    if emit_h:
        h_ref[...] = (_ln_rows(x2) * (1.0 + sc_ref[...]) + sh_ref[...]).astype(h_ref.dtype)


def _ln2(pos, y_sorted, x1, mod, mod_next, ln_g, ln_b, rows, alpha):
    tm = rows.tm
    d = x1.shape[1]
    emit_h = mod_next is not None
    row = pl.BlockSpec((tm, d), lambda i, p: (i, 0))
    whole = lambda a: pl.BlockSpec(a.shape, lambda i, p: (0, 0))
    in_specs = [pl.BlockSpec(memory_space=pl.ANY), row, _mod_spec(rows, 5, d), whole(ln_g), whole(ln_b)]
    inputs = [y_sorted, x1, mod, ln_g, ln_b]
    out_specs = [row]
    out_shape = [jax.ShapeDtypeStruct((rows.n_rows, d), F32)]
    if emit_h:
        in_specs += [_mod_spec(rows, 0, d), _mod_spec(rows, 1, d)]
        inputs += [mod_next, mod_next]
        out_specs.append(row)
        out_shape.append(jax.ShapeDtypeStruct((rows.n_rows, d), BF16))
    grid_spec = pltpu.PrefetchScalarGridSpec(
        num_scalar_prefetch=1, grid=(rows.n_tiles,), in_specs=in_specs, out_specs=out_specs,
        scratch_shapes=[pltpu.VMEM((2, tm, d), F32), pltpu.SemaphoreType.DMA((2,))])
    out = pl.pallas_call(
        functools.partial(_ln2_kernel, alpha=alpha, tm=tm, emit_h=emit_h), grid_spec=grid_spec,
        out_shape=out_shape, compiler_params=_cparams(("arbitrary",)), name="unpermute_ln2",
    )(pos, *inputs)
    return out if emit_h else (out[0], None)


def _rope_tables(seq, batch, ctx_rows, dim, pad_to):
    rows = seq // GRID_W
    row = jnp.repeat(jnp.arange(rows, dtype=jnp.int32), GRID_W)
    col = jnp.tile(jnp.arange(GRID_W, dtype=jnp.int32), rows)
    nf = dim // 4
    inv_freq = ROPE_BASE ** (-jnp.arange(nf, dtype=F32) / nf)
    ang = jnp.stack([row, col], -1).astype(F32)[:, :, None] * inv_freq
    cos, sin = jnp.cos(ang), jnp.sin(ang)
    c = jnp.stack([cos, cos], axis=2).reshape(seq, dim)
    s = jnp.stack([-sin, sin], axis=2).reshape(seq, dim)
    if pad_to > dim:
        c = jnp.pad(c, ((0, 0), (0, pad_to - dim)))
        s = jnp.pad(s, ((0, 0), (0, pad_to - dim)))
    ctx_c = jnp.zeros((ctx_rows, pad_to), F32).at[:, :dim].set(1.0)
    c = jnp.concatenate([jnp.tile(c, (batch, 1)), ctx_c], axis=0)
    s = jnp.concatenate([jnp.tile(s, (batch, 1)), jnp.zeros((ctx_rows, pad_to), F32)], axis=0)
    return c, s


def _w_in_prep_kernel(w_ref, raw_ref, gate_ref, att_ref):
    o1 = 5 * HG_W
    o2 = o1 + MLA_Q_LORA + MLA_KV_LORA
    o3 = o2 + MLA_ROPE
    cast = lambda a, b: w_ref[:, a:b].astype(BF16)
    raw_ref[:, 0:HG_W] = cast(0, HG_W)
    raw_ref[:, HG_W:3 * HG_W] = cast(3 * HG_W, o1)
    gate_ref[...] = cast(HG_W, 3 * HG_W)
    att_ref[:, ATT_SQ:ATT_CQ] = cast(o3, o3 + SWA_W + 2 * SWA_KV_W)
    att_ref[:, ATT_CQ:ATT_KR] = cast(o1, o2)
    att_ref[:, ATT_KR:ATT_KR + MLA_ROPE] = cast(o2, o3)
    att_ref[:, ATT_KR + MLA_ROPE:ATT_COLS] = jnp.zeros((w_ref.shape[0], LANES - MLA_ROPE), BF16)


def _w_in_prep(w_in):
    depth, d, n = w_in.shape
    tr = 256 if d % 256 == 0 else d
    widths = (3 * HG_W, 2 * HG_W, ATT_COLS)
    return pl.pallas_call(
        _w_in_prep_kernel, grid=(depth, d // tr),
        in_specs=[pl.BlockSpec((None, tr, n), lambda l, i: (l, i, 0))],
        out_specs=[pl.BlockSpec((None, tr, w), lambda l, i: (l, i, 0)) for w in widths],
        out_shape=[jax.ShapeDtypeStruct((depth, d, w), BF16) for w in widths],
        compiler_params=_cparams(("arbitrary", "arbitrary")), name="w_in_prep",
    )(w_in)


def _mla_weights(w_uq, w_ukv):
    qk = MLA_NOPE + MLA_ROPE
    wq = w_uq.reshape(MLA_Q_LORA, MLA_HEADS, qk)
    wq = jnp.pad(wq, ((0, 0), (0, 0), (0, MLA_QK_PAD - qk))).reshape(MLA_Q_LORA, MLA_HEADS * MLA_QK_PAD)
    wkv = w_ukv.reshape(MLA_KV_LORA, MLA_HEADS, MLA_NOPE + MLA_V)
    wkv = jnp.concatenate([wkv[:, :, :MLA_NOPE].reshape(MLA_KV_LORA, -1), wkv[:, :, MLA_NOPE:].reshape(MLA_KV_LORA, -1)], axis=1)
    return wq.astype(BF16), wkv.astype(BF16)


def kernel(x, c, ctx, c_ctx, w_ada, b_ada, w_in, w_out, hg_lb_logits, hg_norm_g, mla_q_norm, mla_kv_norm,
           mla_w_uq, mla_w_ukv, swa_sink, ln1_g, ln1_b, ln2_g, ln2_b, router_w, router_b,
           moe_w_gate, moe_w_up, moe_w_down):
    batch, seq, d = x.shape
    ctx_len = ctx.shape[1]
    depth = w_ada.shape[0]
    alpha = (2.0 * depth) ** 0.25
    n_lat, n_ctx = batch * seq, batch * ctx_len
    n_all = n_lat + n_ctx
    assert batch + 1 <= 8 and ctx_len % (4 * HG_CHUNK) == 0 and seq % ctx_len == 0 and seq % GRID_W == 0

    c8 = jnp.zeros((8, d), F32).at[:batch].set(c).at[batch].set(c_ctx)
    mod_all = _ada(c8, w_ada, b_ada).reshape(depth, 8 * 6, 1, d)

    lb = jnp.cumsum(jax.nn.softmax(hg_lb_logits.astype(F32), axis=0), axis=0)
    lb = (lb - lb[0:1]).reshape(depth, 1, 2 * HG_W)
    log_lb, log_1m, one_m = jnp.log(lb), jnp.log1p(-lb), 1.0 - lb

    cs, ss = _rope_tables(seq, batch, n_ctx, SWA_HD, SWA_HD)
    cm, sm = _rope_tables(seq, batch, n_ctx, MLA_ROPE, LANES)
    rwt = router_w.T.astype(BF16)
    rb = router_b.astype(F32).reshape(N_EXPERTS, 1)

    tm_all = _pick_tm(seq, n_ctx, 512)
    rows_all = _Rows(batch, seq, ctx_len, n_all, tm_all)
    rows_lat = _Rows(batch, seq, ctx_len, n_lat, tm_all)
    moe_tm = 256

    xa = (x.reshape(n_lat, d), ctx.reshape(n_ctx, d))
    h = _lnmod(xa[0], xa[1], mod_all[0], rows_all, 0)
    w_raw_all, w_gate_all, w_att_all = _w_in_prep(w_in)
    wg_all, wu_all, wd_all = moe_w_gate.astype(BF16), moe_w_up.astype(BF16), moe_w_down.astype(BF16)

    for layer in range(depth):
        need_ctx = layer < depth - 1
        mod = mod_all[layer]
        rows = rows_all if need_ctx else rows_lat
        wq, wkv = _mla_weights(mla_w_uq[layer], mla_w_ukv[layer])

        lf, kk, p_att, p_raw = _in_proj(h, w_gate_all, w_att_all, w_raw_all, layer, log_lb[layer], log_1m[layer],
                                        one_m[layer], [cs, ss, cm, sm], rows_all)

        o_f = _hg_scan(p_raw, kk, lf, batch, seq, ctx_len, False)
        o_hg = _hg_scan(p_raw, kk, lf, batch, seq, ctx_len, True,
                        (o_f, hg_norm_g[layer].astype(F32).reshape(1, HG_DK)))

        q_mla, k_mla, v_mla = _mla_proj(p_att, mla_q_norm[layer].astype(F32).reshape(1, -1),
                                        mla_kv_norm[layer].astype(F32).reshape(1, -1), wq, wkv, cm, sm, rows_all)
        n_out = rows.n_rows
        o_mla = _mla_attn(q_mla, k_mla, v_mla, batch, seq, ctx_len, True)
        o_swa = _swa(p_att, swa_sink[layer].astype(F32), batch, seq, ctx_len, need_ctx)
        dual = isinstance(xa, tuple)
        if need_ctx:
            o_mla = (o_mla, _mla_attn(q_mla, k_mla, v_mla, batch, seq, ctx_len, False))
            if not dual:
                xa = (xa[:n_lat], xa[n_lat:])
        elif dual:
            xa = xa[0]

        x1, h2, bucket = _out_proj(
            o_hg, o_mla, o_swa, xa, mod, ln1_g[layer].astype(F32).reshape(1, d), ln1_b[layer].astype(F32).reshape(1, d),
            w_out[layer].astype(BF16), rwt, rb, rows, alpha)

        stok, off, pos, e1, e2, n_used = _moe_plan(bucket[0], n_out, moe_tm)
        y_sorted = _moe(h2, stok, off, e1, e2, n_used, wg_all, wu_all, wd_all, layer, moe_tm)
        xa, h = _ln2(pos, y_sorted, x1, mod, mod_all[layer + 1] if need_ctx else None,
                     ln2_g[layer].astype(F32).reshape(1, d), ln2_b[layer].astype(F32).reshape(1, d), rows, alpha)

    return xa[:n_lat].reshape(batch, seq, d)
```
